```python
import math
import functools
import jax
import jax.numpy as jnp
from jax import lax
import numpy as np

D_MODEL = 1024
BATCH = 32
SEQ = 2048
DEPTH = 1
DEC_BATCH = 32
DEC_SEQ = 64
PAST_LEN = 1024

CHUNK = 64
Q_BLOCK = 128
RET_HEADS = 4
RET_QK_DIM = 128
RET_V_DIM = 256
ROPE_BASE = 10000.0
FOX_HEADS = 16
FOX_HEAD_DIM = 64
N_GROUPS = 4
EXPERTS_PER_GROUP = 8
N_EXPERTS = N_GROUPS * EXPERTS_PER_GROUP
TOP_K = 2
EXPERT_DIM = 512
MOE_BLOCK = 256
RET_QK_W = RET_HEADS * RET_QK_DIM
RET_V_W = RET_HEADS * RET_V_DIM
FOX_W = FOX_HEADS * FOX_HEAD_DIM
FGATE_OFFSET = 2 * RET_QK_W + 2 * RET_V_W + 3 * FOX_W
N_IN = FGATE_OFFSET + FOX_HEADS + 2 * D_MODEL
DEEPNORM_ALPHA = (2 * DEPTH) ** 0.25
DEEPNORM_BETA = (8 * DEPTH) ** -0.25
LN_EPS = 1e-5
GN_EPS = 1e-6

kernel_name = "hybrid_retention_fox_hmoe_stream_step"


def _split_in(z):
    sizes = (RET_QK_W, RET_QK_W, RET_V_W, RET_V_W, FOX_W, FOX_W, FOX_W, FOX_HEADS, D_MODEL, D_MODEL)
    cuts = [int(c) for c in np.cumsum(sizes)[:-1]]
    return jnp.split(z, cuts, axis=-1)


def layer_norm(x, w, b):
    xf = x.astype(jnp.float32)
    mu = jnp.mean(xf, axis=-1, keepdims=True)
    var = jnp.mean(jnp.square(xf - mu), axis=-1, keepdims=True)
    return ((xf - mu) * lax.rsqrt(var + LN_EPS) * w + b).astype(x.dtype)


def head_group_norm(o, w, b):
    mu = jnp.mean(o, axis=-1, keepdims=True)
    var = jnp.mean(jnp.square(o - mu), axis=-1, keepdims=True)
    on = (o - mu) * lax.rsqrt(var + GN_EPS)
    return on.reshape(o.shape[0], o.shape[1], -1) * w + b


def rotary(x, pos):
    half = x.shape[-1] // 2
    inv = ROPE_BASE ** (-jnp.linspace(0.0, 1.0, half, dtype=jnp.float32))
    ang = pos.astype(jnp.float32)[:, None] * inv[None, :]
    cos = jnp.cos(ang)[None, :, None, :]
    sin = jnp.sin(ang)[None, :, None, :]
    x1, x2 = x[..., :half], x[..., half:]
    return jnp.concatenate([x1 * cos - x2 * sin, x1 * sin + x2 * cos], axis=-1)


def retention_log_decay():
    return jnp.log1p(-jnp.exp(jnp.linspace(math.log(1.0 / 32), math.log(1.0 / 512), RET_HEADS, dtype=jnp.float32)))


def ret_chunk(q, k, v, state, log_gamma):
    L = q.shape[1]
    n = jnp.arange(L, dtype=jnp.float32)
    diff = n[:, None] - n[None, :]
    decay = jnp.where(diff[None] >= 0, jnp.exp(jnp.maximum(diff, 0.0)[None] * log_gamma[:, None, None]), 0.0)
    scores = jnp.einsum('bnhk,bmhk->bhnm', q, k) * decay
    inner = jnp.einsum('bhnm,bmhv->bnhv', scores, v)
    q_decay = jnp.exp((n + 1.0)[:, None] * log_gamma[None, :])
    cross = jnp.einsum('bnhk,bhkv->bnhv', q, state) * q_decay[None, :, :, None]
    k_decay = jnp.exp((L - 1.0 - n)[:, None] * log_gamma[None, :])
    new_state = (jnp.exp(L * log_gamma)[None, :, None, None] * state
                 + jnp.einsum('bmhk,bmhv->bhkv', k * k_decay[None, :, :, None], v))
    return inner + cross, new_state


def retention_prompt(q, k, v, log_gamma):
    B, S, H, dk = q.shape
    dv = v.shape[-1]
    nc = S // CHUNK

    def to_chunks(t):
        return t.reshape(B, nc, CHUNK, *t.shape[2:]).swapaxes(0, 1)

    def step(state, qkv):
        o, state = ret_chunk(qkv[0], qkv[1], qkv[2], state, log_gamma)
        return state, o

    state0 = jnp.zeros((B, H, dk, dv), jnp.float32)
    state, o = lax.scan(step, state0, (to_chunks(q), to_chunks(k), to_chunks(v)))
    return o.swapaxes(0, 1).reshape(B, S, H, dv), state


def fox_attend(q, k, v, F_q, F_k, q_pos, k_pos):
    s = jnp.einsum('bqhd,bkhd->bhqk', q, k, preferred_element_type=jnp.float32) * (FOX_HEAD_DIM ** -0.5)
    bias = F_q.transpose(0, 2, 1)[:, :, :, None] - F_k.transpose(0, 2, 1)[:, :, None, :]
    s = jnp.where((q_pos[:, None] >= k_pos[None, :])[None, None], s + bias, -jnp.inf)
    p = jax.nn.softmax(s, axis=-1)
    return jnp.einsum('bhqk,bkhd->bqhd', p.astype(v.dtype), v)


def fox_prompt(q, k, v, logf):
    S = q.shape[1]
    F = jnp.cumsum(logf, axis=1)
    pos = jnp.arange(S, dtype=jnp.int32)
    outs = []
    for i in range(S // Q_BLOCK):
        a, e = i * Q_BLOCK, (i + 1) * Q_BLOCK
        outs.append(fox_attend(q[:, a:e], k[:, :e], v[:, :e], F[:, a:e], F[:, :e], pos[a:e], pos[:e]))
    return jnp.concatenate(outs, axis=1)


def fox_sample(q, k, v, logf, cache_k, cache_v, cache_logf):
    past = cache_k.shape[1]
    L = q.shape[1]
    k_all = jnp.concatenate([cache_k.astype(k.dtype), k], axis=1)
    v_all = jnp.concatenate([cache_v.astype(v.dtype), v], axis=1)
    F = jnp.cumsum(jnp.concatenate([cache_logf.astype(jnp.float32), logf], axis=1), axis=1)
    q_pos = past + jnp.arange(L, dtype=jnp.int32)
    k_pos = jnp.arange(past + L, dtype=jnp.int32)
    return fox_attend(q, k_all, v_all, F[:, past:], F, q_pos, k_pos)


def moe_ffn(h, w_rg, b_rg, w_re, b_re, w_e_in, w_e_out):
    M, D = h.shape
    hf = h.astype(jnp.float32)
    g_prob = jax.nn.softmax(hf @ w_rg.astype(jnp.float32) + b_rg.astype(jnp.float32), axis=-1)
    g_p, g_i = lax.top_k(g_prob, 1)
    e_logits = (hf @ w_re.astype(jnp.float32) + b_re.astype(jnp.float32)).reshape(M, N_GROUPS, EXPERTS_PER_GROUP)
    e_logits = jnp.take_along_axis(e_logits, g_i[:, :, None], axis=1)[:, 0]
    e_p, e_i = lax.top_k(jax.nn.softmax(e_logits, axis=-1), TOP_K)
    gate = g_p * e_p / jnp.sum(e_p, axis=-1, keepdims=True)
    expert = g_i * EXPERTS_PER_GROUP + e_i
    A = M * TOP_K
    flat_e = expert.reshape(A)
    flat_g = gate.reshape(A).astype(h.dtype)
    flat_t = jnp.repeat(jnp.arange(M, dtype=jnp.int32), TOP_K)
    order = jnp.argsort(flat_e)
    se = flat_e[order]
    counts = jnp.bincount(flat_e, length=N_EXPERTS)
    starts = jnp.cumsum(counts) - counts
    pcounts = (counts + MOE_BLOCK - 1) // MOE_BLOCK * MOE_BLOCK
    pends = jnp.cumsum(pcounts)
    pstarts = pends - pcounts
    dest = pstarts[se] + jnp.arange(A, dtype=jnp.int32) - starts[se]
    n_blocks = -(-(A + N_EXPERTS * (MOE_BLOCK - 1)) // MOE_BLOCK)
    P = n_blocks * MOE_BLOCK
    slot_t = jnp.zeros((P,), jnp.int32).at[dest].set(flat_t[order])
    slot_g = jnp.zeros((P,), h.dtype).at[dest].set(flat_g[order])
    block_e = jnp.minimum(jnp.searchsorted(pends, jnp.arange(n_blocks, dtype=jnp.int32) * MOE_BLOCK, side='right'),
                          N_EXPERTS - 1).astype(jnp.int32)

    def expert_block(args):
        e, t, g = args
        a, u = jnp.split(h[t] @ w_e_in[e], 2, axis=-1)
        return ((jax.nn.silu(a) * u) @ w_e_out[e]) * g[:, None]

    y = lax.map(expert_block, (block_e, slot_t.reshape(n_blocks, MOE_BLOCK), slot_g.reshape(n_blocks, MOE_BLOCK)))
    return jnp.zeros_like(h).at[slot_t].add(y.reshape(P, D))


def layer_forward(x, c, pos, ret_fn, fox_fn, w_ada, b_ada, w_in, b_in, ret_gn_w, ret_gn_b, w_ret_proj,
                  w_fox_proj, w_o, ln1_w, ln1_b, w_rg, b_rg, w_re, b_re, w_e_in, w_e_out, ln2_w, ln2_b):
    B, L, _ = x.shape
    mod = jax.nn.silu(c) @ w_ada + b_ada
    sh1, sc1, g1, sh2, sc2, g2 = jnp.split(mod[:, None, :], 6, axis=-1)
    h = x * (1 + sc1) + sh1
    rq, rk, rv, rg, fq, fk, fv, ff, ga, gb = _split_in(h @ w_in + b_in)
    rq = rotary(rq.reshape(B, L, RET_HEADS, RET_QK_DIM), pos).astype(jnp.float32)
    rk = (rotary(rk.reshape(B, L, RET_HEADS, RET_QK_DIM), pos) * (RET_QK_DIM ** -0.5)).astype(jnp.float32)
    rv = rv.reshape(B, L, RET_HEADS, RET_V_DIM).astype(jnp.float32)
    o_ret, ret_state = ret_fn(rq, rk, rv)
    y_ret = (jax.nn.silu(rg) * head_group_norm(o_ret, ret_gn_w, ret_gn_b).astype(x.dtype)) @ w_ret_proj
    fq = fq.reshape(B, L, FOX_HEADS, FOX_HEAD_DIM)
    fk = fk.reshape(B, L, FOX_HEADS, FOX_HEAD_DIM)
    fv = fv.reshape(B, L, FOX_HEADS, FOX_HEAD_DIM)
    logf = jax.nn.log_sigmoid(ff.astype(jnp.float32))
    y_fox = fox_fn(fq, fk, fv, logf).reshape(B, L, FOX_W) @ w_fox_proj
    mixed = (jax.nn.sigmoid(ga) * y_ret + jax.nn.sigmoid(gb) * y_fox) @ w_o
    x = layer_norm(DEEPNORM_ALPHA * x + (1 + g1) * mixed, ln1_w, ln1_b)
    h = x * (1 + sc2) + sh2
    y_moe = moe_ffn(h.reshape(B * L, D_MODEL), w_rg, b_rg, w_re, b_re, w_e_in, w_e_out).reshape(B, L, D_MODEL)
    x = layer_norm(DEEPNORM_ALPHA * x + (1 + g2) * y_moe, ln2_w, ln2_b)
    return x, ret_state, fk, fv, logf


def setup_inputs(seed: int = 0) -> dict:
    key = jax.random.key(seed)
    ks = jax.random.split(key, 32)
    f32 = jnp.float32
    D = D_MODEL
    NL = DEPTH

    def nrm(k, shape, scale):
        return scale * jax.random.normal(k, shape, f32)

    fgate_init = jnp.linspace(1.0, 5.0, FOX_HEADS, dtype=f32)
    x_prompt = nrm(ks[0], (BATCH, SEQ, D), 1.0)
    x_sample = nrm(ks[1], (DEC_BATCH, DEC_SEQ, D), 1.0)
    state_ret = nrm(ks[2], (NL, DEC_BATCH, RET_HEADS, RET_QK_DIM, RET_V_DIM), 0.5)
    cache_fox_k = nrm(ks[3], (NL, DEC_BATCH, PAST_LEN, FOX_HEADS, FOX_HEAD_DIM), 1.0)
    cache_fox_v = nrm(ks[4], (NL, DEC_BATCH, PAST_LEN, FOX_HEADS, FOX_HEAD_DIM), 1.0)
    cache_fox_logf = jax.nn.log_sigmoid(fgate_init + nrm(ks[5], (NL, DEC_BATCH, PAST_LEN, FOX_HEADS), 1.0))
    c_prompt = nrm(ks[6], (BATCH, D), 1.0)
    c_sample = nrm(ks[7], (DEC_BATCH, D), 1.0)
    w_ada = nrm(ks[8], (NL, D, 6 * D), 0.1 * D ** -0.5)
    b_ada = nrm(ks[9], (NL, 6 * D), 0.02)
    w_in = nrm(ks[10], (NL, D, N_IN), D ** -0.5)
    b_in = nrm(ks[11], (NL, N_IN), 0.02).at[:, FGATE_OFFSET:FGATE_OFFSET + FOX_HEADS].add(fgate_init)
    ret_gn_w = 1.0 + nrm(ks[12], (NL, RET_V_W), 0.02)
    ret_gn_b = nrm(ks[13], (NL, RET_V_W), 0.02)
    w_ret_proj = nrm(ks[14], (NL, RET_V_W, D), RET_V_W ** -0.5)
    w_fox_proj = nrm(ks[15], (NL, FOX_W, D), FOX_W ** -0.5)
    w_o = nrm(ks[16], (NL, D, D), DEEPNORM_BETA * D ** -0.5)
    ln1_w = 1.0 + nrm(ks[17], (NL, D), 0.02)
    ln1_b = nrm(ks[18], (NL, D), 0.02)
    w_rg = nrm(ks[19], (NL, D, N_GROUPS), D ** -0.5)
    b_rg = nrm(ks[20], (NL, N_GROUPS), 0.01)
    w_re = nrm(ks[21], (NL, D, N_EXPERTS), D ** -0.5)
    b_re = nrm(ks[22], (NL, N_EXPERTS), 0.01)
    w_e_in = nrm(ks[23], (NL, N_EXPERTS, D, 2 * EXPERT_DIM), D ** -0.5)
    w_e_out = nrm(ks[24], (NL, N_EXPERTS, EXPERT_DIM, D), DEEPNORM_BETA * EXPERT_DIM ** -0.5)
    ln2_w = 1.0 + nrm(ks[25], (NL, D), 0.02)
    ln2_b = nrm(ks[26], (NL, D), 0.02)
    return {"x_prompt": x_prompt, "x_sample": x_sample, "state_ret": state_ret,
            "cache_fox_k": cache_fox_k, "cache_fox_v": cache_fox_v, "cache_fox_logf": cache_fox_logf,
            "c_prompt": c_prompt, "c_sample": c_sample, "w_ada": w_ada, "b_ada": b_ada,
            "w_in": w_in, "b_in": b_in, "ret_gn_w": ret_gn_w, "ret_gn_b": ret_gn_b,
            "w_ret_proj": w_ret_proj, "w_fox_proj": w_fox_proj, "w_o": w_o, "ln1_w": ln1_w, "ln1_b": ln1_b,
            "w_rg": w_rg, "b_rg": b_rg, "w_re": w_re, "b_re": b_re, "w_e_in": w_e_in, "w_e_out": w_e_out,
            "ln2_w": ln2_w, "ln2_b": ln2_b}


def reference(x_prompt, x_sample, state_ret, cache_fox_k, cache_fox_v, cache_fox_logf, c_prompt, c_sample,
              w_ada, b_ada, w_in, b_in, ret_gn_w, ret_gn_b, w_ret_proj, w_fox_proj, w_o, ln1_w, ln1_b,
              w_rg, b_rg, w_re, b_re, w_e_in, w_e_out, ln2_w, ln2_b):
    log_gamma = retention_log_decay()
    past = cache_fox_k.shape[2]
    pos_p = jnp.arange(x_prompt.shape[1], dtype=jnp.int32)
    pos_s = past + jnp.arange(x_sample.shape[1], dtype=jnp.int32)
    ret_p = functools.partial(retention_prompt, log_gamma=log_gamma)
    xp, xs = x_prompt, x_sample
    p_ret, p_k, p_v, p_f, s_ret, s_k, s_v, s_f = [], [], [], [], [], [], [], []
    for l in range(DEPTH):
        lw = (w_ada[l], b_ada[l], w_in[l], b_in[l], ret_gn_w[l], ret_gn_b[l], w_ret_proj[l], w_fox_proj[l],
              w_o[l], ln1_w[l], ln1_b[l], w_rg[l], b_rg[l], w_re[l], b_re[l], w_e_in[l], w_e_out[l],
              ln2_w[l], ln2_b[l])
        xp, st, kk, vv, ff = layer_forward(xp, c_prompt, pos_p, ret_p, fox_prompt, *lw)
        p_ret.append(st); p_k.append(kk); p_v.append(vv); p_f.append(ff)
        ret_s = functools.partial(ret_chunk, state=state_ret[l].astype(jnp.float32), log_gamma=log_gamma)
        fox_s = functools.partial(fox_sample, cache_k=cache_fox_k[l], cache_v=cache_fox_v[l],
                                  cache_logf=cache_fox_logf[l])
        xs, st, kk, vv, ff = layer_forward(xs, c_sample, pos_s, ret_s, fox_s, *lw)
        s_ret.append(st); s_k.append(kk); s_v.append(vv); s_f.append(ff)
    return (xp, xs, jnp.stack(p_ret), jnp.stack(p_k), jnp.stack(p_v), jnp.stack(p_f),
            jnp.stack(s_ret), jnp.stack(s_k), jnp.stack(s_v), jnp.stack(s_f))
```

```python
import functools
import math

import jax
import jax.numpy as jnp
import numpy as np
from jax import lax
from jax.experimental import pallas as pl
from jax.experimental.pallas import tpu as pltpu

F32 = jnp.float32
BF16 = jnp.bfloat16

RET_HEADS = 4
RET_QK_DIM = 128
RET_V_DIM = 256
ROPE_BASE = 10000.0
FOX_HEADS = 16
FOX_HEAD_DIM = 64
N_GROUPS = 4
EXPERTS_PER_GROUP = 8
N_EXPERTS = N_GROUPS * EXPERTS_PER_GROUP
LN_EPS = 1e-5
GN_EPS = 1e-6
RET_QK_W = RET_HEADS * RET_QK_DIM
RET_V_W = RET_HEADS * RET_V_DIM
FOX_W = FOX_HEADS * FOX_HEAD_DIM
FOX_PAIRS = FOX_HEADS // 2

LANES = 128
VMEM_LIMIT_BYTES = 56 * 1024 * 1024
MASK_VALUE = -1e30


def _dot(a, b):
    return jnp.dot(a, b, preferred_element_type=F32)


def _dot_nt(a, b):
    return lax.dot_general(a, b, (((1,), (1,)), ((), ())), preferred_element_type=F32)


def _dot_tn(a, b):
    return lax.dot_general(a, b, (((0,), (0,)), ((), ())), preferred_element_type=F32)


def _split3(x):
    hi = x.astype(BF16)
    r1 = x - hi.astype(F32)
    mid = r1.astype(BF16)
    lo = (r1 - mid.astype(F32)).astype(BF16)
    return hi, mid, lo


def _sigmoid(x):
    return 1.0 / (1.0 + jnp.exp(-x))


def _log_sigmoid(x):
    return -(jnp.maximum(-x, 0.0) + jnp.log1p(jnp.exp(-jnp.abs(x))))


def _params(*sem):
    return pltpu.CompilerParams(dimension_semantics=sem, vmem_limit_bytes=VMEM_LIMIT_BYTES)


def _resident(shape):
    nd = len(shape)
    return pl.BlockSpec(shape, lambda *_: (0,) * nd, pipeline_mode=pl.Buffered(1))


def _ada_kernel(c_ref, w_ref, b_ref, o_ref):
    c = c_ref[...]
    s = (c * _sigmoid(c)).astype(BF16)
    o_ref[...] = _dot(s, w_ref[...].astype(BF16)) + b_ref[...]


def _ada(c, w, b):
    n, d = c.shape
    nout = w.shape[1]
    tn = d
    return pl.pallas_call(
        _ada_kernel,
        out_shape=jax.ShapeDtypeStruct((n, nout), F32),
        grid=(nout // tn,),
        in_specs=[pl.BlockSpec((n, d), lambda j: (0, 0)),
                  pl.BlockSpec((d, tn), lambda j: (0, j)),
                  pl.BlockSpec((1, tn), lambda j: (0, j))],
        out_specs=pl.BlockSpec((n, tn), lambda j: (0, j)),
        compiler_params=_params("arbitrary"),
        name="ada",
    )(c, w, b.reshape(1, nout))


def _inproj_kernel(x_ref, mod_ref, w_ref, b_ref, cq_ref, sq_ref, ck_ref, sk_ref,
                   rq_ref, rk_ref, rv_ref, rg_ref, fq_ref, fk_ref, fv_ref, fkb_ref, fvb_ref,
                   lf_ref, ga_ref, gb_ref):
    gb_, tl, d = x_ref.shape
    tm = gb_ * tl
    sh = mod_ref[:, :, 0:d]
    sc = mod_ref[:, :, d:2 * d]
    h = (x_ref[...] * (1.0 + sc) + sh).reshape(tm, d).astype(BF16)

    def proj(lo, width):
        return _dot(h, w_ref[:, lo:lo + width]) + b_ref[:, lo:lo + width]

    def put(ref, val):
        ref[...] = val.reshape(ref.shape).astype(ref.dtype)

    def rot(z, c_ref, s_ref):
        c = c_ref[...]
        s = s_ref[...]
        parts = []
        for hh in range(RET_HEADS):
            zh = z[:, hh * RET_QK_DIM:(hh + 1) * RET_QK_DIM]
            parts.append(zh * c + pltpu.roll(zh, RET_QK_DIM // 2, axis=1) * s)
        return jnp.concatenate(parts, axis=1)

    off = 0
    put(rq_ref, rot(proj(off, RET_QK_W), cq_ref, sq_ref)); off += RET_QK_W
    put(rk_ref, rot(proj(off, RET_QK_W), ck_ref, sk_ref)); off += RET_QK_W
    put(rv_ref, proj(off, RET_V_W)); off += RET_V_W
    z = proj(off, RET_V_W); off += RET_V_W
    put(rg_ref, z * _sigmoid(z))
    put(fq_ref, proj(off, FOX_W) * (FOX_HEAD_DIM ** -0.5)); off += FOX_W
    z = proj(off, FOX_W); off += FOX_W
    put(fk_ref, z); put(fkb_ref, z)
    z = proj(off, FOX_W); off += FOX_W
    put(fv_ref, z); put(fvb_ref, z)
    put(ga_ref, _sigmoid(proj(off, d))); off += d
    put(gb_ref, _sigmoid(proj(off, d))); off += d
    put(lf_ref, _log_sigmoid(proj(off, LANES)))


def _inproj(x, mod, w2, b2, tabs, gb_, tl):
    bx, l, d = x.shape
    tm = gb_ * tl
    nw = w2.shape[1]
    grid = (bx // gb_, l // tl)
    if gb_ == 1:
        tab_spec = pl.BlockSpec((tl, LANES), lambda b, j: (j, 0))
    else:
        tab_spec = pl.BlockSpec((tm, LANES), lambda b, j: (0, 0))

    def out(width, dtype):
        return (jax.ShapeDtypeStruct((bx, l, width), dtype),
                pl.BlockSpec((gb_, tl, width), lambda b, j: (b, j, 0)))

    outs = [out(RET_QK_W, BF16), out(RET_QK_W, BF16), out(RET_V_W, BF16), out(RET_V_W, BF16),
            out(FOX_W, BF16), out(FOX_W, F32), out(FOX_W, F32), out(FOX_W, BF16), out(FOX_W, BF16),
            out(LANES, F32), out(d, BF16), out(d, BF16)]
    return pl.pallas_call(
        _inproj_kernel,
        out_shape=[o[0] for o in outs],
        grid=grid,
        in_specs=[pl.BlockSpec((gb_, tl, d), lambda b, j: (b, j, 0)),
                  pl.BlockSpec((gb_, 1, mod.shape[-1]), lambda b, j: (b, 0, 0)),
                  _resident((d, nw)), _resident((1, nw)),
                  tab_spec, tab_spec, tab_spec, tab_spec],
        out_specs=[o[1] for o in outs],
        compiler_params=_params("arbitrary", "arbitrary"),
        name="inproj",
    )(x, mod, w2, b2, *tabs)


def _fcum_kernel(lf_ref, tril_ref, ft_ref, carry_ref):
    @pl.when(pl.program_id(1) == 0)
    def _():
        carry_ref[...] = jnp.zeros_like(carry_ref)

    tl = lf_ref.shape[1]
    hi, mid, lo = _split3(lf_ref[0])
    tril = tril_ref[...]
    cum = _dot(tril, hi) + _dot(tril, mid) + _dot(tril, lo) + carry_ref[...]
    carry_ref[...] = cum[tl - 1:tl, :]
    ft_ref[0] = cum.T[:FOX_HEADS, :]


def _fcum(lf, tl):
    bx, l, _ = lf.shape
    tril = jnp.asarray(np.tril(np.ones((tl, tl), np.float32)), BF16)
    return pl.pallas_call(
        _fcum_kernel,
        out_shape=jax.ShapeDtypeStruct((bx, FOX_HEADS, l), F32),
        grid=(bx, l // tl),
        in_specs=[pl.BlockSpec((1, tl, LANES), lambda b, j: (b, j, 0)),
                  pl.BlockSpec((tl, tl), lambda b, j: (0, 0))],
        out_specs=pl.BlockSpec((1, FOX_HEADS, tl), lambda b, j: (b, 0, j)),
        scratch_shapes=[pltpu.VMEM((1, LANES), F32)],
        compiler_params=_params("arbitrary", "arbitrary"),
        name="fcum",
    )(lf, tril)


def _retention_kernel(*refs, has_state):
    if has_state:
        (q_ref, k_ref, v_ref, g_ref, dec_ref, qd_ref, kd_ref, gw_ref, gb_ref, s0_ref,
         y_ref, st_ref) = refs
    else:
        (q_ref, k_ref, v_ref, g_ref, dec_ref, qd_ref, kd_ref, gw_ref, gb_ref,
         y_ref, st_ref) = refs
        s0_ref = None
    lc = q_ref.shape[1]

    @pl.when(pl.program_id(1) == 0)
    def _():
        if has_state:
            st_ref[...] = s0_ref[...]
        else:
            st_ref[...] = jnp.zeros_like(st_ref)

    for hh in range(RET_HEADS):
        qs = slice(hh * RET_QK_DIM, (hh + 1) * RET_QK_DIM)
        vs = slice(hh * RET_V_DIM, (hh + 1) * RET_V_DIM)
        q = q_ref[0, :, qs]
        k = k_ref[0, :, qs]
        v = v_ref[0, :, vs]
        state = st_ref[0, hh]
        qd = qd_ref[hh]
        scores = _dot_nt(q, k) * dec_ref[hh]
        inner = _dot(scores.astype(BF16), v)
        cross = _dot(q, state.astype(BF16)) * qd
        o = inner + cross
        kdec = (k.astype(F32) * kd_ref[hh]).astype(BF16)
        st_ref[0, hh] = qd[lc - 1:lc, :] * state + _dot_tn(kdec, v)
        mu = jnp.mean(o, axis=-1, keepdims=True)
        oc = o - mu
        var = jnp.mean(oc * oc, axis=-1, keepdims=True)
        on = oc * lax.rsqrt(var + GN_EPS) * gw_ref[:, vs] + gb_ref[:, vs]
        y_ref[0, :, vs] = (g_ref[0, :, vs].astype(F32) * on).astype(y_ref.dtype)


def _retention_tables(log_gamma, lc):
    n = jnp.arange(lc, dtype=F32)
    diff = n[:, None] - n[None, :]
    decay = jnp.where(diff[None] >= 0, jnp.exp(jnp.maximum(diff, 0.0)[None] * log_gamma[:, None, None]), 0.0)
    qdec = jnp.exp((n + 1.0)[None, :, None] * log_gamma[:, None, None])
    kdec = jnp.exp((lc - 1.0 - n)[None, :, None] * log_gamma[:, None, None])
    return decay, qdec, kdec


def _retention(q, k, v, g, gn_w, gn_b, log_gamma, lc, state0=None):
    bx, l, _ = q.shape
    decay, qdec, kdec = _retention_tables(log_gamma, lc)
    has_state = state0 is not None
    seq = lambda w: pl.BlockSpec((1, lc, w), lambda b, c: (b, c, 0))
    whole = lambda a: pl.BlockSpec(a.shape, lambda b, c: (0,) * a.ndim)
    st_spec = pl.BlockSpec((1, RET_HEADS, RET_QK_DIM, RET_V_DIM), lambda b, c: (b, 0, 0, 0))
    gw = gn_w.reshape(1, RET_V_W)
    gb = gn_b.reshape(1, RET_V_W)
    args = [q, k, v, g, decay, qdec, kdec, gw, gb]
    in_specs = [seq(RET_QK_W), seq(RET_QK_W), seq(RET_V_W), seq(RET_V_W),
                whole(decay), whole(qdec), whole(kdec), whole(gw), whole(gb)]
    if has_state:
        args.append(state0)
        in_specs.append(st_spec)
    return pl.pallas_call(
        functools.partial(_retention_kernel, has_state=has_state),
        out_shape=[jax.ShapeDtypeStruct((bx, l, RET_V_W), BF16),
                   jax.ShapeDtypeStruct((bx, RET_HEADS, RET_QK_DIM, RET_V_DIM), F32)],
        grid=(bx, l // lc),
        in_specs=in_specs,
        out_specs=[seq(RET_V_W), st_spec],
        compiler_params=_params("arbitrary", "arbitrary"),
        name="retention",
    )(*args)


def _fox_prompt_kernel(q_ref, k_ref, v_ref, f_ref, o_ref, m_ref, l_ref, acc_ref):
    t = q_ref.shape[1]
    qi = pl.program_id(2)
    lane = lax.broadcasted_iota(jnp.int32, (1, LANES), 1)
    head_mask = (lane < FOX_HEAD_DIM, lane >= FOX_HEAD_DIM)
    q2 = q_ref[0]
    zero = jnp.zeros_like(q2)
    qh = [jnp.where(head_mask[i], q2, zero) for i in range(2)]
    m_ref[...] = jnp.full_like(m_ref, MASK_VALUE)
    l_ref[...] = jnp.zeros_like(l_ref)
    acc_ref[...] = jnp.zeros_like(acc_ref)

    def step(ki, masked):
        start = pl.multiple_of(ki * t, t)
        k2 = k_ref[0, pl.ds(start, t), :]
        v2 = v_ref[0, pl.ds(start, t), :]
        fk = f_ref[0, 0, :, pl.ds(start, t)]
        for i in range(2):
            s = _dot_nt(qh[i], k2) - fk[i:i + 1, :]
            if masked:
                row = lax.broadcasted_iota(jnp.int32, (t, t), 0)
                col = lax.broadcasted_iota(jnp.int32, (t, t), 1)
                s = jnp.where(row >= col, s, MASK_VALUE)
            m_prev = m_ref[i]
            m_new = jnp.maximum(m_prev, jnp.max(s, axis=-1, keepdims=True))
            alpha = jnp.exp(m_prev - m_new)
            p = jnp.exp(s - m_new)
            l_ref[i] = alpha * l_ref[i] + jnp.sum(p, axis=-1, keepdims=True)
            m_ref[i] = m_new
            vh = jnp.where(head_mask[i], v2, jnp.zeros_like(v2))
            scale = jnp.where(head_mask[i], alpha, 1.0)
            acc_ref[...] = acc_ref[...] * scale + _dot(p.astype(BF16), vh)

    def body(ki, carry):
        step(ki, False)
        return carry

    lax.fori_loop(0, qi, body, 0)
    step(qi, True)
    inv = jnp.where(head_mask[0], 1.0 / l_ref[0], 1.0 / l_ref[1])
    o_ref[0] = (acc_ref[...] * inv).astype(o_ref.dtype)


def _fox_prompt(q, k, v, ft, t):
    b, s, _ = q.shape
    f4 = ft.reshape(b, FOX_PAIRS, 2, s)
    return pl.pallas_call(
        _fox_prompt_kernel,
        out_shape=jax.ShapeDtypeStruct((b, s, FOX_W), BF16),
        grid=(b, FOX_PAIRS, s // t),
        in_specs=[pl.BlockSpec((1, t, LANES), lambda bi, j, qi: (bi, qi, j)),
                  pl.BlockSpec((1, s, LANES), lambda bi, j, qi: (bi, 0, j)),
                  pl.BlockSpec((1, s, LANES), lambda bi, j, qi: (bi, 0, j)),
                  pl.BlockSpec((1, 1, 2, s), lambda bi, j, qi: (bi, j, 0, 0))],
        out_specs=pl.BlockSpec((1, t, LANES), lambda bi, j, qi: (bi, qi, j)),
        scratch_shapes=[pltpu.VMEM((2, t, 1), F32), pltpu.VMEM((2, t, 1), F32),
                        pltpu.VMEM((t, LANES), F32)],
        compiler_params=_params("arbitrary", "arbitrary", "arbitrary"),
        name="fox_prompt",
    )(q, k, v, f4)


def _fox_sample_kernel(q_ref, kc_ref, vc_ref, kn_ref, vn_ref, f_ref, o_ref):
    l = q_ref.shape[1]
    past = kc_ref.shape[1]
    lane = lax.broadcasted_iota(jnp.int32, (1, LANES), 1)
    head_mask = (lane < FOX_HEAD_DIM, lane >= FOX_HEAD_DIM)
    q2 = q_ref[0]
    kc = kc_ref[0].astype(BF16)
    vc = vc_ref[0].astype(BF16)
    kn = kn_ref[0]
    vn = vn_ref[0]
    row = lax.broadcasted_iota(jnp.int32, (l, l), 0)
    col = lax.broadcasted_iota(jnp.int32, (l, l), 1)
    out = jnp.zeros((l, LANES), F32)
    for i in range(2):
        qh = jnp.where(head_mask[i], q2, jnp.zeros_like(q2))
        s_c = _dot_nt(qh, kc) - f_ref[0, 0, i:i + 1, 0:past]
        s_n = _dot_nt(qh, kn) - f_ref[0, 0, i:i + 1, past:past + l]
        s_n = jnp.where(row >= col, s_n, MASK_VALUE)
        m = jnp.maximum(jnp.max(s_c, axis=-1, keepdims=True), jnp.max(s_n, axis=-1, keepdims=True))
        p_c = jnp.exp(s_c - m)
        p_n = jnp.exp(s_n - m)
        denom = jnp.sum(p_c, axis=-1, keepdims=True) + jnp.sum(p_n, axis=-1, keepdims=True)
        vch = jnp.where(head_mask[i], vc, jnp.zeros_like(vc))
        vnh = jnp.where(head_mask[i], vn, jnp.zeros_like(vn))
        out = out + (_dot(p_c.astype(BF16), vch) + _dot(p_n.astype(BF16), vnh)) * (1.0 / denom)
    o_ref[0] = out.astype(o_ref.dtype)


def _fox_sample(q, kn, vn, cache_k, cache_v, ft):
    b, l, _ = q.shape
    past = cache_k.shape[1]
    lf = ft.shape[-1]
    f4 = ft.reshape(b, FOX_PAIRS, 2, lf)
    new = pl.BlockSpec((1, l, LANES), lambda bi, j: (bi, 0, j))
    old = pl.BlockSpec((1, past, LANES), lambda bi, j: (bi, 0, j))
    return pl.pallas_call(
        _fox_sample_kernel,
        out_shape=jax.ShapeDtypeStruct((b, l, FOX_W), BF16),
        grid=(b, FOX_PAIRS),
        in_specs=[new, old, old, new, new,
                  pl.BlockSpec((1, 1, 2, lf), lambda bi, j: (bi, j, 0, 0))],
        out_specs=new,
        compiler_params=_params("arbitrary", "arbitrary"),
        name="fox_sample",
    )(q, cache_k, cache_v, kn, vn, f4)


def _layer_norm(x, w, b):
    mu = jnp.mean(x, axis=-1, keepdims=True)
    xc = x - mu
    var = jnp.mean(xc * xc, axis=-1, keepdims=True)
    return xc * lax.rsqrt(var + LN_EPS) * w + b


def _first_argmax_rows(x, n):
    rows = lax.broadcasted_iota(jnp.int32, x.shape, 0).astype(F32)
    mx = jnp.max(x, axis=0, keepdims=True)
    idx = jnp.min(jnp.where(x == mx, rows, float(n)), axis=0, keepdims=True)
    return mx, idx.astype(jnp.int32)


def _outproj_kernel(yr_ref, yf_ref, ga_ref, gb_ref, x_ref, mod_ref, wr_ref, wf_ref, wo_ref,
                    lw_ref, lb_ref, rw_ref, rb_ref,
                    x1_ref, h2_ref, eid_ref, gate_ref, gt_ref, *, alpha):
    gb_, tl, d = x_ref.shape
    tm = gb_ * tl
    flat = lambda ref: ref[...].reshape(tm, ref.shape[-1])
    y_ret = _dot(flat(yr_ref), wr_ref[...])
    y_fox = _dot(flat(yf_ref), wf_ref[...])
    mix = flat(ga_ref).astype(F32) * y_ret + flat(gb_ref).astype(F32) * y_fox
    mixed = _dot(mix.astype(BF16), wo_ref[...]).reshape(gb_, tl, d)
    g1 = mod_ref[:, :, 2 * d:3 * d]
    sh2 = mod_ref[:, :, 3 * d:4 * d]
    sc2 = mod_ref[:, :, 4 * d:5 * d]
    x1 = _layer_norm(alpha * x_ref[...] + (1.0 + g1) * mixed, lw_ref[...], lb_ref[...])
    x1_ref[...] = x1
    h2 = (x1 * (1.0 + sc2) + sh2)
    h2_ref[...] = h2
    h_hi, h_mid, h_lo = _split3(h2.reshape(tm, d))
    w_hi, w_mid, w_lo = rw_ref[0], rw_ref[1], rw_ref[2]
    lt = (_dot_nt(w_hi, h_hi) + _dot_nt(w_hi, h_mid) + _dot_nt(w_mid, h_hi)
          + _dot_nt(w_hi, h_lo) + _dot_nt(w_lo, h_hi) + _dot_nt(w_mid, h_mid)) + rb_ref[...]
    gl = lt[N_EXPERTS:N_EXPERTS + N_GROUPS, :]
    gmax, gi = _first_argmax_rows(gl, N_GROUPS)
    g_p = 1.0 / jnp.sum(jnp.exp(gl - gmax), axis=0, keepdims=True)
    e_sel = lt[0:EXPERTS_PER_GROUP, :]
    for g in range(1, N_GROUPS):
        e_sel = jnp.where(gi == g, lt[g * EXPERTS_PER_GROUP:(g + 1) * EXPERTS_PER_GROUP, :], e_sel)
    rows = lax.broadcasted_iota(jnp.int32, e_sel.shape, 0)
    m1, i1 = _first_argmax_rows(e_sel, EXPERTS_PER_GROUP)
    m2, i2 = _first_argmax_rows(jnp.where(rows == i1, -jnp.inf, e_sel), EXPERTS_PER_GROUP)
    r = jnp.exp(m2 - m1)
    gate0 = g_p / (1.0 + r)
    gate1 = g_p * r / (1.0 + r)
    eid_ref[0:1, :] = gi * EXPERTS_PER_GROUP + i1
    eid_ref[1:2, :] = gi * EXPERTS_PER_GROUP + i2
    gate_ref[0:1, :] = gate0
    gate_ref[1:2, :] = gate1
    rr = lax.broadcasted_iota(jnp.int32, (LANES, tm), 0)
    gsq = jnp.where(rr == 0, gate0, jnp.where(rr == 1, gate1, 0.0))
    gt_ref[...] = gsq.T


def _outproj(yr, yf, ga, gb, x, mod, wr, wf, wo, lw, lb, rw3, rb, gb_, tl, alpha):
    bx, l, d = x.shape
    m = bx * l
    tm = gb_ * tl
    nl = l // tl
    seq = lambda w: pl.BlockSpec((gb_, tl, w), lambda b, j: (b, j, 0))
    tok = lambda r: pl.BlockSpec((r, tm), lambda b, j: (0, b * nl + j))
    return pl.pallas_call(
        functools.partial(_outproj_kernel, alpha=alpha),
        out_shape=[jax.ShapeDtypeStruct((bx, l, d), F32), jax.ShapeDtypeStruct((bx, l, d), F32),
                   jax.ShapeDtypeStruct((2, m), jnp.int32), jax.ShapeDtypeStruct((2, m), F32),
                   jax.ShapeDtypeStruct((m, LANES), F32)],
        grid=(bx // gb_, nl),
        in_specs=[seq(RET_V_W), seq(FOX_W), seq(d), seq(d), seq(d),
                  pl.BlockSpec((gb_, 1, mod.shape[-1]), lambda b, j: (b, 0, 0)),
                  _resident(wr.shape), _resident(wf.shape), _resident(wo.shape),
                  _resident(lw.shape), _resident(lb.shape), _resident(rw3.shape), _resident(rb.shape)],
        out_specs=[seq(d), seq(d), tok(2), tok(2),
                   pl.BlockSpec((tm, LANES), lambda b, j: (b * nl + j, 0))],
        compiler_params=_params("arbitrary", "arbitrary"),
        name="outproj",
    )(yr, yf, ga, gb, x, mod, wr, wf, wo, lw, lb, rw3, rb)


def _rank_kernel(eid_ref, triu_ref, rank_ref, cnt_ref):
    @pl.when(pl.program_id(0) == 0)
    def _():
        cnt_ref[...] = jnp.zeros_like(cnt_ref)

    ta = eid_ref.shape[1]
    experts = lax.broadcasted_iota(jnp.int32, (N_EXPERTS, ta), 0)
    carry = cnt_ref[...]
    ranks = []
    for kk in range(2):
        hit = eid_ref[kk:kk + 1, :] == experts
        onehot = jnp.where(hit, 1.0, 0.0)
        before = _dot(onehot.astype(BF16), triu_ref[...]) + carry
        rank = jnp.sum(jnp.where(hit, before, 0.0), axis=0, keepdims=True)
        rank_ref[kk:kk + 1, :] = rank.astype(jnp.int32)
        carry = carry + jnp.sum(onehot, axis=1, keepdims=True)
    cnt_ref[...] = carry


def _rank(eid, ta):
    m = eid.shape[1]
    triu = jnp.asarray(np.triu(np.ones((ta, ta), np.float32), 1), BF16)
    return pl.pallas_call(
        _rank_kernel,
        out_shape=[jax.ShapeDtypeStruct((2, m), jnp.int32), jax.ShapeDtypeStruct((N_EXPERTS, 1), F32)],
        grid=(m // ta,),
        in_specs=[pl.BlockSpec((2, ta), lambda i: (0, i)), pl.BlockSpec((ta, ta), lambda i: (0, 0))],
        out_specs=[pl.BlockSpec((2, ta), lambda i: (0, i)), pl.BlockSpec((N_EXPERTS, 1), lambda i: (0, 0))],
        compiler_params=_params("arbitrary"),
        name="moe_rank",
    )(eid, triu)


def _dest_kernel(eid_ref, rank_ref, pstart_ref, dest_ref):
    ta = eid_ref.shape[1]
    experts = lax.broadcasted_iota(jnp.int32, (N_EXPERTS, ta), 0)
    for kk in range(2):
        hit = eid_ref[kk:kk + 1, :] == experts
        start = jnp.sum(jnp.where(hit, pstart_ref[...], 0.0), axis=0, keepdims=True)
        dest_ref[kk:kk + 1, :] = start.astype(jnp.int32) + rank_ref[kk:kk + 1, :]


def _dest(eid, rank, pstart, ta):
    m = eid.shape[1]
    tok = pl.BlockSpec((2, ta), lambda i: (0, i))
    return pl.pallas_call(
        _dest_kernel,
        out_shape=jax.ShapeDtypeStruct((2, m), jnp.int32),
        grid=(m // ta,),
        in_specs=[tok, tok, pl.BlockSpec((N_EXPERTS, 1), lambda i: (0, 0))],
        out_specs=tok,
        compiler_params=_params("arbitrary"),
        name="moe_dest",
    )(eid, rank, pstart)


def _dispatch_kernel(dest_ref, h_ref, xs_in_ref, xs_ref, sem):
    del xs_in_ref
    tm = h_ref.shape[0]

    def copy(t, kk):
        return pltpu.make_async_copy(h_ref.at[pl.ds(t, 1)], xs_ref.at[pl.ds(dest_ref[kk, t], 1)], sem)

    def start(t, c):
        copy(t, 0).start()
        copy(t, 1).start()
        return c

    def wait(t, c):
        copy(t, 0).wait()
        copy(t, 1).wait()
        return c

    lax.fori_loop(0, tm, start, 0)
    lax.fori_loop(0, tm, wait, 0)


def _dispatch(dest, h2, n_slots, tm):
    m, d = h2.shape
    xs0 = jnp.zeros((n_slots, d), h2.dtype)
    return pl.pallas_call(
        _dispatch_kernel,
        out_shape=jax.ShapeDtypeStruct((n_slots, d), h2.dtype),
        grid=(m // tm,),
        in_specs=[pl.BlockSpec((2, tm), lambda i: (0, i), memory_space=pltpu.SMEM),
                  pl.BlockSpec((tm, d), lambda i: (i, 0)),
                  pl.BlockSpec(memory_space=pl.ANY)],
        out_specs=pl.BlockSpec(memory_space=pl.ANY),
        scratch_shapes=[pltpu.SemaphoreType.DMA],
        input_output_aliases={2: 0},
        compiler_params=_params("arbitrary"),
        name="moe_dispatch",
    )(dest, h2, xs0)


def _expert_kernel(be_ref, x_ref, w1_ref, w2_ref, y_ref):
    del be_ref
    e = w2_ref.shape[1]
    au = _dot(x_ref[...].astype(BF16), w1_ref[0])
    a = au[:, :e]
    u = au[:, e:]
    y_ref[...] = _dot((a * _sigmoid(a) * u).astype(BF16), w2_ref[0])


def _experts(block_e, xs, w1, w2, tb):
    p, d = xs.shape
    e = w2.shape[1]
    return pl.pallas_call(
        _expert_kernel,
        out_shape=jax.ShapeDtypeStruct((p, d), F32),
        grid_spec=pltpu.PrefetchScalarGridSpec(
            num_scalar_prefetch=1,
            grid=(p // tb,),
            in_specs=[pl.BlockSpec((tb, d), lambda i, be: (i, 0)),
                      pl.BlockSpec((1, d, 2 * e), lambda i, be: (be[i], 0, 0)),
                      pl.BlockSpec((1, e, d), lambda i, be: (be[i], 0, 0))],
            out_specs=pl.BlockSpec((tb, d), lambda i, be: (i, 0))),
        compiler_params=_params("arbitrary"),
        name="moe_experts",
    )(block_e, xs, w1, w2)


def _combine_kernel(dest_ref, y_hbm, gt_ref, x1_ref, mod_ref, lw_ref, lb_ref, o_ref, buf0, buf1, sem, *, alpha):
    gb_, tl, d = x1_ref.shape
    tm = gb_ * tl
    bufs = (buf0, buf1)

    def copy(t, kk):
        return pltpu.make_async_copy(y_hbm.at[pl.ds(dest_ref[kk, t], 1)], bufs[kk].at[pl.ds(t, 1)], sem)

    def start(t, c):
        copy(t, 0).start()
        copy(t, 1).start()
        return c

    def wait(t, c):
        copy(t, 0).wait()
        copy(t, 1).wait()
        return c

    lax.fori_loop(0, tm, start, 0)
    lax.fori_loop(0, tm, wait, 0)
    y = gt_ref[:, 0:1] * buf0[...] + gt_ref[:, 1:2] * buf1[...]
    g2 = mod_ref[:, :, 5 * d:6 * d]
    o_ref[...] = _layer_norm(alpha * x1_ref[...] + (1.0 + g2) * y.reshape(gb_, tl, d), lw_ref[...], lb_ref[...])


def _combine(dest, y, gt, x1, mod, lw, lb, gb_, tl, alpha):
    bx, l, d = x1.shape
    tm = gb_ * tl
    nl = l // tl
    seq = pl.BlockSpec((gb_, tl, d), lambda b, j: (b, j, 0))
    return pl.pallas_call(
        functools.partial(_combine_kernel, alpha=alpha),
        out_shape=jax.ShapeDtypeStruct((bx, l, d), F32),
        grid=(bx // gb_, nl),
        in_specs=[pl.BlockSpec((2, tm), lambda b, j: (0, b * nl + j), memory_space=pltpu.SMEM),
                  pl.BlockSpec(memory_space=pl.ANY),
                  pl.BlockSpec((tm, LANES), lambda b, j: (b * nl + j, 0)),
                  seq,
                  pl.BlockSpec((gb_, 1, mod.shape[-1]), lambda b, j: (b, 0, 0)),
                  _resident(lw.shape), _resident(lb.shape)],
        out_specs=seq,
        scratch_shapes=[pltpu.VMEM((tm, d), F32), pltpu.VMEM((tm, d), F32), pltpu.SemaphoreType.DMA],
        compiler_params=_params("arbitrary", "arbitrary"),
        name="moe_combine",
    )(dest, y, gt, x1, mod, lw, lb)


def _moe(h2, eid, gt, x1, mod, w1, w2, lw, lb, gb_, tl, alpha, tb, ta):
    bx, l, d = x1.shape
    m = bx * l
    rank, counts = _rank(eid, ta)
    cnt = counts[:, 0].astype(jnp.int32)
    pcnt = (cnt + tb - 1) // tb * tb
    pend = jnp.cumsum(pcnt)
    pstart = (pend - pcnt).astype(jnp.int32)
    n_blocks = -(-(2 * m + N_EXPERTS * (tb - 1)) // tb)
    block_e = jnp.minimum(jnp.searchsorted(pend, jnp.arange(n_blocks, dtype=jnp.int32) * tb, side='right'),
                          N_EXPERTS - 1).astype(jnp.int32)
    dest = _dest(eid, rank, pstart.astype(F32).reshape(N_EXPERTS, 1), ta)
    xs = _dispatch(dest, h2.reshape(m, d), n_blocks * tb, gb_ * tl)
    y = _experts(block_e, xs, w1, w2, tb)
    return _combine(dest, y, gt, x1, mod, lw, lb, gb_, tl, alpha)


def _rotary_tables(pos, reps):
    half = RET_QK_DIM // 2
    inv = ROPE_BASE ** (-jnp.linspace(0.0, 1.0, half, dtype=F32))
    ang = pos.astype(F32)[:, None] * inv[None, :]
    cos = jnp.cos(ang)
    sin = jnp.sin(ang)
    c2 = jnp.concatenate([cos, cos], axis=1)
    s2 = jnp.concatenate([-sin, sin], axis=1)
    kscale = RET_QK_DIM ** -0.5
    tabs = (c2, s2, c2 * kscale, s2 * kscale)
    return tuple(jnp.tile(t, (reps, 1)) for t in tabs)


def _pick(n, pref):
    t = min(n, pref)
    while n % t:
        t //= 2
    return t


def _layer(x, mod, pos, weights, log_gamma, alpha, cache=None):
    (w2, b2, gn_w, gn_b, wr, wf, wo, ln1w, ln1b, rw3, rb, we1, we2, ln2w, ln2b) = weights
    bx, l, d = x.shape
    if cache is None:
        gb_, tl = 1, _pick(l, 512)
    else:
        gb_, tl = _pick(bx, max(1, 512 // l)), l
    tabs = _rotary_tables(pos, gb_ if cache is not None else 1)
    rq, rk, rv, rg, fq, fk, fv, fkb, fvb, lf, ga, gb = _inproj(x, mod, w2, b2, tabs, gb_, tl)

    if cache is None:
        lc = _pick(l, 256)
        y_ret, state = _retention(rq, rk, rv, rg, gn_w, gn_b, log_gamma, lc)
        ft = _fcum(lf, _pick(l, 256))
        y_fox = _fox_prompt(fq, fkb, fvb, ft, _pick(l, 512))
    else:
        state0, cache_k, cache_v, cache_logf = cache
        y_ret, state = _retention(rq, rk, rv, rg, gn_w, gn_b, log_gamma, l, state0)
        past = cache_k.shape[1]
        tf = LANES
        total = -(-(past + l) // tf) * tf
        lf_all = jnp.concatenate(
            [jnp.pad(cache_logf.astype(F32), ((0, 0), (0, 0), (0, LANES - FOX_HEADS))), lf,
             jnp.zeros((bx, total - past - l, LANES), F32)], axis=1)
        ft = _fcum(lf_all, tf)
        y_fox = _fox_sample(fq, fkb, fvb, cache_k.reshape(bx, past, FOX_W), cache_v.reshape(bx, past, FOX_W), ft)

    x1, h2, eid, gate, gt = _outproj(y_ret, y_fox, ga, gb, x, mod, wr, wf, wo, ln1w, ln1b, rw3, rb,
                                     gb_, tl, alpha)
    del gate
    m = bx * l
    out = _moe(h2, eid, gt, x1, mod, we1, we2, ln2w, ln2b, gb_, tl, alpha, tb=256, ta=_pick(m, 1024))
    return out, state, fk, fv, lf[:, :, :FOX_HEADS]


def kernel(x_prompt, x_sample, state_ret, cache_fox_k, cache_fox_v, cache_fox_logf, c_prompt, c_sample,
           w_ada, b_ada, w_in, b_in, ret_gn_w, ret_gn_b, w_ret_proj, w_fox_proj, w_o, ln1_w, ln1_b,
           w_rg, b_rg, w_re, b_re, w_e_in, w_e_out, ln2_w, ln2_b):
    depth = w_ada.shape[0]
    d = x_prompt.shape[-1]
    bp, s, _ = x_prompt.shape
    bs, ls, _ = x_sample.shape
    past = cache_fox_k.shape[2]
    alpha = (2 * depth) ** 0.25
    log_gamma = jnp.log1p(-jnp.exp(jnp.linspace(math.log(1.0 / 32), math.log(1.0 / 512), RET_HEADS, dtype=F32)))
    pos_p = jnp.arange(s, dtype=jnp.int32)
    pos_s = past + jnp.arange(ls, dtype=jnp.int32)
    fg = 2 * RET_QK_W + 2 * RET_V_W + 3 * FOX_W

    xp, xs = x_prompt, x_sample
    p_ret, p_k, p_v, p_f, s_ret, s_k, s_v, s_f = [], [], [], [], [], [], [], []
    for li in range(depth):
        w2 = jnp.concatenate([w_in[li][:, :fg], w_in[li][:, fg + FOX_HEADS:],
                              jnp.pad(w_in[li][:, fg:fg + FOX_HEADS], ((0, 0), (0, LANES - FOX_HEADS)))],
                             axis=1).astype(BF16)
        b2 = jnp.concatenate([b_in[li][:fg], b_in[li][fg + FOX_HEADS:],
                              jnp.pad(b_in[li][fg:fg + FOX_HEADS], (0, LANES - FOX_HEADS))]).reshape(1, -1)
        n_rt = N_EXPERTS + N_GROUPS
        rt_rows = -(-n_rt // 8) * 8
        rwt = jnp.pad(jnp.concatenate([w_re[li], w_rg[li]], axis=1).T.astype(F32), ((0, rt_rows - n_rt), (0, 0)))
        r_hi = rwt.astype(BF16)
        r_mid = (rwt - r_hi.astype(F32)).astype(BF16)
        r_lo = (rwt - r_hi.astype(F32) - r_mid.astype(F32)).astype(BF16)
        rw3 = jnp.stack([r_hi, r_mid, r_lo])
        rb = jnp.pad(jnp.concatenate([b_re[li], b_rg[li]]).astype(F32), (0, rt_rows - n_rt)).reshape(rt_rows, 1)
        weights = (w2, b2, ret_gn_w[li], ret_gn_b[li],
                   w_ret_proj[li].astype(BF16), w_fox_proj[li].astype(BF16), w_o[li].astype(BF16),
                   ln1_w[li].reshape(1, d), ln1_b[li].reshape(1, d), rw3, rb,
                   w_e_in[li].astype(BF16), w_e_out[li].astype(BF16),
                   ln2_w[li].reshape(1, d), ln2_b[li].reshape(1, d))
        mod = _ada(jnp.concatenate([c_prompt, c_sample], axis=0), w_ada[li], b_ada[li])
        mod_p = mod[:bp].reshape(bp, 1, 6 * d)
        mod_s = mod[bp:].reshape(bs, 1, 6 * d)
        xp, st, kk, vv, ff = _layer(xp, mod_p, pos_p, weights, log_gamma, alpha)
        p_ret.append(st); p_k.append(kk.reshape(bp, s, FOX_HEADS, FOX_HEAD_DIM))
        p_v.append(vv.reshape(bp, s, FOX_HEADS, FOX_HEAD_DIM)); p_f.append(ff)
        cache = (state_ret[li].astype(F32), cache_fox_k[li], cache_fox_v[li], cache_fox_logf[li])
        xs, st, kk, vv, ff = _layer(xs, mod_s, pos_s, weights, log_gamma, alpha, cache)
        s_ret.append(st); s_k.append(kk.reshape(bs, ls, FOX_HEADS, FOX_HEAD_DIM))
        s_v.append(vv.reshape(bs, ls, FOX_HEADS, FOX_HEAD_DIM)); s_f.append(ff)
    return (xp, xs, jnp.stack(p_ret), jnp.stack(p_k), jnp.stack(p_v), jnp.stack(p_f),
            jnp.stack(s_ret), jnp.stack(s_k), jnp.stack(s_v), jnp.stack(s_f))
```

```python
import functools
import math

import jax
import jax.numpy as jnp
import numpy as np
from jax import lax
from jax.experimental import pallas as pl
from jax.experimental.pallas import tpu as pltpu

F32 = jnp.float32
BF16 = jnp.bfloat16

RET_HEADS = 4
RET_QK_DIM = 128
RET_V_DIM = 256
ROPE_BASE = 10000.0
FOX_HEADS = 16
FOX_HEAD_DIM = 64
N_GROUPS = 4
EXPERTS_PER_GROUP = 8
N_EXPERTS = N_GROUPS * EXPERTS_PER_GROUP
LN_EPS = 1e-5
GN_EPS = 1e-6
RET_QK_W = RET_HEADS * RET_QK_DIM
RET_V_W = RET_HEADS * RET_V_DIM
FOX_W = FOX_HEADS * FOX_HEAD_DIM
FOX_PAIRS = FOX_HEADS // 2

LANES = 128
QUERY_CHUNK = 512
VMEM_LIMIT_BYTES = 56 * 1024 * 1024
MASK_VALUE = -1e30


def _dot(a, b):
    return jnp.dot(a, b, preferred_element_type=F32)


def _dot_nt(a, b):
    return lax.dot_general(a, b, (((1,), (1,)), ((), ())), preferred_element_type=F32)


def _dot_tn(a, b):
    return lax.dot_general(a, b, (((0,), (0,)), ((), ())), preferred_element_type=F32)


def _split3(x):
    hi = x.astype(BF16)
    r1 = x - hi.astype(F32)
    mid = r1.astype(BF16)
    lo = (r1 - mid.astype(F32)).astype(BF16)
    return hi, mid, lo


def _sigmoid(x):
    return 1.0 / (1.0 + jnp.exp(-x))


def _log_sigmoid(x):
    return -(jnp.maximum(-x, 0.0) + jnp.log1p(jnp.exp(-jnp.abs(x))))


def _params(*sem):
    return pltpu.CompilerParams(dimension_semantics=sem, vmem_limit_bytes=VMEM_LIMIT_BYTES)


def _resident(shape):
    nd = len(shape)
    return pl.BlockSpec(shape, lambda *_: (0,) * nd, pipeline_mode=pl.Buffered(1))


def _ada_kernel(c_ref, w_ref, b_ref, o_ref):
    c = c_ref[...]
    s = (c * _sigmoid(c)).astype(BF16)
    o_ref[...] = _dot(s, w_ref[...].astype(BF16)) + b_ref[...]


def _ada(c, w, b):
    n, d = c.shape
    nout = w.shape[1]
    tn = d
    return pl.pallas_call(
        _ada_kernel,
        out_shape=jax.ShapeDtypeStruct((n, nout), F32),
        grid=(nout // tn,),
        in_specs=[pl.BlockSpec((n, d), lambda j: (0, 0)),
                  pl.BlockSpec((d, tn), lambda j: (0, j)),
                  pl.BlockSpec((1, tn), lambda j: (0, j))],
        out_specs=pl.BlockSpec((n, tn), lambda j: (0, j)),
        compiler_params=_params("arbitrary"),
        name="ada",
    )(c, w, b.reshape(1, nout))


def _inproj_kernel(x_ref, mod_ref, w_ref, b_ref, cq_ref, sq_ref, ck_ref, sk_ref,
                   rq_ref, rk_ref, rv_ref, rg_ref, fq_ref, fk_ref, fv_ref, fkb_ref, fvb_ref,
                   lf_ref, ga_ref, gb_ref, *, transposed_v):
    gb_, tl, d = x_ref.shape
    tm = gb_ * tl
    sh = mod_ref[:, :, 0:d]
    sc = mod_ref[:, :, d:2 * d]
    h = (x_ref[...] * (1.0 + sc) + sh).reshape(tm, d).astype(BF16)

    def proj(lo, width):
        return _dot(h, w_ref[:, lo:lo + width]) + b_ref[:, lo:lo + width]

    def put(ref, val):
        ref[...] = val.reshape(ref.shape).astype(ref.dtype)

    def rot(z, c_ref, s_ref):
        c = c_ref[...]
        s = s_ref[...]
        parts = []
        for hh in range(RET_HEADS):
            zh = z[:, hh * RET_QK_DIM:(hh + 1) * RET_QK_DIM]
            parts.append(zh * c + pltpu.roll(zh, RET_QK_DIM // 2, axis=1) * s)
        return jnp.concatenate(parts, axis=1)

    off = 0
    put(rq_ref, rot(proj(off, RET_QK_W), cq_ref, sq_ref)); off += RET_QK_W
    put(rk_ref, rot(proj(off, RET_QK_W), ck_ref, sk_ref)); off += RET_QK_W
    put(rv_ref, proj(off, RET_V_W)); off += RET_V_W
    z = proj(off, RET_V_W); off += RET_V_W
    put(rg_ref, z * _sigmoid(z))
    put(fq_ref, proj(off, FOX_W) * (FOX_HEAD_DIM ** -0.5)); off += FOX_W
    z = proj(off, FOX_W); off += FOX_W
    put(fk_ref, z); put(fkb_ref, z)
    z = proj(off, FOX_W); off += FOX_W
    put(fv_ref, z)
    if transposed_v:
        fvb_ref[0] = z.T.astype(fvb_ref.dtype)
    else:
        put(fvb_ref, z)
    put(ga_ref, _sigmoid(proj(off, d))); off += d
    put(gb_ref, _sigmoid(proj(off, d))); off += d
    put(lf_ref, _log_sigmoid(proj(off, LANES)))


def _inproj(x, mod, w2, b2, tabs, gb_, tl):
    bx, l, d = x.shape
    tm = gb_ * tl
    nw = w2.shape[1]
    grid = (bx // gb_, l // tl)
    if gb_ == 1:
        tab_spec = pl.BlockSpec((tl, LANES), lambda b, j: (j, 0))
    else:
        tab_spec = pl.BlockSpec((tm, LANES), lambda b, j: (0, 0))

    def out(width, dtype):
        return (jax.ShapeDtypeStruct((bx, l, width), dtype),
                pl.BlockSpec((gb_, tl, width), lambda b, j: (b, j, 0)))

    transposed_v = gb_ == 1
    if transposed_v:
        fvb = (jax.ShapeDtypeStruct((bx, FOX_W, l), BF16), pl.BlockSpec((1, FOX_W, tl), lambda b, j: (b, 0, j)))
    else:
        fvb = out(FOX_W, BF16)
    outs = [out(RET_QK_W, BF16), out(RET_QK_W, BF16), out(RET_V_W, BF16), out(RET_V_W, BF16),
            out(FOX_W, BF16), out(FOX_W, F32), out(FOX_W, F32), out(FOX_W, BF16), fvb,
            out(LANES, F32), out(d, BF16), out(d, BF16)]
    return pl.pallas_call(
        functools.partial(_inproj_kernel, transposed_v=transposed_v),
        out_shape=[o[0] for o in outs],
        grid=grid,
        in_specs=[pl.BlockSpec((gb_, tl, d), lambda b, j: (b, j, 0)),
                  pl.BlockSpec((gb_, 1, mod.shape[-1]), lambda b, j: (b, 0, 0)),
                  _resident((d, nw)), _resident((1, nw)),
                  tab_spec, tab_spec, tab_spec, tab_spec],
        out_specs=[o[1] for o in outs],
        compiler_params=_params("arbitrary", "arbitrary"),
        name="inproj",
    )(x, mod, w2, b2, *tabs)


BIAS_PIECES = 3


def _bias_selector():
    sel = np.zeros((BIAS_PIECES * LANES, FOX_PAIRS * LANES), np.float32)
    for h in range(FOX_HEADS):
        for p in range(BIAS_PIECES):
            sel[p * LANES + h, LANES * (h // 2) + BIAS_PIECES * (h % 2) + p] = 1.0
    return sel


def _fcum_kernel(lf_ref, tril_ref, sel_ref, fp_ref, carry_ref):
    @pl.when(pl.program_id(1) == 0)
    def _():
        carry_ref[...] = jnp.zeros_like(carry_ref)

    tl = lf_ref.shape[1]
    hi, mid, lo = _split3(lf_ref[0])
    tril = tril_ref[...]
    cum = _dot(tril, hi) + _dot(tril, mid) + _dot(tril, lo) + carry_ref[...]
    carry_ref[...] = cum[tl - 1:tl, :]
    pieces = jnp.concatenate(_split3(-cum), axis=1)
    fp_ref[0] = _dot(pieces, sel_ref[...]).astype(fp_ref.dtype)


def _fcum(lf, tl):
    bx, l, _ = lf.shape
    tril = jnp.asarray(np.tril(np.ones((tl, tl), np.float32)), BF16)
    sel = jnp.asarray(_bias_selector(), BF16)
    return pl.pallas_call(
        _fcum_kernel,
        out_shape=jax.ShapeDtypeStruct((bx, l, FOX_PAIRS * LANES), BF16),
        grid=(bx, l // tl),
        in_specs=[pl.BlockSpec((1, tl, LANES), lambda b, j: (b, j, 0)),
                  pl.BlockSpec((tl, tl), lambda b, j: (0, 0)),
                  pl.BlockSpec(sel.shape, lambda b, j: (0, 0))],
        out_specs=pl.BlockSpec((1, tl, FOX_PAIRS * LANES), lambda b, j: (b, j, 0)),
        scratch_shapes=[pltpu.VMEM((1, LANES), F32)],
        compiler_params=_params("arbitrary", "arbitrary"),
        name="fcum",
    )(lf, tril, sel)


def _retention_kernel(*refs, has_state):
    if has_state:
        (q_ref, k_ref, v_ref, g_ref, dec_ref, qd_ref, kd_ref, gw_ref, gb_ref, s0_ref,
         y_ref, st_ref) = refs
    else:
        (q_ref, k_ref, v_ref, g_ref, dec_ref, qd_ref, kd_ref, gw_ref, gb_ref,
         y_ref, st_ref) = refs
        s0_ref = None
    lc = q_ref.shape[1]

    @pl.when(pl.program_id(1) == 0)
    def _():
        if has_state:
            st_ref[...] = s0_ref[...]
        else:
            st_ref[...] = jnp.zeros_like(st_ref)

    for hh in range(RET_HEADS):
        qs = slice(hh * RET_QK_DIM, (hh + 1) * RET_QK_DIM)
        vs = slice(hh * RET_V_DIM, (hh + 1) * RET_V_DIM)
        q = q_ref[0, :, qs]
        k = k_ref[0, :, qs]
        v = v_ref[0, :, vs]
        state = st_ref[0, hh]
        qd = qd_ref[hh]
        scores = _dot_nt(q, k) * dec_ref[hh]
        inner = _dot(scores.astype(BF16), v)
        cross = _dot(q, state.astype(BF16)) * qd
        o = inner + cross
        kdec = (k.astype(F32) * kd_ref[hh]).astype(BF16)
        st_ref[0, hh] = qd[lc - 1:lc, :] * state + _dot_tn(kdec, v)
        mu = jnp.mean(o, axis=-1, keepdims=True)
        oc = o - mu
        var = jnp.mean(oc * oc, axis=-1, keepdims=True)
        on = oc * lax.rsqrt(var + GN_EPS) * gw_ref[:, vs] + gb_ref[:, vs]
        y_ref[0, :, vs] = (g_ref[0, :, vs].astype(F32) * on).astype(y_ref.dtype)


def _retention_tables(log_gamma, lc):
    n = jnp.arange(lc, dtype=F32)
    diff = n[:, None] - n[None, :]
    decay = jnp.where(diff[None] >= 0, jnp.exp(jnp.maximum(diff, 0.0)[None] * log_gamma[:, None, None]), 0.0)
    qdec = jnp.exp((n + 1.0)[None, :, None] * log_gamma[:, None, None])
    kdec = jnp.exp((lc - 1.0 - n)[None, :, None] * log_gamma[:, None, None])
    return decay, qdec, kdec


def _retention(q, k, v, g, gn_w, gn_b, log_gamma, lc, state0=None):
    bx, l, _ = q.shape
    decay, qdec, kdec = _retention_tables(log_gamma, lc)
    has_state = state0 is not None
    seq = lambda w: pl.BlockSpec((1, lc, w), lambda b, c: (b, c, 0))
    whole = lambda a: pl.BlockSpec(a.shape, lambda b, c: (0,) * a.ndim)
    st_spec = pl.BlockSpec((1, RET_HEADS, RET_QK_DIM, RET_V_DIM), lambda b, c: (b, 0, 0, 0))
    gw = gn_w.reshape(1, RET_V_W)
    gb = gn_b.reshape(1, RET_V_W)
    args = [q, k, v, g, decay, qdec, kdec, gw, gb]
    in_specs = [seq(RET_QK_W), seq(RET_QK_W), seq(RET_V_W), seq(RET_V_W),
                whole(decay), whole(qdec), whole(kdec), whole(gw), whole(gb)]
    if has_state:
        args.append(state0)
        in_specs.append(st_spec)
    return pl.pallas_call(
        functools.partial(_retention_kernel, has_state=has_state),
        out_shape=[jax.ShapeDtypeStruct((bx, l, RET_V_W), BF16),
                   jax.ShapeDtypeStruct((bx, RET_HEADS, RET_QK_DIM, RET_V_DIM), F32)],
        grid=(bx, l // lc),
        in_specs=in_specs,
        out_specs=[seq(RET_V_W), st_spec],
        compiler_params=_params("arbitrary", "arbitrary"),
        name="retention",
    )(*args)


def _pair_queries(q2):
    t = q2.shape[0]
    lane = lax.broadcasted_iota(jnp.int32, (t, LANES), 1)
    out = []
    for i in range(2):
        head = (lane >= i * FOX_HEAD_DIM) & (lane < (i + 1) * FOX_HEAD_DIM)
        ones = (lane >= i * BIAS_PIECES) & (lane < (i + 1) * BIAS_PIECES)
        out.append(jnp.concatenate([jnp.where(head, q2, jnp.zeros_like(q2)),
                                    jnp.where(ones, 1.0, 0.0).astype(q2.dtype)], axis=1))
    return out


def _fox_prompt_kernel(q_ref, k_ref, fp_ref, vt_ref, o_ref, m_ref, l_ref, acc_ref):
    t = q_ref.shape[1]
    qi = pl.program_id(2)
    qc = _pair_queries(q_ref[0])
    m_ref[...] = jnp.full_like(m_ref, MASK_VALUE)
    l_ref[...] = jnp.zeros_like(l_ref)
    acc_ref[...] = jnp.zeros_like(acc_ref)

    tq = min(t, QUERY_CHUNK)
    tiles = [(i, c) for i in range(2) for c in range(t // tq)]
    lookahead = 2

    def step(ki, masked):
        start = pl.multiple_of(ki * t, t)
        kc = jnp.concatenate([k_ref[0, pl.ds(start, t), :], fp_ref[0, pl.ds(start, t), :]], axis=1)

        def scores(n):
            i, c = tiles[n]
            return _dot_nt(kc, qc[i][c * tq:(c + 1) * tq, :])

        pending = [scores(n) for n in range(min(lookahead, len(tiles)))]
        for n, (i, c) in enumerate(tiles):
            rows = slice(i * FOX_HEAD_DIM, (i + 1) * FOX_HEAD_DIM)
            cols = slice(c * tq, (c + 1) * tq)
            s = pending.pop(0)
            if masked:
                key = lax.broadcasted_iota(jnp.int32, (t, tq), 0)
                qry = lax.broadcasted_iota(jnp.int32, (t, tq), 1) + c * tq
                s = jnp.where(qry >= key, s, MASK_VALUE)
            m_prev = m_ref[i, :, cols]
            m_new = jnp.maximum(m_prev, jnp.max(s, axis=0, keepdims=True))
            alpha = jnp.exp(m_prev - m_new)
            p = jnp.exp(s - m_new)
            l_ref[i, :, cols] = alpha * l_ref[i, :, cols] + jnp.sum(p, axis=0, keepdims=True)
            m_ref[i, :, cols] = m_new
            pv = _dot(vt_ref[0, rows, pl.ds(start, t)], p.astype(BF16))
            acc_ref[rows, cols] = acc_ref[rows, cols] * alpha + pv
            if n + lookahead < len(tiles):
                pending.append(scores(n + lookahead))

    def body(ki, carry):
        step(ki, False)
        return carry

    lax.fori_loop(0, qi, body, 0)
    step(qi, True)
    out_t = jnp.concatenate(
        [acc_ref[i * FOX_HEAD_DIM:(i + 1) * FOX_HEAD_DIM, :] * (1.0 / l_ref[i]) for i in range(2)], axis=0)
    o_ref[0] = out_t.T.astype(o_ref.dtype)


def _fox_prompt(q, k, fp, vt, t):
    b, s, _ = q.shape
    kv = lambda bi, j, qi: (bi, 0, j)
    return pl.pallas_call(
        _fox_prompt_kernel,
        out_shape=jax.ShapeDtypeStruct((b, s, FOX_W), BF16),
        grid=(b, FOX_PAIRS, s // t),
        in_specs=[pl.BlockSpec((1, t, LANES), lambda bi, j, qi: (bi, qi, j)),
                  pl.BlockSpec((1, s, LANES), kv),
                  pl.BlockSpec((1, s, LANES), kv),
                  pl.BlockSpec((1, LANES, s), lambda bi, j, qi: (bi, j, 0))],
        out_specs=pl.BlockSpec((1, t, LANES), lambda bi, j, qi: (bi, qi, j)),
        scratch_shapes=[pltpu.VMEM((2, 1, t), F32), pltpu.VMEM((2, 1, t), F32),
                        pltpu.VMEM((LANES, t), F32)],
        compiler_params=_params("arbitrary", "arbitrary", "arbitrary"),
        name="fox_prompt",
    )(q, k, fp, vt)


def _fox_sample_kernel(q_ref, kc_ref, vc_ref, kn_ref, vn_ref, fp_ref, o_ref):
    l = q_ref.shape[1]
    past = kc_ref.shape[1]
    lane = lax.broadcasted_iota(jnp.int32, (1, LANES), 1)
    head_mask = (lane < FOX_HEAD_DIM, lane >= FOX_HEAD_DIM)
    qc = _pair_queries(q_ref[0])
    kc = jnp.concatenate([kc_ref[0].astype(BF16), fp_ref[0, 0:past, :]], axis=1)
    kn = jnp.concatenate([kn_ref[0], fp_ref[0, past:past + l, :]], axis=1)
    vc = vc_ref[0].astype(BF16)
    vn = vn_ref[0]
    row = lax.broadcasted_iota(jnp.int32, (l, l), 0)
    col = lax.broadcasted_iota(jnp.int32, (l, l), 1)
    out = jnp.zeros((l, LANES), F32)
    for i in range(2):
        s_c = _dot_nt(qc[i], kc)
        s_n = jnp.where(row >= col, _dot_nt(qc[i], kn), MASK_VALUE)
        m = jnp.maximum(jnp.max(s_c, axis=-1, keepdims=True), jnp.max(s_n, axis=-1, keepdims=True))
        p_c = jnp.exp(s_c - m)
        p_n = jnp.exp(s_n - m)
        denom = jnp.sum(p_c, axis=-1, keepdims=True) + jnp.sum(p_n, axis=-1, keepdims=True)
        vch = jnp.where(head_mask[i], vc, jnp.zeros_like(vc))
        vnh = jnp.where(head_mask[i], vn, jnp.zeros_like(vn))
        out = out + (_dot(p_c.astype(BF16), vch) + _dot(p_n.astype(BF16), vnh)) * (1.0 / denom)
    o_ref[0] = out.astype(o_ref.dtype)


def _fox_sample(q, kn, vn, cache_k, cache_v, fp):
    b, l, _ = q.shape
    past = cache_k.shape[1]
    new = pl.BlockSpec((1, l, LANES), lambda bi, j: (bi, 0, j))
    old = pl.BlockSpec((1, past, LANES), lambda bi, j: (bi, 0, j))
    return pl.pallas_call(
        _fox_sample_kernel,
        out_shape=jax.ShapeDtypeStruct((b, l, FOX_W), BF16),
        grid=(b, FOX_PAIRS),
        in_specs=[new, old, old, new, new,
                  pl.BlockSpec((1, fp.shape[1], LANES), lambda bi, j: (bi, 0, j))],
        out_specs=new,
        compiler_params=_params("arbitrary", "arbitrary"),
        name="fox_sample",
    )(q, cache_k, cache_v, kn, vn, fp)


def _layer_norm(x, w, b):
    mu = jnp.mean(x, axis=-1, keepdims=True)
    xc = x - mu
    var = jnp.mean(xc * xc, axis=-1, keepdims=True)
    return xc * lax.rsqrt(var + LN_EPS) * w + b


def _first_argmax_rows(x, n):
    rows = lax.broadcasted_iota(jnp.int32, x.shape, 0).astype(F32)
    mx = jnp.max(x, axis=0, keepdims=True)
    idx = jnp.min(jnp.where(x == mx, rows, float(n)), axis=0, keepdims=True)
    return mx, idx.astype(jnp.int32)


def _outproj_kernel(yr_ref, yf_ref, ga_ref, gb_ref, x_ref, mod_ref, wr_ref, wf_ref, wo_ref,
                    lw_ref, lb_ref, rw_ref, rb_ref,
                    x1_ref, h2_ref, eid_ref, gate_ref, gt_ref, *, alpha):
    gb_, tl, d = x_ref.shape
    tm = gb_ * tl
    flat = lambda ref: ref[...].reshape(tm, ref.shape[-1])
    y_ret = _dot(flat(yr_ref), wr_ref[...])
    y_fox = _dot(flat(yf_ref), wf_ref[...])
    mix = flat(ga_ref).astype(F32) * y_ret + flat(gb_ref).astype(F32) * y_fox
    mixed = _dot(mix.astype(BF16), wo_ref[...]).reshape(gb_, tl, d)
    g1 = mod_ref[:, :, 2 * d:3 * d]
    sh2 = mod_ref[:, :, 3 * d:4 * d]
    sc2 = mod_ref[:, :, 4 * d:5 * d]
    x1 = _layer_norm(alpha * x_ref[...] + (1.0 + g1) * mixed, lw_ref[...], lb_ref[...])
    x1_ref[...] = x1
    h2 = (x1 * (1.0 + sc2) + sh2)
    h2_ref[...] = h2
    h_hi, h_mid, h_lo = _split3(h2.reshape(tm, d))
    w_hi, w_mid, w_lo = rw_ref[0], rw_ref[1], rw_ref[2]
    lt = (_dot_nt(w_hi, h_hi) + _dot_nt(w_hi, h_mid) + _dot_nt(w_mid, h_hi)
          + _dot_nt(w_hi, h_lo) + _dot_nt(w_lo, h_hi) + _dot_nt(w_mid, h_mid)) + rb_ref[...]
    gl = lt[N_EXPERTS:N_EXPERTS + N_GROUPS, :]
    gmax, gi = _first_argmax_rows(gl, N_GROUPS)
    g_p = 1.0 / jnp.sum(jnp.exp(gl - gmax), axis=0, keepdims=True)
    e_sel = lt[0:EXPERTS_PER_GROUP, :]
    for g in range(1, N_GROUPS):
        e_sel = jnp.where(gi == g, lt[g * EXPERTS_PER_GROUP:(g + 1) * EXPERTS_PER_GROUP, :], e_sel)
    rows = lax.broadcasted_iota(jnp.int32, e_sel.shape, 0)
    m1, i1 = _first_argmax_rows(e_sel, EXPERTS_PER_GROUP)
    m2, i2 = _first_argmax_rows(jnp.where(rows == i1, -jnp.inf, e_sel), EXPERTS_PER_GROUP)
    r = jnp.exp(m2 - m1)
    gate0 = g_p / (1.0 + r)
    gate1 = g_p * r / (1.0 + r)
    eid_ref[0:1, :] = gi * EXPERTS_PER_GROUP + i1
    eid_ref[1:2, :] = gi * EXPERTS_PER_GROUP + i2
    gate_ref[0:1, :] = gate0
    gate_ref[1:2, :] = gate1
    rr = lax.broadcasted_iota(jnp.int32, (LANES, tm), 0)
    gsq = jnp.where(rr == 0, gate0, jnp.where(rr == 1, gate1, 0.0))
    gt_ref[...] = gsq.T


def _outproj(yr, yf, ga, gb, x, mod, wr, wf, wo, lw, lb, rw3, rb, gb_, tl, alpha):
    bx, l, d = x.shape
    m = bx * l
    tm = gb_ * tl
    nl = l // tl
    seq = lambda w: pl.BlockSpec((gb_, tl, w), lambda b, j: (b, j, 0))
    tok = lambda r: pl.BlockSpec((r, tm), lambda b, j: (0, b * nl + j))
    return pl.pallas_call(
        functools.partial(_outproj_kernel, alpha=alpha),
        out_shape=[jax.ShapeDtypeStruct((bx, l, d), F32), jax.ShapeDtypeStruct((bx, l, d), F32),
                   jax.ShapeDtypeStruct((2, m), jnp.int32), jax.ShapeDtypeStruct((2, m), F32),
                   jax.ShapeDtypeStruct((m, LANES), F32)],
        grid=(bx // gb_, nl),
        in_specs=[seq(RET_V_W), seq(FOX_W), seq(d), seq(d), seq(d),
                  pl.BlockSpec((gb_, 1, mod.shape[-1]), lambda b, j: (b, 0, 0)),
                  _resident(wr.shape), _resident(wf.shape), _resident(wo.shape),
                  _resident(lw.shape), _resident(lb.shape), _resident(rw3.shape), _resident(rb.shape)],
        out_specs=[seq(d), seq(d), tok(2), tok(2),
                   pl.BlockSpec((tm, LANES), lambda b, j: (b * nl + j, 0))],
        compiler_params=_params("arbitrary", "arbitrary"),
        name="outproj",
    )(yr, yf, ga, gb, x, mod, wr, wf, wo, lw, lb, rw3, rb)


def _rank_kernel(eid_ref, triu_ref, rank_ref, cnt_ref):
    @pl.when(pl.program_id(0) == 0)
    def _():
        cnt_ref[...] = jnp.zeros_like(cnt_ref)

    ta = eid_ref.shape[1]
    experts = lax.broadcasted_iota(jnp.int32, (N_EXPERTS, ta), 0)
    carry = cnt_ref[...]
    ranks = []
    for kk in range(2):
        hit = eid_ref[kk:kk + 1, :] == experts
        onehot = jnp.where(hit, 1.0, 0.0)
        before = _dot(onehot.astype(BF16), triu_ref[...]) + carry
        rank = jnp.sum(jnp.where(hit, before, 0.0), axis=0, keepdims=True)
        rank_ref[kk:kk + 1, :] = rank.astype(jnp.int32)
        carry = carry + jnp.sum(onehot, axis=1, keepdims=True)
    cnt_ref[...] = carry


def _rank(eid, ta):
    m = eid.shape[1]
    triu = jnp.asarray(np.triu(np.ones((ta, ta), np.float32), 1), BF16)
    return pl.pallas_call(
        _rank_kernel,
        out_shape=[jax.ShapeDtypeStruct((2, m), jnp.int32), jax.ShapeDtypeStruct((N_EXPERTS, 1), F32)],
        grid=(m // ta,),
        in_specs=[pl.BlockSpec((2, ta), lambda i: (0, i)), pl.BlockSpec((ta, ta), lambda i: (0, 0))],
        out_specs=[pl.BlockSpec((2, ta), lambda i: (0, i)), pl.BlockSpec((N_EXPERTS, 1), lambda i: (0, 0))],
        compiler_params=_params("arbitrary"),
        name="moe_rank",
    )(eid, triu)


def _dest_kernel(eid_ref, rank_ref, pstart_ref, dest_ref):
    ta = eid_ref.shape[1]
    experts = lax.broadcasted_iota(jnp.int32, (N_EXPERTS, ta), 0)
    for kk in range(2):
        hit = eid_ref[kk:kk + 1, :] == experts
        start = jnp.sum(jnp.where(hit, pstart_ref[...], 0.0), axis=0, keepdims=True)
        dest_ref[kk:kk + 1, :] = start.astype(jnp.int32) + rank_ref[kk:kk + 1, :]


def _dest(eid, rank, pstart, ta):
    m = eid.shape[1]
    tok = pl.BlockSpec((2, ta), lambda i: (0, i))
    return pl.pallas_call(
        _dest_kernel,
        out_shape=jax.ShapeDtypeStruct((2, m), jnp.int32),
        grid=(m // ta,),
        in_specs=[tok, tok, pl.BlockSpec((N_EXPERTS, 1), lambda i: (0, 0))],
        out_specs=tok,
        compiler_params=_params("arbitrary"),
        name="moe_dest",
    )(eid, rank, pstart)


def _dispatch_kernel(dest_ref, h_ref, xs_in_ref, xs_ref, sem):
    del xs_in_ref
    tm = h_ref.shape[0]

    def copy(t, kk):
        return pltpu.make_async_copy(h_ref.at[pl.ds(t, 1)], xs_ref.at[pl.ds(dest_ref[kk, t], 1)], sem)

    def start(t, c):
        copy(t, 0).start()
        copy(t, 1).start()
        return c

    def wait(t, c):
        copy(t, 0).wait()
        copy(t, 1).wait()
        return c

    lax.fori_loop(0, tm, start, 0)
    lax.fori_loop(0, tm, wait, 0)


def _dispatch(dest, h2, n_slots, tm):
    m, d = h2.shape
    xs0 = jnp.zeros((n_slots, d), h2.dtype)
    return pl.pallas_call(
        _dispatch_kernel,
        out_shape=jax.ShapeDtypeStruct((n_slots, d), h2.dtype),
        grid=(m // tm,),
        in_specs=[pl.BlockSpec((2, tm), lambda i: (0, i), memory_space=pltpu.SMEM),
                  pl.BlockSpec((tm, d), lambda i: (i, 0)),
                  pl.BlockSpec(memory_space=pl.ANY)],
        out_specs=pl.BlockSpec(memory_space=pl.ANY),
        scratch_shapes=[pltpu.SemaphoreType.DMA],
        input_output_aliases={2: 0},
        compiler_params=_params("arbitrary"),
        name="moe_dispatch",
    )(dest, h2, xs0)


def _expert_kernel(be_ref, x_ref, w1_ref, w2_ref, y_ref):
    del be_ref
    e = w2_ref.shape[1]
    au = _dot(x_ref[...].astype(BF16), w1_ref[0])
    a = au[:, :e]
    u = au[:, e:]
    y_ref[...] = _dot((a * _sigmoid(a) * u).astype(BF16), w2_ref[0])


def _experts(block_e, xs, w1, w2, tb):
    p, d = xs.shape
    e = w2.shape[1]
    return pl.pallas_call(
        _expert_kernel,
        out_shape=jax.ShapeDtypeStruct((p, d), F32),
        grid_spec=pltpu.PrefetchScalarGridSpec(
            num_scalar_prefetch=1,
            grid=(p // tb,),
            in_specs=[pl.BlockSpec((tb, d), lambda i, be: (i, 0)),
                      pl.BlockSpec((1, d, 2 * e), lambda i, be: (be[i], 0, 0)),
                      pl.BlockSpec((1, e, d), lambda i, be: (be[i], 0, 0))],
            out_specs=pl.BlockSpec((tb, d), lambda i, be: (i, 0))),
        compiler_params=_params("arbitrary"),
        name="moe_experts",
    )(block_e, xs, w1, w2)


def _combine_kernel(dest_ref, y_hbm, gt_ref, x1_ref, mod_ref, lw_ref, lb_ref, o_ref, buf0, buf1, sem, *, alpha):
    gb_, tl, d = x1_ref.shape
    tm = gb_ * tl
    bufs = (buf0, buf1)

    def copy(t, kk):
        return pltpu.make_async_copy(y_hbm.at[pl.ds(dest_ref[kk, t], 1)], bufs[kk].at[pl.ds(t, 1)], sem)

    def start(t, c):
        copy(t, 0).start()
        copy(t, 1).start()
        return c

    def wait(t, c):
        copy(t, 0).wait()
        copy(t, 1).wait()
        return c

    lax.fori_loop(0, tm, start, 0)
    lax.fori_loop(0, tm, wait, 0)
    y = gt_ref[:, 0:1] * buf0[...] + gt_ref[:, 1:2] * buf1[...]
    g2 = mod_ref[:, :, 5 * d:6 * d]
    o_ref[...] = _layer_norm(alpha * x1_ref[...] + (1.0 + g2) * y.reshape(gb_, tl, d), lw_ref[...], lb_ref[...])


def _combine(dest, y, gt, x1, mod, lw, lb, gb_, tl, alpha):
    bx, l, d = x1.shape
    tm = gb_ * tl
    nl = l // tl
    seq = pl.BlockSpec((gb_, tl, d), lambda b, j: (b, j, 0))
    return pl.pallas_call(
        functools.partial(_combine_kernel, alpha=alpha),
        out_shape=jax.ShapeDtypeStruct((bx, l, d), F32),
        grid=(bx // gb_, nl),
        in_specs=[pl.BlockSpec((2, tm), lambda b, j: (0, b * nl + j), memory_space=pltpu.SMEM),
                  pl.BlockSpec(memory_space=pl.ANY),
                  pl.BlockSpec((tm, LANES), lambda b, j: (b * nl + j, 0)),
                  seq,
                  pl.BlockSpec((gb_, 1, mod.shape[-1]), lambda b, j: (b, 0, 0)),
                  _resident(lw.shape), _resident(lb.shape)],
        out_specs=seq,
        scratch_shapes=[pltpu.VMEM((tm, d), F32), pltpu.VMEM((tm, d), F32), pltpu.SemaphoreType.DMA],
        compiler_params=_params("arbitrary", "arbitrary"),
        name="moe_combine",
    )(dest, y, gt, x1, mod, lw, lb)


def _moe(h2, eid, gt, x1, mod, w1, w2, lw, lb, gb_, tl, alpha, tb, ta):
    bx, l, d = x1.shape
    m = bx * l
    rank, counts = _rank(eid, ta)
    cnt = counts[:, 0].astype(jnp.int32)
    pcnt = (cnt + tb - 1) // tb * tb
    pend = jnp.cumsum(pcnt)
    pstart = (pend - pcnt).astype(jnp.int32)
    n_blocks = -(-(2 * m + N_EXPERTS * (tb - 1)) // tb)
    block_row0 = jnp.arange(n_blocks, dtype=jnp.int32) * tb
    block_e = jnp.minimum(jnp.sum((pend[None, :] <= block_row0[:, None]).astype(jnp.int32), axis=1),
                          N_EXPERTS - 1).astype(jnp.int32)
    dest = _dest(eid, rank, pstart.astype(F32).reshape(N_EXPERTS, 1), ta)
    xs = _dispatch(dest, h2.reshape(m, d), n_blocks * tb, gb_ * tl)
    y = _experts(block_e, xs, w1, w2, tb)
    return _combine(dest, y, gt, x1, mod, lw, lb, gb_, tl, alpha)


def _rotary_tables(pos, reps):
    half = RET_QK_DIM // 2
    inv = ROPE_BASE ** (-jnp.linspace(0.0, 1.0, half, dtype=F32))
    ang = pos.astype(F32)[:, None] * inv[None, :]
    cos = jnp.cos(ang)
    sin = jnp.sin(ang)
    c2 = jnp.concatenate([cos, cos], axis=1)
    s2 = jnp.concatenate([-sin, sin], axis=1)
    kscale = RET_QK_DIM ** -0.5
    tabs = (c2, s2, c2 * kscale, s2 * kscale)
    return tuple(jnp.tile(t, (reps, 1)) for t in tabs)


def _pick(n, pref):
    t = min(n, pref)
    while n % t:
        t //= 2
    return t


def _layer(x, mod, pos, weights, log_gamma, alpha, cache=None):
    (w2, b2, gn_w, gn_b, wr, wf, wo, ln1w, ln1b, rw3, rb, we1, we2, ln2w, ln2b) = weights
    bx, l, d = x.shape
    if cache is None:
        gb_, tl = 1, _pick(l, 512)
    else:
        gb_, tl = _pick(bx, max(1, 512 // l)), l
    tabs = _rotary_tables(pos, gb_ if cache is not None else 1)
    rq, rk, rv, rg, fq, fk, fv, fkb, fvb, lf, ga, gb = _inproj(x, mod, w2, b2, tabs, gb_, tl)

    if cache is None:
        lc = _pick(l, 256)
        y_ret, state = _retention(rq, rk, rv, rg, gn_w, gn_b, log_gamma, lc)
        fp = _fcum(lf, _pick(l, 256))
        y_fox = _fox_prompt(fq, fkb, fp, fvb, _pick(l, 512))
    else:
        state0, cache_k, cache_v, cache_logf = cache
        y_ret, state = _retention(rq, rk, rv, rg, gn_w, gn_b, log_gamma, l, state0)
        past = cache_k.shape[1]
        tf = LANES
        total = -(-(past + l) // tf) * tf
        lf_all = jnp.concatenate(
            [jnp.pad(cache_logf.astype(F32), ((0, 0), (0, 0), (0, LANES - FOX_HEADS))), lf,
             jnp.zeros((bx, total - past - l, LANES), F32)], axis=1)
        fp = _fcum(lf_all, tf)
        y_fox = _fox_sample(fq, fkb, fvb, cache_k.reshape(bx, past, FOX_W), cache_v.reshape(bx, past, FOX_W), fp)

    x1, h2, eid, gate, gt = _outproj(y_ret, y_fox, ga, gb, x, mod, wr, wf, wo, ln1w, ln1b, rw3, rb,
                                     gb_, tl, alpha)
    del gate
    m = bx * l
    out = _moe(h2, eid, gt, x1, mod, we1, we2, ln2w, ln2b, gb_, tl, alpha, tb=256, ta=_pick(m, 1024))
    return out, state, fk, fv, lf[:, :, :FOX_HEADS]


def kernel(x_prompt, x_sample, state_ret, cache_fox_k, cache_fox_v, cache_fox_logf, c_prompt, c_sample,
           w_ada, b_ada, w_in, b_in, ret_gn_w, ret_gn_b, w_ret_proj, w_fox_proj, w_o, ln1_w, ln1_b,
           w_rg, b_rg, w_re, b_re, w_e_in, w_e_out, ln2_w, ln2_b):
    depth = w_ada.shape[0]
    d = x_prompt.shape[-1]
    bp, s, _ = x_prompt.shape
    bs, ls, _ = x_sample.shape
    past = cache_fox_k.shape[2]
    alpha = (2 * depth) ** 0.25
    log_gamma = jnp.log1p(-jnp.exp(jnp.linspace(math.log(1.0 / 32), math.log(1.0 / 512), RET_HEADS, dtype=F32)))
    pos_p = jnp.arange(s, dtype=jnp.int32)
    pos_s = past + jnp.arange(ls, dtype=jnp.int32)
    fg = 2 * RET_QK_W + 2 * RET_V_W + 3 * FOX_W

    xp, xs = x_prompt, x_sample
    p_ret, p_k, p_v, p_f, s_ret, s_k, s_v, s_f = [], [], [], [], [], [], [], []
    for li in range(depth):
        w2 = jnp.concatenate([w_in[li][:, :fg], w_in[li][:, fg + FOX_HEADS:],
                              jnp.pad(w_in[li][:, fg:fg + FOX_HEADS], ((0, 0), (0, LANES - FOX_HEADS)))],
                             axis=1).astype(BF16)
        b2 = jnp.concatenate([b_in[li][:fg], b_in[li][fg + FOX_HEADS:],
                              jnp.pad(b_in[li][fg:fg + FOX_HEADS], (0, LANES - FOX_HEADS))]).reshape(1, -1)
        n_rt = N_EXPERTS + N_GROUPS
        rt_rows = -(-n_rt // 8) * 8
        rwt = jnp.pad(jnp.concatenate([w_re[li], w_rg[li]], axis=1).T.astype(F32), ((0, rt_rows - n_rt), (0, 0)))
        r_hi = rwt.astype(BF16)
        r_mid = (rwt - r_hi.astype(F32)).astype(BF16)
        r_lo = (rwt - r_hi.astype(F32) - r_mid.astype(F32)).astype(BF16)
        rw3 = jnp.stack([r_hi, r_mid, r_lo])
        rb = jnp.pad(jnp.concatenate([b_re[li], b_rg[li]]).astype(F32), (0, rt_rows - n_rt)).reshape(rt_rows, 1)
        weights = (w2, b2, ret_gn_w[li], ret_gn_b[li],
                   w_ret_proj[li].astype(BF16), w_fox_proj[li].astype(BF16), w_o[li].astype(BF16),
                   ln1_w[li].reshape(1, d), ln1_b[li].reshape(1, d), rw3, rb,
                   w_e_in[li].astype(BF16), w_e_out[li].astype(BF16),
                   ln2_w[li].reshape(1, d), ln2_b[li].reshape(1, d))
        mod = _ada(jnp.concatenate([c_prompt, c_sample], axis=0), w_ada[li], b_ada[li])
        mod_p = mod[:bp].reshape(bp, 1, 6 * d)
        mod_s = mod[bp:].reshape(bs, 1, 6 * d)
        xp, st, kk, vv, ff = _layer(xp, mod_p, pos_p, weights, log_gamma, alpha)
        p_ret.append(st); p_k.append(kk.reshape(bp, s, FOX_HEADS, FOX_HEAD_DIM))
        p_v.append(vv.reshape(bp, s, FOX_HEADS, FOX_HEAD_DIM)); p_f.append(ff)
        cache = (state_ret[li].astype(F32), cache_fox_k[li], cache_fox_v[li], cache_fox_logf[li])
        xs, st, kk, vv, ff = _layer(xs, mod_s, pos_s, weights, log_gamma, alpha, cache)
        s_ret.append(st); s_k.append(kk.reshape(bs, ls, FOX_HEADS, FOX_HEAD_DIM))
        s_v.append(vv.reshape(bs, ls, FOX_HEADS, FOX_HEAD_DIM)); s_f.append(ff)
    return (xp, xs, jnp.stack(p_ret), jnp.stack(p_k), jnp.stack(p_v), jnp.stack(p_f),
            jnp.stack(s_ret), jnp.stack(s_k), jnp.stack(s_v), jnp.stack(s_f))
```

```python
import functools
import math

import jax
import jax.numpy as jnp
import numpy as np
from jax import lax
from jax.experimental import pallas as pl
from jax.experimental.pallas import tpu as pltpu

F32 = jnp.float32
BF16 = jnp.bfloat16

RET_HEADS = 4
RET_QK_DIM = 128
RET_V_DIM = 256
ROPE_BASE = 10000.0
FOX_HEADS = 16
FOX_HEAD_DIM = 64
N_GROUPS = 4
EXPERTS_PER_GROUP = 8
N_EXPERTS = N_GROUPS * EXPERTS_PER_GROUP
LN_EPS = 1e-5
GN_EPS = 1e-6
RET_QK_W = RET_HEADS * RET_QK_DIM
RET_V_W = RET_HEADS * RET_V_DIM
FOX_W = FOX_HEADS * FOX_HEAD_DIM
FOX_PAIRS = FOX_HEADS // 2

LANES = 128
VMEM_LIMIT_BYTES = 56 * 1024 * 1024
MASK_VALUE = -1e30


def _dot(a, b):
    return jnp.dot(a, b, preferred_element_type=F32)


def _dot_nt(a, b):
    return lax.dot_general(a, b, (((1,), (1,)), ((), ())), preferred_element_type=F32)


def _dot_tn(a, b):
    return lax.dot_general(a, b, (((0,), (0,)), ((), ())), preferred_element_type=F32)


def _split3(x):
    hi = x.astype(BF16)
    r1 = x - hi.astype(F32)
    mid = r1.astype(BF16)
    lo = (r1 - mid.astype(F32)).astype(BF16)
    return hi, mid, lo


def _sigmoid(x):
    return 1.0 / (1.0 + jnp.exp(-x))


def _log_sigmoid(x):
    return -(jnp.maximum(-x, 0.0) + jnp.log1p(jnp.exp(-jnp.abs(x))))


def _params(*sem):
    return pltpu.CompilerParams(dimension_semantics=sem, vmem_limit_bytes=VMEM_LIMIT_BYTES)


def _resident(shape):
    nd = len(shape)
    return pl.BlockSpec(shape, lambda *_: (0,) * nd, pipeline_mode=pl.Buffered(1))


def _ada_kernel(c_ref, w_ref, b_ref, o_ref):
    c = c_ref[...]
    s = (c * _sigmoid(c)).astype(BF16)
    o_ref[...] = _dot(s, w_ref[...].astype(BF16)) + b_ref[...]


def _ada(c, w, b):
    n, d = c.shape
    nout = w.shape[1]
    tn = d
    return pl.pallas_call(
        _ada_kernel,
        out_shape=jax.ShapeDtypeStruct((n, nout), F32),
        grid=(nout // tn,),
        in_specs=[pl.BlockSpec((n, d), lambda j: (0, 0)),
                  pl.BlockSpec((d, tn), lambda j: (0, j)),
                  pl.BlockSpec((1, tn), lambda j: (0, j))],
        out_specs=pl.BlockSpec((n, tn), lambda j: (0, j)),
        compiler_params=_params("arbitrary"),
        name="ada",
    )(c, w, b.reshape(1, nout))


def _inproj_kernel(x_ref, mod_ref, w_ref, b_ref, cq_ref, sq_ref, ck_ref, sk_ref,
                   rq_ref, rk_ref, rv_ref, rg_ref, fq_ref, fk_ref, fv_ref, fkb_ref, fvb_ref,
                   lf_ref, ga_ref, gb_ref, *, transposed_v):
    gb_, tl, d = x_ref.shape
    tm = gb_ * tl
    sh = mod_ref[:, :, 0:d]
    sc = mod_ref[:, :, d:2 * d]
    h = (x_ref[...] * (1.0 + sc) + sh).reshape(tm, d).astype(BF16)

    def proj(lo, width):
        return _dot(h, w_ref[:, lo:lo + width]) + b_ref[:, lo:lo + width]

    def put(ref, val):
        ref[...] = val.reshape(ref.shape).astype(ref.dtype)

    def rot(z, c_ref, s_ref):
        c = c_ref[...]
        s = s_ref[...]
        parts = []
        for hh in range(RET_HEADS):
            zh = z[:, hh * RET_QK_DIM:(hh + 1) * RET_QK_DIM]
            parts.append(zh * c + pltpu.roll(zh, RET_QK_DIM // 2, axis=1) * s)
        return jnp.concatenate(parts, axis=1)

    off = 0
    put(rq_ref, rot(proj(off, RET_QK_W), cq_ref, sq_ref)); off += RET_QK_W
    put(rk_ref, rot(proj(off, RET_QK_W), ck_ref, sk_ref)); off += RET_QK_W
    put(rv_ref, proj(off, RET_V_W)); off += RET_V_W
    z = proj(off, RET_V_W); off += RET_V_W
    put(rg_ref, z * _sigmoid(z))
    put(fq_ref, proj(off, FOX_W) * (FOX_HEAD_DIM ** -0.5)); off += FOX_W
    z = proj(off, FOX_W); off += FOX_W
    put(fk_ref, z); put(fkb_ref, z)
    z = proj(off, FOX_W); off += FOX_W
    put(fv_ref, z)
    if transposed_v:
        fvb_ref[0] = z.T.astype(fvb_ref.dtype)
    else:
        put(fvb_ref, z)
    put(ga_ref, _sigmoid(proj(off, d))); off += d
    put(gb_ref, _sigmoid(proj(off, d))); off += d
    put(lf_ref, _log_sigmoid(proj(off, LANES)))


def _inproj(x, mod, w2, b2, tabs, gb_, tl):
    bx, l, d = x.shape
    tm = gb_ * tl
    nw = w2.shape[1]
    grid = (bx // gb_, l // tl)
    if gb_ == 1:
        tab_spec = pl.BlockSpec((tl, LANES), lambda b, j: (j, 0))
    else:
        tab_spec = pl.BlockSpec((tm, LANES), lambda b, j: (0, 0))

    def out(width, dtype):
        return (jax.ShapeDtypeStruct((bx, l, width), dtype),
                pl.BlockSpec((gb_, tl, width), lambda b, j: (b, j, 0)))

    transposed_v = gb_ == 1
    if transposed_v:
        fvb = (jax.ShapeDtypeStruct((bx, FOX_W, l), BF16), pl.BlockSpec((1, FOX_W, tl), lambda b, j: (b, 0, j)))
    else:
        fvb = out(FOX_W, BF16)
    outs = [out(RET_QK_W, BF16), out(RET_QK_W, BF16), out(RET_V_W, BF16), out(RET_V_W, BF16),
            out(FOX_W, BF16), out(FOX_W, F32), out(FOX_W, F32), out(FOX_W, BF16), fvb,
            out(LANES, F32), out(d, BF16), out(d, BF16)]
    return pl.pallas_call(
        functools.partial(_inproj_kernel, transposed_v=transposed_v),
        out_shape=[o[0] for o in outs],
        grid=grid,
        in_specs=[pl.BlockSpec((gb_, tl, d), lambda b, j: (b, j, 0)),
                  pl.BlockSpec((gb_, 1, mod.shape[-1]), lambda b, j: (b, 0, 0)),
                  _resident((d, nw)), _resident((1, nw)),
                  tab_spec, tab_spec, tab_spec, tab_spec],
        out_specs=[o[1] for o in outs],
        compiler_params=_params("arbitrary", "arbitrary"),
        name="inproj",
    )(x, mod, w2, b2, *tabs)


BIAS_PIECES = 3


def _bias_selector():
    sel = np.zeros((BIAS_PIECES * LANES, FOX_PAIRS * LANES), np.float32)
    for h in range(FOX_HEADS):
        for p in range(BIAS_PIECES):
            sel[p * LANES + h, LANES * (h // 2) + BIAS_PIECES * (h % 2) + p] = 1.0
    return sel


def _fcum_kernel(lf_ref, tril_ref, sel_ref, fp_ref, carry_ref):
    @pl.when(pl.program_id(1) == 0)
    def _():
        carry_ref[...] = jnp.zeros_like(carry_ref)

    tl = lf_ref.shape[1]
    hi, mid, lo = _split3(lf_ref[0])
    tril = tril_ref[...]
    cum = _dot(tril, hi) + _dot(tril, mid) + _dot(tril, lo) + carry_ref[...]
    carry_ref[...] = cum[tl - 1:tl, :]
    pieces = jnp.concatenate(_split3(-cum), axis=1)
    fp_ref[0] = _dot(pieces, sel_ref[...]).astype(fp_ref.dtype)


def _fcum(lf, tl):
    bx, l, _ = lf.shape
    tril = jnp.asarray(np.tril(np.ones((tl, tl), np.float32)), BF16)
    sel = jnp.asarray(_bias_selector(), BF16)
    return pl.pallas_call(
        _fcum_kernel,
        out_shape=jax.ShapeDtypeStruct((bx, l, FOX_PAIRS * LANES), BF16),
        grid=(bx, l // tl),
        in_specs=[pl.BlockSpec((1, tl, LANES), lambda b, j: (b, j, 0)),
                  pl.BlockSpec((tl, tl), lambda b, j: (0, 0)),
                  pl.BlockSpec(sel.shape, lambda b, j: (0, 0))],
        out_specs=pl.BlockSpec((1, tl, FOX_PAIRS * LANES), lambda b, j: (b, j, 0)),
        scratch_shapes=[pltpu.VMEM((1, LANES), F32)],
        compiler_params=_params("arbitrary", "arbitrary"),
        name="fcum",
    )(lf, tril, sel)


def _retention_kernel(*refs, has_state):
    if has_state:
        (q_ref, k_ref, v_ref, g_ref, dec_ref, qd_ref, kd_ref, gw_ref, gb_ref, s0_ref,
         y_ref, st_ref) = refs
    else:
        (q_ref, k_ref, v_ref, g_ref, dec_ref, qd_ref, kd_ref, gw_ref, gb_ref,
         y_ref, st_ref) = refs
        s0_ref = None
    lc = q_ref.shape[1]

    @pl.when(pl.program_id(1) == 0)
    def _():
        if has_state:
            st_ref[...] = s0_ref[...]
        else:
            st_ref[...] = jnp.zeros_like(st_ref)

    for hh in range(RET_HEADS):
        qs = slice(hh * RET_QK_DIM, (hh + 1) * RET_QK_DIM)
        vs = slice(hh * RET_V_DIM, (hh + 1) * RET_V_DIM)
        q = q_ref[0, :, qs]
        k = k_ref[0, :, qs]
        v = v_ref[0, :, vs]
        state = st_ref[0, hh]
        qd = qd_ref[hh]
        scores = _dot_nt(q, k) * dec_ref[hh]
        inner = _dot(scores.astype(BF16), v)
        cross = _dot(q, state.astype(BF16)) * qd
        o = inner + cross
        kdec = (k.astype(F32) * kd_ref[hh]).astype(BF16)
        st_ref[0, hh] = qd[lc - 1:lc, :] * state + _dot_tn(kdec, v)
        mu = jnp.mean(o, axis=-1, keepdims=True)
        oc = o - mu
        var = jnp.mean(oc * oc, axis=-1, keepdims=True)
        on = oc * lax.rsqrt(var + GN_EPS) * gw_ref[:, vs] + gb_ref[:, vs]
        y_ref[0, :, vs] = (g_ref[0, :, vs].astype(F32) * on).astype(y_ref.dtype)


def _retention_tables(log_gamma, lc):
    n = jnp.arange(lc, dtype=F32)
    diff = n[:, None] - n[None, :]
    decay = jnp.where(diff[None] >= 0, jnp.exp(jnp.maximum(diff, 0.0)[None] * log_gamma[:, None, None]), 0.0)
    qdec = jnp.exp((n + 1.0)[None, :, None] * log_gamma[:, None, None])
    kdec = jnp.exp((lc - 1.0 - n)[None, :, None] * log_gamma[:, None, None])
    return decay, qdec, kdec


def _retention(q, k, v, g, gn_w, gn_b, log_gamma, lc, state0=None):
    bx, l, _ = q.shape
    decay, qdec, kdec = _retention_tables(log_gamma, lc)
    has_state = state0 is not None
    seq = lambda w: pl.BlockSpec((1, lc, w), lambda b, c: (b, c, 0))
    whole = lambda a: pl.BlockSpec(a.shape, lambda b, c: (0,) * a.ndim)
    st_spec = pl.BlockSpec((1, RET_HEADS, RET_QK_DIM, RET_V_DIM), lambda b, c: (b, 0, 0, 0))
    gw = gn_w.reshape(1, RET_V_W)
    gb = gn_b.reshape(1, RET_V_W)
    args = [q, k, v, g, decay, qdec, kdec, gw, gb]
    in_specs = [seq(RET_QK_W), seq(RET_QK_W), seq(RET_V_W), seq(RET_V_W),
                whole(decay), whole(qdec), whole(kdec), whole(gw), whole(gb)]
    if has_state:
        args.append(state0)
        in_specs.append(st_spec)
    return pl.pallas_call(
        functools.partial(_retention_kernel, has_state=has_state),
        out_shape=[jax.ShapeDtypeStruct((bx, l, RET_V_W), BF16),
                   jax.ShapeDtypeStruct((bx, RET_HEADS, RET_QK_DIM, RET_V_DIM), F32)],
        grid=(bx, l // lc),
        in_specs=in_specs,
        out_specs=[seq(RET_V_W), st_spec],
        compiler_params=_params("arbitrary", "arbitrary"),
        name="retention",
    )(*args)


def _pair_queries(q2):
    t = q2.shape[0]
    lane = lax.broadcasted_iota(jnp.int32, (t, LANES), 1)
    out = []
    for i in range(2):
        head = (lane >= i * FOX_HEAD_DIM) & (lane < (i + 1) * FOX_HEAD_DIM)
        ones = (lane >= i * BIAS_PIECES) & (lane < (i + 1) * BIAS_PIECES)
        out.append(jnp.concatenate([jnp.where(head, q2, jnp.zeros_like(q2)),
                                    jnp.where(ones, 1.0, 0.0).astype(q2.dtype)], axis=1))
    return out


def _fox_prompt_kernel(q_ref, k_ref, fp_ref, vt_ref, o_ref, m_ref, l_ref, acc_ref, qc_ref, sa_ref, sb_ref, *, t):
    nq = q_ref.shape[1] // t
    steps = [(qi, ki) for qi in range(nq) for ki in range(qi + 1)]
    bufs = (sa_ref, sb_ref)

    def produce(n, i):
        qi, ki = steps[n]
        if ki == 0 and i == 0:
            qc = _pair_queries(q_ref[0, qi * t:(qi + 1) * t, :])
            qc_ref[0] = qc[0]
            qc_ref[1] = qc[1]
        kc = jnp.concatenate([k_ref[0, ki * t:(ki + 1) * t, :], fp_ref[0, ki * t:(ki + 1) * t, :]], axis=1)
        bufs[n % 2][i] = _dot_nt(kc, qc_ref[i])

    def consume(n, i):
        qi, ki = steps[n]
        rows = slice(i * FOX_HEAD_DIM, (i + 1) * FOX_HEAD_DIM)
        s = bufs[n % 2][i]
        if ki == qi:
            key = lax.broadcasted_iota(jnp.int32, (t, t), 0)
            qry = lax.broadcasted_iota(jnp.int32, (t, t), 1)
            s = jnp.where(qry >= key, s, MASK_VALUE)
        smax = jnp.max(s, axis=0, keepdims=True)
        m_new = smax if ki == 0 else jnp.maximum(m_ref[i], smax)
        p = jnp.exp(s - m_new)
        psum = jnp.sum(p, axis=0, keepdims=True)
        pv = _dot(vt_ref[0, rows, ki * t:(ki + 1) * t], p.astype(BF16))
        if ki == 0:
            l_ref[i] = psum
            acc_ref[rows, :] = pv
        else:
            alpha = jnp.exp(m_ref[i] - m_new)
            l_ref[i] = alpha * l_ref[i] + psum
            acc_ref[rows, :] = acc_ref[rows, :] * alpha + pv
        m_ref[i] = m_new
        if ki == qi and i == 1:
            out_t = jnp.concatenate(
                [acc_ref[h * FOX_HEAD_DIM:(h + 1) * FOX_HEAD_DIM, :] * (1.0 / l_ref[h]) for h in range(2)], axis=0)
            o_ref[0, qi * t:(qi + 1) * t, :] = out_t.T.astype(o_ref.dtype)

    for i in range(2):
        produce(0, i)
    for n in range(len(steps)):
        for i in range(2):
            if n + 1 < len(steps):
                produce(n + 1, i)
            consume(n, i)


def _fox_prompt(q, k, fp, vt, t):
    b, s, _ = q.shape
    seq = pl.BlockSpec((1, s, LANES), lambda bi, j: (bi, 0, j))
    return pl.pallas_call(
        functools.partial(_fox_prompt_kernel, t=t),
        out_shape=jax.ShapeDtypeStruct((b, s, FOX_W), BF16),
        grid=(b, FOX_PAIRS),
        in_specs=[seq, seq, seq, pl.BlockSpec((1, LANES, s), lambda bi, j: (bi, j, 0))],
        out_specs=seq,
        scratch_shapes=[pltpu.VMEM((2, 1, t), F32), pltpu.VMEM((2, 1, t), F32),
                        pltpu.VMEM((LANES, t), F32), pltpu.VMEM((2, t, 2 * LANES), BF16),
                        pltpu.VMEM((2, t, t), F32), pltpu.VMEM((2, t, t), F32)],
        compiler_params=_params("arbitrary", "arbitrary"),
        name="fox_prompt",
    )(q, k, fp, vt)


def _fox_sample_kernel(q_ref, kc_ref, vc_ref, kn_ref, vn_ref, fp_ref, o_ref):
    l = q_ref.shape[1]
    past = kc_ref.shape[1]
    lane = lax.broadcasted_iota(jnp.int32, (1, LANES), 1)
    head_mask = (lane < FOX_HEAD_DIM, lane >= FOX_HEAD_DIM)
    qc = _pair_queries(q_ref[0])
    kc = jnp.concatenate([kc_ref[0].astype(BF16), fp_ref[0, 0:past, :]], axis=1)
    kn = jnp.concatenate([kn_ref[0], fp_ref[0, past:past + l, :]], axis=1)
    vc = vc_ref[0].astype(BF16)
    vn = vn_ref[0]
    row = lax.broadcasted_iota(jnp.int32, (l, l), 0)
    col = lax.broadcasted_iota(jnp.int32, (l, l), 1)
    out = jnp.zeros((l, LANES), F32)
    for i in range(2):
        s_c = _dot_nt(qc[i], kc)
        s_n = jnp.where(row >= col, _dot_nt(qc[i], kn), MASK_VALUE)
        m = jnp.maximum(jnp.max(s_c, axis=-1, keepdims=True), jnp.max(s_n, axis=-1, keepdims=True))
        p_c = jnp.exp(s_c - m)
        p_n = jnp.exp(s_n - m)
        denom = jnp.sum(p_c, axis=-1, keepdims=True) + jnp.sum(p_n, axis=-1, keepdims=True)
        vch = jnp.where(head_mask[i], vc, jnp.zeros_like(vc))
        vnh = jnp.where(head_mask[i], vn, jnp.zeros_like(vn))
        out = out + (_dot(p_c.astype(BF16), vch) + _dot(p_n.astype(BF16), vnh)) * (1.0 / denom)
    o_ref[0] = out.astype(o_ref.dtype)


def _fox_sample(q, kn, vn, cache_k, cache_v, fp):
    b, l, _ = q.shape
    past = cache_k.shape[1]
    new = pl.BlockSpec((1, l, LANES), lambda bi, j: (bi, 0, j))
    old = pl.BlockSpec((1, past, LANES), lambda bi, j: (bi, 0, j))
    return pl.pallas_call(
        _fox_sample_kernel,
        out_shape=jax.ShapeDtypeStruct((b, l, FOX_W), BF16),
        grid=(b, FOX_PAIRS),
        in_specs=[new, old, old, new, new,
                  pl.BlockSpec((1, fp.shape[1], LANES), lambda bi, j: (bi, 0, j))],
        out_specs=new,
        compiler_params=_params("arbitrary", "arbitrary"),
        name="fox_sample",
    )(q, cache_k, cache_v, kn, vn, fp)


def _layer_norm(x, w, b):
    mu = jnp.mean(x, axis=-1, keepdims=True)
    xc = x - mu
    var = jnp.mean(xc * xc, axis=-1, keepdims=True)
    return xc * lax.rsqrt(var + LN_EPS) * w + b


def _first_argmax_rows(x, n):
    rows = lax.broadcasted_iota(jnp.int32, x.shape, 0).astype(F32)
    mx = jnp.max(x, axis=0, keepdims=True)
    idx = jnp.min(jnp.where(x == mx, rows, float(n)), axis=0, keepdims=True)
    return mx, idx.astype(jnp.int32)


def _outproj_kernel(yr_ref, yf_ref, ga_ref, gb_ref, x_ref, mod_ref, wr_ref, wf_ref, wo_ref,
                    lw_ref, lb_ref, rw_ref, rb_ref,
                    x1_ref, h2_ref, eid_ref, gate_ref, gt_ref, *, alpha):
    gb_, tl, d = x_ref.shape
    tm = gb_ * tl
    flat = lambda ref: ref[...].reshape(tm, ref.shape[-1])
    y_ret = _dot(flat(yr_ref), wr_ref[...])
    y_fox = _dot(flat(yf_ref), wf_ref[...])
    mix = flat(ga_ref).astype(F32) * y_ret + flat(gb_ref).astype(F32) * y_fox
    mixed = _dot(mix.astype(BF16), wo_ref[...]).reshape(gb_, tl, d)
    g1 = mod_ref[:, :, 2 * d:3 * d]
    sh2 = mod_ref[:, :, 3 * d:4 * d]
    sc2 = mod_ref[:, :, 4 * d:5 * d]
    x1 = _layer_norm(alpha * x_ref[...] + (1.0 + g1) * mixed, lw_ref[...], lb_ref[...])
    x1_ref[...] = x1
    h2 = (x1 * (1.0 + sc2) + sh2)
    h2_ref[...] = h2
    h_hi, h_mid, h_lo = _split3(h2.reshape(tm, d))
    w_hi, w_mid, w_lo = rw_ref[0], rw_ref[1], rw_ref[2]
    lt = (_dot_nt(w_hi, h_hi) + _dot_nt(w_hi, h_mid) + _dot_nt(w_mid, h_hi)
          + _dot_nt(w_hi, h_lo) + _dot_nt(w_lo, h_hi) + _dot_nt(w_mid, h_mid)) + rb_ref[...]
    gl = lt[N_EXPERTS:N_EXPERTS + N_GROUPS, :]
    gmax, gi = _first_argmax_rows(gl, N_GROUPS)
    g_p = 1.0 / jnp.sum(jnp.exp(gl - gmax), axis=0, keepdims=True)
    e_sel = lt[0:EXPERTS_PER_GROUP, :]
    for g in range(1, N_GROUPS):
        e_sel = jnp.where(gi == g, lt[g * EXPERTS_PER_GROUP:(g + 1) * EXPERTS_PER_GROUP, :], e_sel)
    rows = lax.broadcasted_iota(jnp.int32, e_sel.shape, 0)
    m1, i1 = _first_argmax_rows(e_sel, EXPERTS_PER_GROUP)
    m2, i2 = _first_argmax_rows(jnp.where(rows == i1, -jnp.inf, e_sel), EXPERTS_PER_GROUP)
    r = jnp.exp(m2 - m1)
    gate0 = g_p / (1.0 + r)
    gate1 = g_p * r / (1.0 + r)
    eid_ref[0:1, :] = gi * EXPERTS_PER_GROUP + i1
    eid_ref[1:2, :] = gi * EXPERTS_PER_GROUP + i2
    gate_ref[0:1, :] = gate0
    gate_ref[1:2, :] = gate1
    rr = lax.broadcasted_iota(jnp.int32, (LANES, tm), 0)
    gsq = jnp.where(rr == 0, gate0, jnp.where(rr == 1, gate1, 0.0))
    gt_ref[...] = gsq.T


def _outproj(yr, yf, ga, gb, x, mod, wr, wf, wo, lw, lb, rw3, rb, gb_, tl, alpha):
    bx, l, d = x.shape
    m = bx * l
    tm = gb_ * tl
    nl = l // tl
    seq = lambda w: pl.BlockSpec((gb_, tl, w), lambda b, j: (b, j, 0))
    tok = lambda r: pl.BlockSpec((r, tm), lambda b, j: (0, b * nl + j))
    return pl.pallas_call(
        functools.partial(_outproj_kernel, alpha=alpha),
        out_shape=[jax.ShapeDtypeStruct((bx, l, d), F32), jax.ShapeDtypeStruct((bx, l, d), F32),
                   jax.ShapeDtypeStruct((2, m), jnp.int32), jax.ShapeDtypeStruct((2, m), F32),
                   jax.ShapeDtypeStruct((m, LANES), F32)],
        grid=(bx // gb_, nl),
        in_specs=[seq(RET_V_W), seq(FOX_W), seq(d), seq(d), seq(d),
                  pl.BlockSpec((gb_, 1, mod.shape[-1]), lambda b, j: (b, 0, 0)),
                  _resident(wr.shape), _resident(wf.shape), _resident(wo.shape),
                  _resident(lw.shape), _resident(lb.shape), _resident(rw3.shape), _resident(rb.shape)],
        out_specs=[seq(d), seq(d), tok(2), tok(2),
                   pl.BlockSpec((tm, LANES), lambda b, j: (b * nl + j, 0))],
        compiler_params=_params("arbitrary", "arbitrary"),
        name="outproj",
    )(yr, yf, ga, gb, x, mod, wr, wf, wo, lw, lb, rw3, rb)


def _rank_kernel(eid_ref, triu_ref, rank_ref, cnt_ref):
    @pl.when(pl.program_id(0) == 0)
    def _():
        cnt_ref[...] = jnp.zeros_like(cnt_ref)

    ta = eid_ref.shape[1]
    experts = lax.broadcasted_iota(jnp.int32, (N_EXPERTS, ta), 0)
    carry = cnt_ref[...]
    ranks = []
    for kk in range(2):
        hit = eid_ref[kk:kk + 1, :] == experts
        onehot = jnp.where(hit, 1.0, 0.0)
        before = _dot(onehot.astype(BF16), triu_ref[...]) + carry
        rank = jnp.sum(jnp.where(hit, before, 0.0), axis=0, keepdims=True)
        rank_ref[kk:kk + 1, :] = rank.astype(jnp.int32)
        carry = carry + jnp.sum(onehot, axis=1, keepdims=True)
    cnt_ref[...] = carry


def _rank(eid, ta):
    m = eid.shape[1]
    triu = jnp.asarray(np.triu(np.ones((ta, ta), np.float32), 1), BF16)
    return pl.pallas_call(
        _rank_kernel,
        out_shape=[jax.ShapeDtypeStruct((2, m), jnp.int32), jax.ShapeDtypeStruct((N_EXPERTS, 1), F32)],
        grid=(m // ta,),
        in_specs=[pl.BlockSpec((2, ta), lambda i: (0, i)), pl.BlockSpec((ta, ta), lambda i: (0, 0))],
        out_specs=[pl.BlockSpec((2, ta), lambda i: (0, i)), pl.BlockSpec((N_EXPERTS, 1), lambda i: (0, 0))],
        compiler_params=_params("arbitrary"),
        name="moe_rank",
    )(eid, triu)


def _dest_kernel(eid_ref, rank_ref, pstart_ref, dest_ref):
    ta = eid_ref.shape[1]
    experts = lax.broadcasted_iota(jnp.int32, (N_EXPERTS, ta), 0)
    for kk in range(2):
        hit = eid_ref[kk:kk + 1, :] == experts
        start = jnp.sum(jnp.where(hit, pstart_ref[...], 0.0), axis=0, keepdims=True)
        dest_ref[kk:kk + 1, :] = start.astype(jnp.int32) + rank_ref[kk:kk + 1, :]


def _dest(eid, rank, pstart, ta):
    m = eid.shape[1]
    tok = pl.BlockSpec((2, ta), lambda i: (0, i))
    return pl.pallas_call(
        _dest_kernel,
        out_shape=jax.ShapeDtypeStruct((2, m), jnp.int32),
        grid=(m // ta,),
        in_specs=[tok, tok, pl.BlockSpec((N_EXPERTS, 1), lambda i: (0, 0))],
        out_specs=tok,
        compiler_params=_params("arbitrary"),
        name="moe_dest",
    )(eid, rank, pstart)


def _dispatch_kernel(dest_ref, h_ref, xs_in_ref, xs_ref, sem):
    del xs_in_ref
    tm = h_ref.shape[0]

    def copy(t, kk):
        return pltpu.make_async_copy(h_ref.at[pl.ds(t, 1)], xs_ref.at[pl.ds(dest_ref[kk, t], 1)], sem)

    def start(t, c):
        copy(t, 0).start()
        copy(t, 1).start()
        return c

    def wait(t, c):
        copy(t, 0).wait()
        copy(t, 1).wait()
        return c

    lax.fori_loop(0, tm, start, 0)
    lax.fori_loop(0, tm, wait, 0)


def _dispatch(dest, h2, n_slots, tm):
    m, d = h2.shape
    xs0 = jnp.zeros((n_slots, d), h2.dtype)
    return pl.pallas_call(
        _dispatch_kernel,
        out_shape=jax.ShapeDtypeStruct((n_slots, d), h2.dtype),
        grid=(m // tm,),
        in_specs=[pl.BlockSpec((2, tm), lambda i: (0, i), memory_space=pltpu.SMEM),
                  pl.BlockSpec((tm, d), lambda i: (i, 0)),
                  pl.BlockSpec(memory_space=pl.ANY)],
        out_specs=pl.BlockSpec(memory_space=pl.ANY),
        scratch_shapes=[pltpu.SemaphoreType.DMA],
        input_output_aliases={2: 0},
        compiler_params=_params("arbitrary"),
        name="moe_dispatch",
    )(dest, h2, xs0)


def _expert_kernel(be_ref, x_ref, w1_ref, w2_ref, y_ref):
    del be_ref
    e = w2_ref.shape[1]
    au = _dot(x_ref[...].astype(BF16), w1_ref[0])
    a = au[:, :e]
    u = au[:, e:]
    y_ref[...] = _dot((a * _sigmoid(a) * u).astype(BF16), w2_ref[0])


def _experts(block_e, xs, w1, w2, tb):
    p, d = xs.shape
    e = w2.shape[1]
    return pl.pallas_call(
        _expert_kernel,
        out_shape=jax.ShapeDtypeStruct((p, d), F32),
        grid_spec=pltpu.PrefetchScalarGridSpec(
            num_scalar_prefetch=1,
            grid=(p // tb,),
            in_specs=[pl.BlockSpec((tb, d), lambda i, be: (i, 0)),
                      pl.BlockSpec((1, d, 2 * e), lambda i, be: (be[i], 0, 0)),
                      pl.BlockSpec((1, e, d), lambda i, be: (be[i], 0, 0))],
            out_specs=pl.BlockSpec((tb, d), lambda i, be: (i, 0))),
        compiler_params=_params("arbitrary"),
        name="moe_experts",
    )(block_e, xs, w1, w2)


def _combine_kernel(dest_ref, y_hbm, gt_ref, x1_ref, mod_ref, lw_ref, lb_ref, o_ref, buf0, buf1, sem, *, alpha):
    gb_, tl, d = x1_ref.shape
    tm = gb_ * tl
    bufs = (buf0, buf1)

    def copy(t, kk):
        return pltpu.make_async_copy(y_hbm.at[pl.ds(dest_ref[kk, t], 1)], bufs[kk].at[pl.ds(t, 1)], sem)

    def start(t, c):
        copy(t, 0).start()
        copy(t, 1).start()
        return c

    def wait(t, c):
        copy(t, 0).wait()
        copy(t, 1).wait()
        return c

    lax.fori_loop(0, tm, start, 0)
    lax.fori_loop(0, tm, wait, 0)
    y = gt_ref[:, 0:1] * buf0[...] + gt_ref[:, 1:2] * buf1[...]
    g2 = mod_ref[:, :, 5 * d:6 * d]
    o_ref[...] = _layer_norm(alpha * x1_ref[...] + (1.0 + g2) * y.reshape(gb_, tl, d), lw_ref[...], lb_ref[...])


def _combine(dest, y, gt, x1, mod, lw, lb, gb_, tl, alpha):
    bx, l, d = x1.shape
    tm = gb_ * tl
    nl = l // tl
    seq = pl.BlockSpec((gb_, tl, d), lambda b, j: (b, j, 0))
    return pl.pallas_call(
        functools.partial(_combine_kernel, alpha=alpha),
        out_shape=jax.ShapeDtypeStruct((bx, l, d), F32),
        grid=(bx // gb_, nl),
        in_specs=[pl.BlockSpec((2, tm), lambda b, j: (0, b * nl + j), memory_space=pltpu.SMEM),
                  pl.BlockSpec(memory_space=pl.ANY),
                  pl.BlockSpec((tm, LANES), lambda b, j: (b * nl + j, 0)),
                  seq,
                  pl.BlockSpec((gb_, 1, mod.shape[-1]), lambda b, j: (b, 0, 0)),
                  _resident(lw.shape), _resident(lb.shape)],
        out_specs=seq,
        scratch_shapes=[pltpu.VMEM((tm, d), F32), pltpu.VMEM((tm, d), F32), pltpu.SemaphoreType.DMA],
        compiler_params=_params("arbitrary", "arbitrary"),
        name="moe_combine",
    )(dest, y, gt, x1, mod, lw, lb)


def _moe(h2, eid, gt, x1, mod, w1, w2, lw, lb, gb_, tl, alpha, tb, ta):
    bx, l, d = x1.shape
    m = bx * l
    rank, counts = _rank(eid, ta)
    cnt = counts[:, 0].astype(jnp.int32)
    pcnt = (cnt + tb - 1) // tb * tb
    pend = jnp.cumsum(pcnt)
    pstart = (pend - pcnt).astype(jnp.int32)
    n_blocks = -(-(2 * m + N_EXPERTS * (tb - 1)) // tb)
    block_row0 = jnp.arange(n_blocks, dtype=jnp.int32) * tb
    block_e = jnp.minimum(jnp.sum((pend[None, :] <= block_row0[:, None]).astype(jnp.int32), axis=1),
                          N_EXPERTS - 1).astype(jnp.int32)
    dest = _dest(eid, rank, pstart.astype(F32).reshape(N_EXPERTS, 1), ta)
    xs = _dispatch(dest, h2.reshape(m, d), n_blocks * tb, gb_ * tl)
    y = _experts(block_e, xs, w1, w2, tb)
    return _combine(dest, y, gt, x1, mod, lw, lb, gb_, tl, alpha)


def _rotary_tables(pos, reps):
    half = RET_QK_DIM // 2
    inv = ROPE_BASE ** (-jnp.linspace(0.0, 1.0, half, dtype=F32))
    ang = pos.astype(F32)[:, None] * inv[None, :]
    cos = jnp.cos(ang)
    sin = jnp.sin(ang)
    c2 = jnp.concatenate([cos, cos], axis=1)
    s2 = jnp.concatenate([-sin, sin], axis=1)
    kscale = RET_QK_DIM ** -0.5
    tabs = (c2, s2, c2 * kscale, s2 * kscale)
    return tuple(jnp.tile(t, (reps, 1)) for t in tabs)


def _pick(n, pref):
    t = min(n, pref)
    while n % t:
        t //= 2
    return t


def _layer(x, mod, pos, weights, log_gamma, alpha, cache=None):
    (w2, b2, gn_w, gn_b, wr, wf, wo, ln1w, ln1b, rw3, rb, we1, we2, ln2w, ln2b) = weights
    bx, l, d = x.shape
    if cache is None:
        gb_, tl = 1, _pick(l, 512)
    else:
        gb_, tl = _pick(bx, max(1, 512 // l)), l
    tabs = _rotary_tables(pos, gb_ if cache is not None else 1)
    rq, rk, rv, rg, fq, fk, fv, fkb, fvb, lf, ga, gb = _inproj(x, mod, w2, b2, tabs, gb_, tl)

    if cache is None:
        lc = _pick(l, 256)
        y_ret, state = _retention(rq, rk, rv, rg, gn_w, gn_b, log_gamma, lc)
        fp = _fcum(lf, _pick(l, 512))
        y_fox = _fox_prompt(fq, fkb, fp, fvb, _pick(l, 512))
    else:
        state0, cache_k, cache_v, cache_logf = cache
        y_ret, state = _retention(rq, rk, rv, rg, gn_w, gn_b, log_gamma, l, state0)
        past = cache_k.shape[1]
        total = -(-(past + l) // LANES) * LANES
        tf = total
        lf_all = jnp.concatenate(
            [jnp.pad(cache_logf.astype(F32), ((0, 0), (0, 0), (0, LANES - FOX_HEADS))), lf,
             jnp.zeros((bx, total - past - l, LANES), F32)], axis=1)
        fp = _fcum(lf_all, tf)
        y_fox = _fox_sample(fq, fkb, fvb, cache_k.reshape(bx, past, FOX_W), cache_v.reshape(bx, past, FOX_W), fp)

    x1, h2, eid, gate, gt = _outproj(y_ret, y_fox, ga, gb, x, mod, wr, wf, wo, ln1w, ln1b, rw3, rb,
                                     gb_, tl, alpha)
    del gate
    m = bx * l
    out = _moe(h2, eid, gt, x1, mod, we1, we2, ln2w, ln2b, gb_, tl, alpha, tb=256, ta=_pick(m, 1024))
    return out, state, fk, fv, lf[:, :, :FOX_HEADS]


def kernel(x_prompt, x_sample, state_ret, cache_fox_k, cache_fox_v, cache_fox_logf, c_prompt, c_sample,
           w_ada, b_ada, w_in, b_in, ret_gn_w, ret_gn_b, w_ret_proj, w_fox_proj, w_o, ln1_w, ln1_b,
           w_rg, b_rg, w_re, b_re, w_e_in, w_e_out, ln2_w, ln2_b):
    depth = w_ada.shape[0]
    d = x_prompt.shape[-1]
    bp, s, _ = x_prompt.shape
    bs, ls, _ = x_sample.shape
    past = cache_fox_k.shape[2]
    alpha = (2 * depth) ** 0.25
    log_gamma = jnp.log1p(-jnp.exp(jnp.linspace(math.log(1.0 / 32), math.log(1.0 / 512), RET_HEADS, dtype=F32)))
    pos_p = jnp.arange(s, dtype=jnp.int32)
    pos_s = past + jnp.arange(ls, dtype=jnp.int32)
    fg = 2 * RET_QK_W + 2 * RET_V_W + 3 * FOX_W

    xp, xs = x_prompt, x_sample
    p_ret, p_k, p_v, p_f, s_ret, s_k, s_v, s_f = [], [], [], [], [], [], [], []
    for li in range(depth):
        w2 = jnp.concatenate([w_in[li][:, :fg], w_in[li][:, fg + FOX_HEADS:],
                              jnp.pad(w_in[li][:, fg:fg + FOX_HEADS], ((0, 0), (0, LANES - FOX_HEADS)))],
                             axis=1).astype(BF16)
        b2 = jnp.concatenate([b_in[li][:fg], b_in[li][fg + FOX_HEADS:],
                              jnp.pad(b_in[li][fg:fg + FOX_HEADS], (0, LANES - FOX_HEADS))]).reshape(1, -1)
        n_rt = N_EXPERTS + N_GROUPS
        rt_rows = -(-n_rt // 8) * 8
        rwt = jnp.pad(jnp.concatenate([w_re[li], w_rg[li]], axis=1).T.astype(F32), ((0, rt_rows - n_rt), (0, 0)))
        r_hi = rwt.astype(BF16)
        r_mid = (rwt - r_hi.astype(F32)).astype(BF16)
        r_lo = (rwt - r_hi.astype(F32) - r_mid.astype(F32)).astype(BF16)
        rw3 = jnp.stack([r_hi, r_mid, r_lo])
        rb = jnp.pad(jnp.concatenate([b_re[li], b_rg[li]]).astype(F32), (0, rt_rows - n_rt)).reshape(rt_rows, 1)
        weights = (w2, b2, ret_gn_w[li], ret_gn_b[li],
                   w_ret_proj[li].astype(BF16), w_fox_proj[li].astype(BF16), w_o[li].astype(BF16),
                   ln1_w[li].reshape(1, d), ln1_b[li].reshape(1, d), rw3, rb,
                   w_e_in[li].astype(BF16), w_e_out[li].astype(BF16),
                   ln2_w[li].reshape(1, d), ln2_b[li].reshape(1, d))
        mod = _ada(jnp.concatenate([c_prompt, c_sample], axis=0), w_ada[li], b_ada[li])
        mod_p = mod[:bp].reshape(bp, 1, 6 * d)
        mod_s = mod[bp:].reshape(bs, 1, 6 * d)
        xp, st, kk, vv, ff = _layer(xp, mod_p, pos_p, weights, log_gamma, alpha)
        p_ret.append(st); p_k.append(kk.reshape(bp, s, FOX_HEADS, FOX_HEAD_DIM))
        p_v.append(vv.reshape(bp, s, FOX_HEADS, FOX_HEAD_DIM)); p_f.append(ff)
        cache = (state_ret[li].astype(F32), cache_fox_k[li], cache_fox_v[li], cache_fox_logf[li])
        xs, st, kk, vv, ff = _layer(xs, mod_s, pos_s, weights, log_gamma, alpha, cache)
        s_ret.append(st); s_k.append(kk.reshape(bs, ls, FOX_HEADS, FOX_HEAD_DIM))
        s_v.append(vv.reshape(bs, ls, FOX_HEADS, FOX_HEAD_DIM)); s_f.append(ff)
    return (xp, xs, jnp.stack(p_ret), jnp.stack(p_k), jnp.stack(p_v), jnp.stack(p_f),
            jnp.stack(s_ret), jnp.stack(s_k), jnp.stack(s_v), jnp.stack(s_f))
```

```python
import functools
import math

import jax
import jax.numpy as jnp
import numpy as np
from jax import lax
from jax.experimental import pallas as pl
from jax.experimental.pallas import tpu as pltpu

F32 = jnp.float32
BF16 = jnp.bfloat16

RET_HEADS = 4
RET_QK_DIM = 128
RET_V_DIM = 256
ROPE_BASE = 10000.0
FOX_HEADS = 16
FOX_HEAD_DIM = 64
N_GROUPS = 4
EXPERTS_PER_GROUP = 8
N_EXPERTS = N_GROUPS * EXPERTS_PER_GROUP
LN_EPS = 1e-5
GN_EPS = 1e-6
RET_QK_W = RET_HEADS * RET_QK_DIM
RET_V_W = RET_HEADS * RET_V_DIM
FOX_W = FOX_HEADS * FOX_HEAD_DIM
FOX_PAIRS = FOX_HEADS // 2

LANES = 128
VMEM_LIMIT_BYTES = 56 * 1024 * 1024
MASK_VALUE = -1e30


def _dot(a, b):
    return jnp.dot(a, b, preferred_element_type=F32)


def _dot_nt(a, b):
    return lax.dot_general(a, b, (((1,), (1,)), ((), ())), preferred_element_type=F32)


def _dot_tn(a, b):
    return lax.dot_general(a, b, (((0,), (0,)), ((), ())), preferred_element_type=F32)


def _split3(x):
    hi = x.astype(BF16)
    r1 = x - hi.astype(F32)
    mid = r1.astype(BF16)
    lo = (r1 - mid.astype(F32)).astype(BF16)
    return hi, mid, lo


def _sigmoid(x):
    return 1.0 / (1.0 + jnp.exp(-x))


def _log_sigmoid(x):
    return -(jnp.maximum(-x, 0.0) + jnp.log1p(jnp.exp(-jnp.abs(x))))


def _params(*sem):
    return pltpu.CompilerParams(dimension_semantics=sem, vmem_limit_bytes=VMEM_LIMIT_BYTES)


def _resident(shape):
    nd = len(shape)
    return pl.BlockSpec(shape, lambda *_: (0,) * nd, pipeline_mode=pl.Buffered(1))


def _ada_kernel(c_ref, w_ref, b_ref, o_ref):
    c = c_ref[...]
    s = (c * _sigmoid(c)).astype(BF16)
    o_ref[...] = _dot(s, w_ref[...].astype(BF16)) + b_ref[...]


def _ada(c, w, b):
    n, d = c.shape
    nout = w.shape[1]
    tn = d
    return pl.pallas_call(
        _ada_kernel,
        out_shape=jax.ShapeDtypeStruct((n, nout), F32),
        grid=(nout // tn,),
        in_specs=[pl.BlockSpec((n, d), lambda j: (0, 0)),
                  pl.BlockSpec((d, tn), lambda j: (0, j)),
                  pl.BlockSpec((1, tn), lambda j: (0, j))],
        out_specs=pl.BlockSpec((n, tn), lambda j: (0, j)),
        compiler_params=_params("arbitrary"),
        name="ada",
    )(c, w, b.reshape(1, nout))


def _inproj_kernel(x_ref, mod_ref, w_ref, b_ref, cq_ref, sq_ref, ck_ref, sk_ref,
                   rq_ref, rk_ref, rv_ref, rg_ref, fq_ref, fk_ref, fv_ref, fkb_ref, fvb_ref,
                   lf_ref, ga_ref, gb_ref, *, transposed_v):
    gb_, tl, d = x_ref.shape
    tm = gb_ * tl
    sh = mod_ref[:, :, 0:d]
    sc = mod_ref[:, :, d:2 * d]
    h = (x_ref[...] * (1.0 + sc) + sh).reshape(tm, d).astype(BF16)

    def proj(lo, width):
        return _dot(h, w_ref[:, lo:lo + width]) + b_ref[:, lo:lo + width]

    def put(ref, val):
        ref[...] = val.reshape(ref.shape).astype(ref.dtype)

    def rot(z, c_ref, s_ref):
        c = c_ref[...]
        s = s_ref[...]
        parts = []
        for hh in range(RET_HEADS):
            zh = z[:, hh * RET_QK_DIM:(hh + 1) * RET_QK_DIM]
            parts.append(zh * c + pltpu.roll(zh, RET_QK_DIM // 2, axis=1) * s)
        return jnp.concatenate(parts, axis=1)

    off = 0
    put(rq_ref, rot(proj(off, RET_QK_W), cq_ref, sq_ref)); off += RET_QK_W
    put(rk_ref, rot(proj(off, RET_QK_W), ck_ref, sk_ref)); off += RET_QK_W
    put(rv_ref, proj(off, RET_V_W)); off += RET_V_W
    z = proj(off, RET_V_W); off += RET_V_W
    put(rg_ref, z * _sigmoid(z))
    put(fq_ref, proj(off, FOX_W) * (FOX_HEAD_DIM ** -0.5)); off += FOX_W
    z = proj(off, FOX_W); off += FOX_W
    put(fk_ref, z); put(fkb_ref, z)
    z = proj(off, FOX_W); off += FOX_W
    put(fv_ref, z)
    if transposed_v:
        fvb_ref[0] = z.T.astype(fvb_ref.dtype)
    else:
        put(fvb_ref, z)
    put(ga_ref, _sigmoid(proj(off, d))); off += d
    put(gb_ref, _sigmoid(proj(off, d))); off += d
    put(lf_ref, _log_sigmoid(proj(off, LANES)))


def _inproj(x, mod, w2, b2, tabs, gb_, tl):
    bx, l, d = x.shape
    tm = gb_ * tl
    nw = w2.shape[1]
    grid = (bx // gb_, l // tl)
    if gb_ == 1:
        tab_spec = pl.BlockSpec((tl, LANES), lambda b, j: (j, 0))
    else:
        tab_spec = pl.BlockSpec((tm, LANES), lambda b, j: (0, 0))

    def out(width, dtype):
        return (jax.ShapeDtypeStruct((bx, l, width), dtype),
                pl.BlockSpec((gb_, tl, width), lambda b, j: (b, j, 0)))

    transposed_v = gb_ == 1
    if transposed_v:
        fvb = (jax.ShapeDtypeStruct((bx, FOX_W, l), BF16), pl.BlockSpec((1, FOX_W, tl), lambda b, j: (b, 0, j)))
    else:
        fvb = out(FOX_W, BF16)
    outs = [out(RET_QK_W, BF16), out(RET_QK_W, BF16), out(RET_V_W, BF16), out(RET_V_W, BF16),
            out(FOX_W, BF16), out(FOX_W, F32), out(FOX_W, F32), out(FOX_W, BF16), fvb,
            out(LANES, F32), out(d, BF16), out(d, BF16)]
    return pl.pallas_call(
        functools.partial(_inproj_kernel, transposed_v=transposed_v),
        out_shape=[o[0] for o in outs],
        grid=grid,
        in_specs=[pl.BlockSpec((gb_, tl, d), lambda b, j: (b, j, 0)),
                  pl.BlockSpec((gb_, 1, mod.shape[-1]), lambda b, j: (b, 0, 0)),
                  _resident((d, nw)), _resident((1, nw)),
                  tab_spec, tab_spec, tab_spec, tab_spec],
        out_specs=[o[1] for o in outs],
        compiler_params=_params("arbitrary", "arbitrary"),
        name="inproj",
    )(x, mod, w2, b2, *tabs)


BIAS_PIECES = 3


def _bias_selector():
    sel = np.zeros((BIAS_PIECES * LANES, FOX_PAIRS * LANES), np.float32)
    for h in range(FOX_HEADS):
        for p in range(BIAS_PIECES):
            sel[p * LANES + h, LANES * (h // 2) + BIAS_PIECES * (h % 2) + p] = 1.0
    return sel


def _fcum_kernel(lf_ref, tril_ref, sel_ref, fp_ref, carry_ref):
    @pl.when(pl.program_id(1) == 0)
    def _():
        carry_ref[...] = jnp.zeros_like(carry_ref)

    tl = lf_ref.shape[1]
    hi, mid, lo = _split3(lf_ref[0])
    tril = tril_ref[...]
    cum = _dot(tril, hi) + _dot(tril, mid) + _dot(tril, lo) + carry_ref[...]
    carry_ref[...] = cum[tl - 1:tl, :]
    pieces = jnp.concatenate(_split3(-cum), axis=1)
    fp_ref[0] = _dot(pieces, sel_ref[...]).astype(fp_ref.dtype)


def _fcum(lf, tl):
    bx, l, _ = lf.shape
    tril = jnp.asarray(np.tril(np.ones((tl, tl), np.float32)), BF16)
    sel = jnp.asarray(_bias_selector(), BF16)
    return pl.pallas_call(
        _fcum_kernel,
        out_shape=jax.ShapeDtypeStruct((bx, l, FOX_PAIRS * LANES), BF16),
        grid=(bx, l // tl),
        in_specs=[pl.BlockSpec((1, tl, LANES), lambda b, j: (b, j, 0)),
                  pl.BlockSpec((tl, tl), lambda b, j: (0, 0)),
                  pl.BlockSpec(sel.shape, lambda b, j: (0, 0))],
        out_specs=pl.BlockSpec((1, tl, FOX_PAIRS * LANES), lambda b, j: (b, j, 0)),
        scratch_shapes=[pltpu.VMEM((1, LANES), F32)],
        compiler_params=_params("arbitrary", "arbitrary"),
        name="fcum",
    )(lf, tril, sel)


def _retention_kernel(*refs, has_state):
    if has_state:
        (q_ref, k_ref, v_ref, g_ref, dec_ref, qd_ref, kd_ref, gw_ref, gb_ref, s0_ref,
         y_ref, st_ref) = refs
    else:
        (q_ref, k_ref, v_ref, g_ref, dec_ref, qd_ref, kd_ref, gw_ref, gb_ref,
         y_ref, st_ref) = refs
        s0_ref = None
    lc = q_ref.shape[1]

    @pl.when(pl.program_id(1) == 0)
    def _():
        if has_state:
            st_ref[...] = s0_ref[...]
        else:
            st_ref[...] = jnp.zeros_like(st_ref)

    for hh in range(RET_HEADS):
        qs = slice(hh * RET_QK_DIM, (hh + 1) * RET_QK_DIM)
        vs = slice(hh * RET_V_DIM, (hh + 1) * RET_V_DIM)
        q = q_ref[0, :, qs]
        k = k_ref[0, :, qs]
        v = v_ref[0, :, vs]
        state = st_ref[0, hh]
        qd = qd_ref[hh]
        scores = _dot_nt(q, k) * dec_ref[hh]
        inner = _dot(scores.astype(BF16), v)
        cross = _dot(q, state.astype(BF16)) * qd
        o = inner + cross
        kdec = (k.astype(F32) * kd_ref[hh]).astype(BF16)
        st_ref[0, hh] = qd[lc - 1:lc, :] * state + _dot_tn(kdec, v)
        mu = jnp.mean(o, axis=-1, keepdims=True)
        oc = o - mu
        var = jnp.mean(oc * oc, axis=-1, keepdims=True)
        on = oc * lax.rsqrt(var + GN_EPS) * gw_ref[:, vs] + gb_ref[:, vs]
        y_ref[0, :, vs] = (g_ref[0, :, vs].astype(F32) * on).astype(y_ref.dtype)


def _retention_tables(log_gamma, lc):
    n = jnp.arange(lc, dtype=F32)
    diff = n[:, None] - n[None, :]
    decay = jnp.where(diff[None] >= 0, jnp.exp(jnp.maximum(diff, 0.0)[None] * log_gamma[:, None, None]), 0.0)
    qdec = jnp.exp((n + 1.0)[None, :, None] * log_gamma[:, None, None])
    kdec = jnp.exp((lc - 1.0 - n)[None, :, None] * log_gamma[:, None, None])
    return decay, qdec, kdec


def _retention(q, k, v, g, gn_w, gn_b, log_gamma, lc, state0=None):
    bx, l, _ = q.shape
    decay, qdec, kdec = _retention_tables(log_gamma, lc)
    has_state = state0 is not None
    seq = lambda w: pl.BlockSpec((1, lc, w), lambda b, c: (b, c, 0))
    whole = lambda a: pl.BlockSpec(a.shape, lambda b, c: (0,) * a.ndim)
    st_spec = pl.BlockSpec((1, RET_HEADS, RET_QK_DIM, RET_V_DIM), lambda b, c: (b, 0, 0, 0))
    gw = gn_w.reshape(1, RET_V_W)
    gb = gn_b.reshape(1, RET_V_W)
    args = [q, k, v, g, decay, qdec, kdec, gw, gb]
    in_specs = [seq(RET_QK_W), seq(RET_QK_W), seq(RET_V_W), seq(RET_V_W),
                whole(decay), whole(qdec), whole(kdec), whole(gw), whole(gb)]
    if has_state:
        args.append(state0)
        in_specs.append(st_spec)
    return pl.pallas_call(
        functools.partial(_retention_kernel, has_state=has_state),
        out_shape=[jax.ShapeDtypeStruct((bx, l, RET_V_W), BF16),
                   jax.ShapeDtypeStruct((bx, RET_HEADS, RET_QK_DIM, RET_V_DIM), F32)],
        grid=(bx, l // lc),
        in_specs=in_specs,
        out_specs=[seq(RET_V_W), st_spec],
        compiler_params=_params("arbitrary", "arbitrary"),
        name="retention",
    )(*args)


def _pair_queries(q2):
    t = q2.shape[0]
    lane = lax.broadcasted_iota(jnp.int32, (t, LANES), 1)
    out = []
    for i in range(2):
        head = (lane >= i * FOX_HEAD_DIM) & (lane < (i + 1) * FOX_HEAD_DIM)
        ones = (lane >= i * BIAS_PIECES) & (lane < (i + 1) * BIAS_PIECES)
        out.append(jnp.concatenate([jnp.where(head, q2, jnp.zeros_like(q2)),
                                    jnp.where(ones, 1.0, 0.0).astype(q2.dtype)], axis=1))
    return out


def _fox_prompt_kernel(q_ref, k_ref, fp_ref, vt_ref, o_ref, m_ref, l_ref, acc_ref, qc_ref, sa_ref, sb_ref, *, t):
    nq = q_ref.shape[1] // t
    steps = [(qi, ki) for qi in range(nq) for ki in range(qi + 1)]
    bufs = (sa_ref, sb_ref)

    def produce(n, i):
        qi, ki = steps[n]
        if ki == 0 and i == 0:
            qc = _pair_queries(q_ref[0, qi * t:(qi + 1) * t, :])
            qc_ref[0] = qc[0]
            qc_ref[1] = qc[1]
        kc = jnp.concatenate([k_ref[0, ki * t:(ki + 1) * t, :], fp_ref[0, ki * t:(ki + 1) * t, :]], axis=1)
        bufs[n % 2][i] = _dot_nt(kc, qc_ref[i])

    def consume(n, i):
        qi, ki = steps[n]
        rows = slice(i * FOX_HEAD_DIM, (i + 1) * FOX_HEAD_DIM)
        s = bufs[n % 2][i]
        if ki == qi:
            key = lax.broadcasted_iota(jnp.int32, (t, t), 0)
            qry = lax.broadcasted_iota(jnp.int32, (t, t), 1)
            s = jnp.where(qry >= key, s, MASK_VALUE)
        smax = jnp.max(s, axis=0, keepdims=True)
        m_new = smax if ki == 0 else jnp.maximum(m_ref[i], smax)
        p = jnp.exp(s - m_new)
        psum = jnp.sum(p, axis=0, keepdims=True)
        pv = _dot(vt_ref[0, rows, ki * t:(ki + 1) * t], p.astype(BF16))
        if ki == 0:
            l_ref[i] = psum
            acc_ref[rows, :] = pv
        else:
            alpha = jnp.exp(m_ref[i] - m_new)
            l_ref[i] = alpha * l_ref[i] + psum
            acc_ref[rows, :] = acc_ref[rows, :] * alpha + pv
        m_ref[i] = m_new
        if ki == qi and i == 1:
            out_t = jnp.concatenate(
                [acc_ref[h * FOX_HEAD_DIM:(h + 1) * FOX_HEAD_DIM, :] * (1.0 / l_ref[h]) for h in range(2)], axis=0)
            o_ref[0, qi * t:(qi + 1) * t, :] = out_t.T.astype(o_ref.dtype)

    for i in range(2):
        produce(0, i)
    for n in range(len(steps)):
        for i in range(2):
            if n + 1 < len(steps):
                produce(n + 1, i)
            consume(n, i)


def _fox_prompt(q, k, fp, vt, t):
    b, s, _ = q.shape
    seq = pl.BlockSpec((1, s, LANES), lambda bi, j: (bi, 0, j))
    return pl.pallas_call(
        functools.partial(_fox_prompt_kernel, t=t),
        out_shape=jax.ShapeDtypeStruct((b, s, FOX_W), BF16),
        grid=(b, FOX_PAIRS),
        in_specs=[seq, seq, seq, pl.BlockSpec((1, LANES, s), lambda bi, j: (bi, j, 0))],
        out_specs=seq,
        scratch_shapes=[pltpu.VMEM((2, 1, t), F32), pltpu.VMEM((2, 1, t), F32),
                        pltpu.VMEM((LANES, t), F32), pltpu.VMEM((2, t, 2 * LANES), BF16),
                        pltpu.VMEM((2, t, t), F32), pltpu.VMEM((2, t, t), F32)],
        compiler_params=_params("arbitrary", "arbitrary"),
        name="fox_prompt",
    )(q, k, fp, vt)


def _fox_sample_kernel(q_ref, kc_ref, vc_ref, kn_ref, vn_ref, fp_ref, o_ref):
    l = q_ref.shape[1]
    past = kc_ref.shape[1]
    lane = lax.broadcasted_iota(jnp.int32, (1, LANES), 1)
    head_mask = (lane < FOX_HEAD_DIM, lane >= FOX_HEAD_DIM)
    qc = _pair_queries(q_ref[0])
    kc = jnp.concatenate([kc_ref[0].astype(BF16), fp_ref[0, 0:past, :]], axis=1)
    kn = jnp.concatenate([kn_ref[0], fp_ref[0, past:past + l, :]], axis=1)
    vc = vc_ref[0].astype(BF16)
    vn = vn_ref[0]
    row = lax.broadcasted_iota(jnp.int32, (l, l), 0)
    col = lax.broadcasted_iota(jnp.int32, (l, l), 1)
    out = jnp.zeros((l, LANES), F32)
    for i in range(2):
        s_c = _dot_nt(qc[i], kc)
        s_n = jnp.where(row >= col, _dot_nt(qc[i], kn), MASK_VALUE)
        m = jnp.maximum(jnp.max(s_c, axis=-1, keepdims=True), jnp.max(s_n, axis=-1, keepdims=True))
        p_c = jnp.exp(s_c - m)
        p_n = jnp.exp(s_n - m)
        denom = jnp.sum(p_c, axis=-1, keepdims=True) + jnp.sum(p_n, axis=-1, keepdims=True)
        vch = jnp.where(head_mask[i], vc, jnp.zeros_like(vc))
        vnh = jnp.where(head_mask[i], vn, jnp.zeros_like(vn))
        out = out + (_dot(p_c.astype(BF16), vch) + _dot(p_n.astype(BF16), vnh)) * (1.0 / denom)
    o_ref[0] = out.astype(o_ref.dtype)


def _fox_sample(q, kn, vn, cache_k, cache_v, fp):
    b, l, _ = q.shape
    past = cache_k.shape[1]
    new = pl.BlockSpec((1, l, LANES), lambda bi, j: (bi, 0, j))
    old = pl.BlockSpec((1, past, LANES), lambda bi, j: (bi, 0, j))
    return pl.pallas_call(
        _fox_sample_kernel,
        out_shape=jax.ShapeDtypeStruct((b, l, FOX_W), BF16),
        grid=(b, FOX_PAIRS),
        in_specs=[new, old, old, new, new,
                  pl.BlockSpec((1, fp.shape[1], LANES), lambda bi, j: (bi, 0, j))],
        out_specs=new,
        compiler_params=_params("arbitrary", "arbitrary"),
        name="fox_sample",
    )(q, cache_k, cache_v, kn, vn, fp)


def _layer_norm(x, w, b):
    mu = jnp.mean(x, axis=-1, keepdims=True)
    xc = x - mu
    var = jnp.mean(xc * xc, axis=-1, keepdims=True)
    return xc * lax.rsqrt(var + LN_EPS) * w + b


def _first_argmax_rows(x, n):
    rows = lax.broadcasted_iota(jnp.int32, x.shape, 0).astype(F32)
    mx = jnp.max(x, axis=0, keepdims=True)
    idx = jnp.min(jnp.where(x == mx, rows, float(n)), axis=0, keepdims=True)
    return mx, idx.astype(jnp.int32)


def _outproj_kernel(yr_ref, yf_ref, ga_ref, gb_ref, x_ref, mod_ref, wr_ref, wf_ref, wo_ref,
                    lw_ref, lb_ref, rw_ref, rb_ref,
                    x1_ref, h2_ref, eid_ref, gate_ref, gt_ref, *, alpha):
    gb_, tl, d = x_ref.shape
    tm = gb_ * tl
    flat = lambda ref: ref[...].reshape(tm, ref.shape[-1])
    y_ret = _dot(flat(yr_ref), wr_ref[...])
    y_fox = _dot(flat(yf_ref), wf_ref[...])
    mix = flat(ga_ref).astype(F32) * y_ret + flat(gb_ref).astype(F32) * y_fox
    mixed = _dot(mix.astype(BF16), wo_ref[...]).reshape(gb_, tl, d)
    g1 = mod_ref[:, :, 2 * d:3 * d]
    sh2 = mod_ref[:, :, 3 * d:4 * d]
    sc2 = mod_ref[:, :, 4 * d:5 * d]
    x1 = _layer_norm(alpha * x_ref[...] + (1.0 + g1) * mixed, lw_ref[...], lb_ref[...])
    x1_ref[...] = x1
    h2 = (x1 * (1.0 + sc2) + sh2)
    h2_ref[...] = h2.astype(h2_ref.dtype)
    h_hi, h_mid, h_lo = _split3(h2.reshape(tm, d))
    w_hi, w_mid, w_lo = rw_ref[0], rw_ref[1], rw_ref[2]
    lt = (_dot_nt(w_hi, h_hi) + _dot_nt(w_hi, h_mid) + _dot_nt(w_mid, h_hi)
          + _dot_nt(w_hi, h_lo) + _dot_nt(w_lo, h_hi) + _dot_nt(w_mid, h_mid)) + rb_ref[...]
    gl = lt[N_EXPERTS:N_EXPERTS + N_GROUPS, :]
    gmax, gi = _first_argmax_rows(gl, N_GROUPS)
    g_p = 1.0 / jnp.sum(jnp.exp(gl - gmax), axis=0, keepdims=True)
    e_sel = lt[0:EXPERTS_PER_GROUP, :]
    for g in range(1, N_GROUPS):
        e_sel = jnp.where(gi == g, lt[g * EXPERTS_PER_GROUP:(g + 1) * EXPERTS_PER_GROUP, :], e_sel)
    rows = lax.broadcasted_iota(jnp.int32, e_sel.shape, 0)
    m1, i1 = _first_argmax_rows(e_sel, EXPERTS_PER_GROUP)
    m2, i2 = _first_argmax_rows(jnp.where(rows == i1, -jnp.inf, e_sel), EXPERTS_PER_GROUP)
    r = jnp.exp(m2 - m1)
    gate0 = g_p / (1.0 + r)
    gate1 = g_p * r / (1.0 + r)
    eid_ref[0:1, :] = gi * EXPERTS_PER_GROUP + i1
    eid_ref[1:2, :] = gi * EXPERTS_PER_GROUP + i2
    gate_ref[0:1, :] = gate0
    gate_ref[1:2, :] = gate1
    rr = lax.broadcasted_iota(jnp.int32, (LANES, tm), 0)
    gsq = jnp.where(rr == 0, gate0, jnp.where(rr == 1, gate1, 0.0))
    gt_ref[...] = gsq.T


def _outproj(yr, yf, ga, gb, x, mod, wr, wf, wo, lw, lb, rw3, rb, gb_, tl, alpha):
    bx, l, d = x.shape
    m = bx * l
    tm = gb_ * tl
    nl = l // tl
    seq = lambda w: pl.BlockSpec((gb_, tl, w), lambda b, j: (b, j, 0))
    tok = lambda r: pl.BlockSpec((r, tm), lambda b, j: (0, b * nl + j))
    return pl.pallas_call(
        functools.partial(_outproj_kernel, alpha=alpha),
        out_shape=[jax.ShapeDtypeStruct((bx, l, d), F32), jax.ShapeDtypeStruct((bx, l, d), BF16),
                   jax.ShapeDtypeStruct((2, m), jnp.int32), jax.ShapeDtypeStruct((2, m), F32),
                   jax.ShapeDtypeStruct((m, LANES), F32)],
        grid=(bx // gb_, nl),
        in_specs=[seq(RET_V_W), seq(FOX_W), seq(d), seq(d), seq(d),
                  pl.BlockSpec((gb_, 1, mod.shape[-1]), lambda b, j: (b, 0, 0)),
                  _resident(wr.shape), _resident(wf.shape), _resident(wo.shape),
                  _resident(lw.shape), _resident(lb.shape), _resident(rw3.shape), _resident(rb.shape)],
        out_specs=[seq(d), seq(d), tok(2), tok(2),
                   pl.BlockSpec((tm, LANES), lambda b, j: (b * nl + j, 0))],
        compiler_params=_params("arbitrary", "arbitrary"),
        name="outproj",
    )(yr, yf, ga, gb, x, mod, wr, wf, wo, lw, lb, rw3, rb)


MOE_CHUNK = 8


def _ceil_chunk(x):
    return jnp.floor((x + (MOE_CHUNK - 1.0)) * (1.0 / MOE_CHUNK)) * MOE_CHUNK


def _plan_kernel(eid_ref, triu_ref, ld_ref, pc_ref, pre_ref, carry_ref):
    @pl.when(pl.program_id(0) == 0)
    def _():
        carry_ref[...] = jnp.zeros_like(carry_ref)

    t = eid_ref.shape[1]
    experts = lax.broadcasted_iota(jnp.int32, (N_EXPERTS, t), 0)
    hit = [eid_ref[kk:kk + 1, :] == experts for kk in range(2)]
    onehot = [jnp.where(h, 1.0, 0.0) for h in hit]
    onehot_b = [o.astype(BF16) for o in onehot]
    ones = jnp.ones((8, t), BF16)
    cnt_row = (_dot_nt(ones, onehot_b[0]) + _dot_nt(ones, onehot_b[1]))[0:1, :]
    cnt0_col = jnp.sum(onehot[0], axis=1, keepdims=True)
    pc_row = _ceil_chunk(cnt_row)
    e_lane = lax.broadcasted_iota(jnp.int32, (N_EXPERTS, N_EXPERTS), 1)
    e_sub = lax.broadcasted_iota(jnp.int32, (N_EXPERTS, N_EXPERTS), 0)
    lstart_col = jnp.sum(jnp.where(e_lane < e_sub, pc_row, 0.0), axis=1, keepdims=True)
    for kk in range(2):
        before = _dot(onehot_b[kk], triu_ref[...])
        base = lstart_col + cnt0_col if kk == 1 else lstart_col
        row = jnp.sum(jnp.where(hit[kk], before + base, 0.0), axis=0, keepdims=True)
        ld_ref[kk:kk + 1, :] = row.astype(jnp.int32)
    pc_ref[0] = pc_row.astype(jnp.int32)
    pre_ref[0] = carry_ref[...].astype(jnp.int32)
    carry_ref[...] = carry_ref[...] + pc_row


def _plan(eid, t):
    m = eid.shape[1]
    n_tiles = m // t
    triu = jnp.asarray(np.triu(np.ones((t, t), np.float32), 1), BF16)
    per_tile = jax.ShapeDtypeStruct((n_tiles, 1, N_EXPERTS), jnp.int32)
    per_tile_spec = pl.BlockSpec((1, 1, N_EXPERTS), lambda i: (i, 0, 0))
    return pl.pallas_call(
        _plan_kernel,
        out_shape=[jax.ShapeDtypeStruct((2, m), jnp.int32), per_tile, per_tile],
        grid=(n_tiles,),
        in_specs=[pl.BlockSpec((2, t), lambda i: (0, i)), pl.BlockSpec((t, t), lambda i: (0, 0))],
        out_specs=[pl.BlockSpec((2, t), lambda i: (0, i)), per_tile_spec, per_tile_spec],
        scratch_shapes=[pltpu.VMEM((1, N_EXPERTS), F32)],
        compiler_params=_params("arbitrary"),
        name="moe_plan",
    )(eid, triu)


def _for_each_chunk(tile, pc_ref, goff_ref, fn):
    def per_expert(e, local):
        n = lax.div(pc_ref[tile * N_EXPERTS + e], jnp.int32(MOE_CHUNK))
        dst = goff_ref[tile * N_EXPERTS + e]

        def per_chunk(j, c):
            fn(pl.multiple_of(local + j * MOE_CHUNK, MOE_CHUNK), pl.multiple_of(dst + j * MOE_CHUNK, MOE_CHUNK))
            return c

        lax.fori_loop(0, n, per_chunk, 0)
        return local + n * MOE_CHUNK

    lax.fori_loop(0, N_EXPERTS, per_expert, 0)


def _one_hot_rows(ld_row, lp):
    rows = lax.broadcasted_iota(jnp.int32, (lp, ld_row.shape[1]), 0)
    return rows == ld_row


def _dispatch_kernel(pc_ref, goff_ref, ld_ref, h_ref, xs_in_ref, xs_ref, buf_ref, sem):
    del xs_in_ref
    tile = pl.program_id(0)
    lp = buf_ref.shape[0]
    perm = jnp.where(_one_hot_rows(ld_ref[0:1, :], lp), 1.0,
                     jnp.where(_one_hot_rows(ld_ref[1:2, :], lp), 1.0, 0.0)).astype(BF16)
    buf_ref[...] = _dot(perm, h_ref[...])

    def copy(local, dst):
        return pltpu.make_async_copy(buf_ref.at[pl.ds(local, MOE_CHUNK)], xs_ref.at[pl.ds(dst, MOE_CHUNK)], sem)

    _for_each_chunk(tile, pc_ref, goff_ref, lambda a, b: copy(a, b).start())
    _for_each_chunk(tile, pc_ref, goff_ref, lambda a, b: copy(a, b).wait())


def _dispatch(pc, goff, ld, h2, n_slots, t, lp):
    m, d = h2.shape
    xs0 = jnp.zeros((n_slots, d), F32)
    return pl.pallas_call(
        _dispatch_kernel,
        out_shape=jax.ShapeDtypeStruct((n_slots, d), F32),
        grid_spec=pltpu.PrefetchScalarGridSpec(
            num_scalar_prefetch=2,
            grid=(m // t,),
            in_specs=[pl.BlockSpec((2, t), lambda i, pc, go: (0, i)),
                      pl.BlockSpec((t, d), lambda i, pc, go: (i, 0)),
                      pl.BlockSpec(memory_space=pl.ANY)],
            out_specs=pl.BlockSpec(memory_space=pl.ANY),
            scratch_shapes=[pltpu.VMEM((lp, d), F32), pltpu.SemaphoreType.DMA]),
        input_output_aliases={4: 0},
        compiler_params=_params("arbitrary"),
        name="moe_dispatch",
    )(pc, goff, ld, h2, xs0)


def _expert_kernel(be_ref, nv_ref, x_ref, w1_ref, w2_ref, y_ref):
    del be_ref

    @pl.when(pl.program_id(0) < nv_ref[0])
    def _():
        e = w2_ref.shape[1]
        au = _dot(x_ref[...].astype(BF16), w1_ref[0])
        a = au[:, :e]
        u = au[:, e:]
        y_ref[...] = _dot((a * _sigmoid(a) * u).astype(BF16), w2_ref[0])

    @pl.when(pl.program_id(0) >= nv_ref[0])
    def _():
        y_ref[...] = jnp.zeros_like(y_ref)


def _experts(block_e, n_valid, xs, w1, w2, tb):
    p, d = xs.shape
    e = w2.shape[1]
    rows = lambda i, be, nv: (jnp.minimum(i, nv[0] - 1), 0)
    return pl.pallas_call(
        _expert_kernel,
        out_shape=jax.ShapeDtypeStruct((p, d), F32),
        grid_spec=pltpu.PrefetchScalarGridSpec(
            num_scalar_prefetch=2,
            grid=(p // tb,),
            in_specs=[pl.BlockSpec((tb, d), rows),
                      pl.BlockSpec((1, d, 2 * e), lambda i, be, nv: (be[i], 0, 0)),
                      pl.BlockSpec((1, e, d), lambda i, be, nv: (be[i], 0, 0))],
            out_specs=pl.BlockSpec((tb, d), lambda i, be, nv: (i, 0))),
        compiler_params=_params("arbitrary"),
        name="moe_experts",
    )(block_e, n_valid, xs, w1, w2)


def _combine_kernel(pc_ref, goff_ref, ld_ref, y_hbm, gt_ref, x1_ref, mod_ref, lw_ref, lb_ref, o_ref,
                    buf_ref, sem, *, alpha):
    gb_, tl, d = x1_ref.shape
    nl = pl.num_programs(1)
    tile = pl.program_id(0) * nl + pl.program_id(1)
    lp = buf_ref.shape[0]

    @pl.when(tile == 0)
    def _():
        buf_ref[...] = jnp.zeros_like(buf_ref)

    def copy(local, src):
        return pltpu.make_async_copy(y_hbm.at[pl.ds(src, MOE_CHUNK)], buf_ref.at[pl.ds(local, MOE_CHUNK)], sem)

    _for_each_chunk(tile, pc_ref, goff_ref, lambda a, b: copy(a, b).start())
    _for_each_chunk(tile, pc_ref, goff_ref, lambda a, b: copy(a, b).wait())
    yb = buf_ref[...].astype(BF16)
    picked = [_dot_tn(jnp.where(_one_hot_rows(ld_ref[kk:kk + 1, :], lp), 1.0, 0.0).astype(BF16), yb)
              for kk in range(2)]
    y = gt_ref[:, 0:1] * picked[0] + gt_ref[:, 1:2] * picked[1]
    g2 = mod_ref[:, :, 5 * d:6 * d]
    o_ref[...] = _layer_norm(alpha * x1_ref[...] + (1.0 + g2) * y.reshape(gb_, tl, d), lw_ref[...], lb_ref[...])


def _combine(pc, goff, ld, y, gt, x1, mod, lw, lb, gb_, tl, alpha, lp):
    bx, l, d = x1.shape
    tm = gb_ * tl
    nl = l // tl
    seq = pl.BlockSpec((gb_, tl, d), lambda b, j, pc, go: (b, j, 0))
    fixed = lambda shape: pl.BlockSpec(shape, lambda b, j, pc, go: (0,) * len(shape), pipeline_mode=pl.Buffered(1))
    return pl.pallas_call(
        functools.partial(_combine_kernel, alpha=alpha),
        out_shape=jax.ShapeDtypeStruct((bx, l, d), F32),
        grid_spec=pltpu.PrefetchScalarGridSpec(
            num_scalar_prefetch=2,
            grid=(bx // gb_, nl),
            in_specs=[pl.BlockSpec((2, tm), lambda b, j, pc, go: (0, b * nl + j)),
                      pl.BlockSpec(memory_space=pl.ANY),
                      pl.BlockSpec((tm, LANES), lambda b, j, pc, go: (b * nl + j, 0)),
                      seq,
                      pl.BlockSpec((gb_, 1, mod.shape[-1]), lambda b, j, pc, go: (b, 0, 0)),
                      fixed(lw.shape), fixed(lb.shape)],
            out_specs=seq,
            scratch_shapes=[pltpu.VMEM((lp, d), F32), pltpu.SemaphoreType.DMA]),
        compiler_params=_params("arbitrary", "arbitrary"),
        name="moe_combine",
    )(pc, goff, ld, y, gt, x1, mod, lw, lb)


def _moe(h2, eid, gt, x1, mod, w1, w2, lw, lb, gb_, tl, alpha, tb):
    bx, l, d = x1.shape
    m = bx * l
    t = gb_ * tl
    n_tiles = m // t
    lp = 2 * t + N_EXPERTS * MOE_CHUNK
    ld, pc, pre = _plan(eid, t)
    pc = pc.reshape(n_tiles, N_EXPERTS)
    pre = pre.reshape(n_tiles, N_EXPERTS)
    total = pre[-1] + pc[-1]
    region = (total + tb - 1) // tb * tb
    gend = jnp.cumsum(region)
    goff = (gend - region)[None, :] + pre
    n_blocks = -(-(2 * m + n_tiles * N_EXPERTS * (MOE_CHUNK - 1) + N_EXPERTS * (tb - 1)) // tb)
    block_row0 = jnp.arange(n_blocks, dtype=jnp.int32) * tb
    block_e = jnp.minimum(jnp.sum((gend[None, :] <= block_row0[:, None]).astype(jnp.int32), axis=1),
                          N_EXPERTS - 1).astype(jnp.int32)
    n_valid = (gend[-1:] // tb).astype(jnp.int32)
    pc = pc.reshape(-1).astype(jnp.int32)
    goff = goff.reshape(-1).astype(jnp.int32)
    xs = _dispatch(pc, goff, ld, h2.reshape(m, d), n_blocks * tb, t, lp)
    y = _experts(block_e, n_valid, xs, w1, w2, tb)
    return _combine(pc, goff, ld, y, gt, x1, mod, lw, lb, gb_, tl, alpha, lp)


def _rotary_tables(pos, reps):
    half = RET_QK_DIM // 2
    inv = ROPE_BASE ** (-jnp.linspace(0.0, 1.0, half, dtype=F32))
    ang = pos.astype(F32)[:, None] * inv[None, :]
    cos = jnp.cos(ang)
    sin = jnp.sin(ang)
    c2 = jnp.concatenate([cos, cos], axis=1)
    s2 = jnp.concatenate([-sin, sin], axis=1)
    kscale = RET_QK_DIM ** -0.5
    tabs = (c2, s2, c2 * kscale, s2 * kscale)
    return tuple(jnp.tile(t, (reps, 1)) for t in tabs)


def _pick(n, pref):
    t = min(n, pref)
    while n % t:
        t //= 2
    return t


def _layer(x, mod, pos, weights, log_gamma, alpha, cache=None):
    (w2, b2, gn_w, gn_b, wr, wf, wo, ln1w, ln1b, rw3, rb, we1, we2, ln2w, ln2b) = weights
    bx, l, d = x.shape
    if cache is None:
        gb_, tl = 1, _pick(l, 512)
    else:
        gb_, tl = _pick(bx, max(1, 512 // l)), l
    tabs = _rotary_tables(pos, gb_ if cache is not None else 1)
    rq, rk, rv, rg, fq, fk, fv, fkb, fvb, lf, ga, gb = _inproj(x, mod, w2, b2, tabs, gb_, tl)

    if cache is None:
        lc = _pick(l, 256)
        y_ret, state = _retention(rq, rk, rv, rg, gn_w, gn_b, log_gamma, lc)
        fp = _fcum(lf, _pick(l, 512))
        y_fox = _fox_prompt(fq, fkb, fp, fvb, _pick(l, 512))
    else:
        state0, cache_k, cache_v, cache_logf = cache
        y_ret, state = _retention(rq, rk, rv, rg, gn_w, gn_b, log_gamma, l, state0)
        past = cache_k.shape[1]
        total = -(-(past + l) // LANES) * LANES
        tf = total
        lf_all = jnp.concatenate(
            [jnp.pad(cache_logf.astype(F32), ((0, 0), (0, 0), (0, LANES - FOX_HEADS))), lf,
             jnp.zeros((bx, total - past - l, LANES), F32)], axis=1)
        fp = _fcum(lf_all, tf)
        y_fox = _fox_sample(fq, fkb, fvb, cache_k.reshape(bx, past, FOX_W), cache_v.reshape(bx, past, FOX_W), fp)

    x1, h2, eid, gate, gt = _outproj(y_ret, y_fox, ga, gb, x, mod, wr, wf, wo, ln1w, ln1b, rw3, rb,
                                     gb_, tl, alpha)
    del gate
    m = bx * l
    out = _moe(h2, eid, gt, x1, mod, we1, we2, ln2w, ln2b, gb_, tl, alpha, tb=256)
    return out, state, fk, fv, lf[:, :, :FOX_HEADS]


def kernel(x_prompt, x_sample, state_ret, cache_fox_k, cache_fox_v, cache_fox_logf, c_prompt, c_sample,
           w_ada, b_ada, w_in, b_in, ret_gn_w, ret_gn_b, w_ret_proj, w_fox_proj, w_o, ln1_w, ln1_b,
           w_rg, b_rg, w_re, b_re, w_e_in, w_e_out, ln2_w, ln2_b):
    depth = w_ada.shape[0]
    d = x_prompt.shape[-1]
    bp, s, _ = x_prompt.shape
    bs, ls, _ = x_sample.shape
    past = cache_fox_k.shape[2]
    alpha = (2 * depth) ** 0.25
    log_gamma = jnp.log1p(-jnp.exp(jnp.linspace(math.log(1.0 / 32), math.log(1.0 / 512), RET_HEADS, dtype=F32)))
    pos_p = jnp.arange(s, dtype=jnp.int32)
    pos_s = past + jnp.arange(ls, dtype=jnp.int32)
    fg = 2 * RET_QK_W + 2 * RET_V_W + 3 * FOX_W

    xp, xs = x_prompt, x_sample
    p_ret, p_k, p_v, p_f, s_ret, s_k, s_v, s_f = [], [], [], [], [], [], [], []
    for li in range(depth):
        w2 = jnp.concatenate([w_in[li][:, :fg], w_in[li][:, fg + FOX_HEADS:],
                              jnp.pad(w_in[li][:, fg:fg + FOX_HEADS], ((0, 0), (0, LANES - FOX_HEADS)))],
                             axis=1).astype(BF16)
        b2 = jnp.concatenate([b_in[li][:fg], b_in[li][fg + FOX_HEADS:],
                              jnp.pad(b_in[li][fg:fg + FOX_HEADS], (0, LANES - FOX_HEADS))]).reshape(1, -1)
        n_rt = N_EXPERTS + N_GROUPS
        rt_rows = -(-n_rt // 8) * 8
        rwt = jnp.pad(jnp.concatenate([w_re[li], w_rg[li]], axis=1).T.astype(F32), ((0, rt_rows - n_rt), (0, 0)))
        r_hi = rwt.astype(BF16)
        r_mid = (rwt - r_hi.astype(F32)).astype(BF16)
        r_lo = (rwt - r_hi.astype(F32) - r_mid.astype(F32)).astype(BF16)
        rw3 = jnp.stack([r_hi, r_mid, r_lo])
        rb = jnp.pad(jnp.concatenate([b_re[li], b_rg[li]]).astype(F32), (0, rt_rows - n_rt)).reshape(rt_rows, 1)
        weights = (w2, b2, ret_gn_w[li], ret_gn_b[li],
                   w_ret_proj[li].astype(BF16), w_fox_proj[li].astype(BF16), w_o[li].astype(BF16),
                   ln1_w[li].reshape(1, d), ln1_b[li].reshape(1, d), rw3, rb,
                   w_e_in[li].astype(BF16), w_e_out[li].astype(BF16),
                   ln2_w[li].reshape(1, d), ln2_b[li].reshape(1, d))
        mod = _ada(jnp.concatenate([c_prompt, c_sample], axis=0), w_ada[li], b_ada[li])
        mod_p = mod[:bp].reshape(bp, 1, 6 * d)
        mod_s = mod[bp:].reshape(bs, 1, 6 * d)
        xp, st, kk, vv, ff = _layer(xp, mod_p, pos_p, weights, log_gamma, alpha)
        p_ret.append(st); p_k.append(kk.reshape(bp, s, FOX_HEADS, FOX_HEAD_DIM))
        p_v.append(vv.reshape(bp, s, FOX_HEADS, FOX_HEAD_DIM)); p_f.append(ff)
        cache = (state_ret[li].astype(F32), cache_fox_k[li], cache_fox_v[li], cache_fox_logf[li])
        xs, st, kk, vv, ff = _layer(xs, mod_s, pos_s, weights, log_gamma, alpha, cache)
        s_ret.append(st); s_k.append(kk.reshape(bs, ls, FOX_HEADS, FOX_HEAD_DIM))
        s_v.append(vv.reshape(bs, ls, FOX_HEADS, FOX_HEAD_DIM)); s_f.append(ff)
    return (xp, xs, jnp.stack(p_ret), jnp.stack(p_k), jnp.stack(p_v), jnp.stack(p_f),
            jnp.stack(s_ret), jnp.stack(s_k), jnp.stack(s_v), jnp.stack(s_f))
```

```python
import functools
import math

import jax
import jax.numpy as jnp
import numpy as np
from jax import lax
from jax.experimental import pallas as pl
from jax.experimental.pallas import tpu as pltpu

F32 = jnp.float32
BF16 = jnp.bfloat16

RET_HEADS = 4
RET_QK_DIM = 128
RET_V_DIM = 256
ROPE_BASE = 10000.0
FOX_HEADS = 16
FOX_HEAD_DIM = 64
N_GROUPS = 4
EXPERTS_PER_GROUP = 8
N_EXPERTS = N_GROUPS * EXPERTS_PER_GROUP
LN_EPS = 1e-5
GN_EPS = 1e-6
RET_QK_W = RET_HEADS * RET_QK_DIM
RET_V_W = RET_HEADS * RET_V_DIM
FOX_W = FOX_HEADS * FOX_HEAD_DIM
FOX_PAIRS = FOX_HEADS // 2
LOG2_E = math.log2(math.e)
FOX_Q_SCALE = FOX_HEAD_DIM ** -0.5 * LOG2_E

LANES = 128
VMEM_LIMIT_BYTES = 56 * 1024 * 1024
MASK_VALUE = -1e30


def _dot(a, b):
    return jnp.dot(a, b, preferred_element_type=F32)


def _dot_nt(a, b):
    return lax.dot_general(a, b, (((1,), (1,)), ((), ())), preferred_element_type=F32)


def _dot_tn(a, b):
    return lax.dot_general(a, b, (((0,), (0,)), ((), ())), preferred_element_type=F32)


def _split3(x):
    hi = x.astype(BF16)
    r1 = x - hi.astype(F32)
    mid = r1.astype(BF16)
    lo = (r1 - mid.astype(F32)).astype(BF16)
    return hi, mid, lo


def _sigmoid(x):
    return 1.0 / (1.0 + jnp.exp(-x))


def _log_sigmoid(x):
    return -(jnp.maximum(-x, 0.0) + jnp.log1p(jnp.exp(-jnp.abs(x))))


def _params(*sem):
    return pltpu.CompilerParams(dimension_semantics=sem, vmem_limit_bytes=VMEM_LIMIT_BYTES)


def _resident(shape):
    nd = len(shape)
    return pl.BlockSpec(shape, lambda *_: (0,) * nd, pipeline_mode=pl.Buffered(1))


def _ada_kernel(c_ref, w_ref, b_ref, o_ref):
    c = c_ref[...]
    s = (c * _sigmoid(c)).astype(BF16)
    o_ref[...] = _dot(s, w_ref[...].astype(BF16)) + b_ref[...]


def _ada(c, w, b):
    n, d = c.shape
    nout = w.shape[1]
    tn = d
    return pl.pallas_call(
        _ada_kernel,
        out_shape=jax.ShapeDtypeStruct((n, nout), F32),
        grid=(nout // tn,),
        in_specs=[pl.BlockSpec((n, d), lambda j: (0, 0)),
                  pl.BlockSpec((d, tn), lambda j: (0, j)),
                  pl.BlockSpec((1, tn), lambda j: (0, j))],
        out_specs=pl.BlockSpec((n, tn), lambda j: (0, j)),
        compiler_params=_params("arbitrary"),
        name="ada",
    )(c, w, b.reshape(1, nout))


def _inproj_kernel(x_ref, mod_ref, w_ref, b_ref, cq_ref, sq_ref, ck_ref, sk_ref,
                   rq_ref, rk_ref, rv_ref, rg_ref, fq_ref, fk_ref, fv_ref, fkb_ref, fvb_ref,
                   lf_ref, ga_ref, gb_ref, *, transposed_v):
    gb_, tl, d = x_ref.shape
    tm = gb_ * tl
    sh = mod_ref[:, :, 0:d]
    sc = mod_ref[:, :, d:2 * d]
    h = (x_ref[...] * (1.0 + sc) + sh).reshape(tm, d).astype(BF16)

    def proj(lo, width):
        return _dot(h, w_ref[:, lo:lo + width]) + b_ref[:, lo:lo + width]

    def put(ref, val):
        ref[...] = val.reshape(ref.shape).astype(ref.dtype)

    def rot(z, c_ref, s_ref):
        c = c_ref[...]
        s = s_ref[...]
        parts = []
        for hh in range(RET_HEADS):
            zh = z[:, hh * RET_QK_DIM:(hh + 1) * RET_QK_DIM]
            parts.append(zh * c + pltpu.roll(zh, RET_QK_DIM // 2, axis=1) * s)
        return jnp.concatenate(parts, axis=1)

    off = 0
    put(rq_ref, rot(proj(off, RET_QK_W), cq_ref, sq_ref)); off += RET_QK_W
    put(rk_ref, rot(proj(off, RET_QK_W), ck_ref, sk_ref)); off += RET_QK_W
    put(rv_ref, proj(off, RET_V_W)); off += RET_V_W
    z = proj(off, RET_V_W); off += RET_V_W
    put(rg_ref, z * _sigmoid(z))
    put(fq_ref, proj(off, FOX_W) * FOX_Q_SCALE); off += FOX_W
    z = proj(off, FOX_W); off += FOX_W
    put(fk_ref, z); put(fkb_ref, z)
    z = proj(off, FOX_W); off += FOX_W
    put(fv_ref, z)
    if transposed_v:
        fvb_ref[0] = z.T.astype(fvb_ref.dtype)
    else:
        put(fvb_ref, z)
    put(ga_ref, _sigmoid(proj(off, d))); off += d
    put(gb_ref, _sigmoid(proj(off, d))); off += d
    put(lf_ref, _log_sigmoid(proj(off, LANES)))


def _inproj(x, mod, w2, b2, tabs, gb_, tl):
    bx, l, d = x.shape
    tm = gb_ * tl
    nw = w2.shape[1]
    grid = (bx // gb_, l // tl)
    if gb_ == 1:
        tab_spec = pl.BlockSpec((tl, LANES), lambda b, j: (j, 0))
    else:
        tab_spec = pl.BlockSpec((tm, LANES), lambda b, j: (0, 0))

    def out(width, dtype):
        return (jax.ShapeDtypeStruct((bx, l, width), dtype),
                pl.BlockSpec((gb_, tl, width), lambda b, j: (b, j, 0)))

    transposed_v = gb_ == 1
    if transposed_v:
        fvb = (jax.ShapeDtypeStruct((bx, FOX_W, l), BF16), pl.BlockSpec((1, FOX_W, tl), lambda b, j: (b, 0, j)))
    else:
        fvb = out(FOX_W, BF16)
    outs = [out(RET_QK_W, BF16), out(RET_QK_W, BF16), out(RET_V_W, BF16), out(RET_V_W, BF16),
            out(FOX_W, BF16), out(FOX_W, F32), out(FOX_W, F32), out(FOX_W, BF16), fvb,
            out(LANES, F32), out(d, BF16), out(d, BF16)]
    return pl.pallas_call(
        functools.partial(_inproj_kernel, transposed_v=transposed_v),
        out_shape=[o[0] for o in outs],
        grid=grid,
        in_specs=[pl.BlockSpec((gb_, tl, d), lambda b, j: (b, j, 0)),
                  pl.BlockSpec((gb_, 1, mod.shape[-1]), lambda b, j: (b, 0, 0)),
                  _resident((d, nw)), _resident((1, nw)),
                  tab_spec, tab_spec, tab_spec, tab_spec],
        out_specs=[o[1] for o in outs],
        compiler_params=_params("arbitrary", "arbitrary"),
        name="inproj",
    )(x, mod, w2, b2, *tabs)


BIAS_PIECES = 3


def _bias_selector():
    sel = np.zeros((BIAS_PIECES * LANES, FOX_PAIRS * LANES), np.float32)
    for h in range(FOX_HEADS):
        for p in range(BIAS_PIECES):
            sel[p * LANES + h, LANES * (h // 2) + BIAS_PIECES * (h % 2) + p] = 1.0
    return sel


def _fcum_kernel(lf_ref, tril_ref, sel_ref, fp_ref, carry_ref):
    @pl.when(pl.program_id(1) == 0)
    def _():
        carry_ref[...] = jnp.zeros_like(carry_ref)

    tl = lf_ref.shape[1]
    hi, mid, lo = _split3(lf_ref[0])
    tril = tril_ref[...]
    cum = _dot(tril, hi) + _dot(tril, mid) + _dot(tril, lo) + carry_ref[...]
    carry_ref[...] = cum[tl - 1:tl, :]
    pieces = jnp.concatenate(_split3(cum * -LOG2_E), axis=1)
    fp_ref[0] = _dot(pieces, sel_ref[...]).astype(fp_ref.dtype)


def _fcum(lf, tl):
    bx, l, _ = lf.shape
    tril = jnp.asarray(np.tril(np.ones((tl, tl), np.float32)), BF16)
    sel = jnp.asarray(_bias_selector(), BF16)
    return pl.pallas_call(
        _fcum_kernel,
        out_shape=jax.ShapeDtypeStruct((bx, l, FOX_PAIRS * LANES), BF16),
        grid=(bx, l // tl),
        in_specs=[pl.BlockSpec((1, tl, LANES), lambda b, j: (b, j, 0)),
                  pl.BlockSpec((tl, tl), lambda b, j: (0, 0)),
                  pl.BlockSpec(sel.shape, lambda b, j: (0, 0))],
        out_specs=pl.BlockSpec((1, tl, FOX_PAIRS * LANES), lambda b, j: (b, j, 0)),
        scratch_shapes=[pltpu.VMEM((1, LANES), F32)],
        compiler_params=_params("arbitrary", "arbitrary"),
        name="fcum",
    )(lf, tril, sel)


def _retention_kernel(*refs, has_state):
    if has_state:
        (q_ref, k_ref, v_ref, g_ref, dec_ref, qd_ref, kd_ref, gw_ref, gb_ref, s0_ref,
         y_ref, st_ref) = refs
    else:
        (q_ref, k_ref, v_ref, g_ref, dec_ref, qd_ref, kd_ref, gw_ref, gb_ref,
         y_ref, st_ref) = refs
        s0_ref = None
    lc = q_ref.shape[1]

    @pl.when(pl.program_id(1) == 0)
    def _():
        if has_state:
            st_ref[...] = s0_ref[...]
        else:
            st_ref[...] = jnp.zeros_like(st_ref)

    for hh in range(RET_HEADS):
        qs = slice(hh * RET_QK_DIM, (hh + 1) * RET_QK_DIM)
        vs = slice(hh * RET_V_DIM, (hh + 1) * RET_V_DIM)
        q = q_ref[0, :, qs]
        k = k_ref[0, :, qs]
        v = v_ref[0, :, vs]
        state = st_ref[0, hh]
        qd = qd_ref[hh]
        scores = _dot_nt(q, k) * dec_ref[hh]
        inner = _dot(scores.astype(BF16), v)
        cross = _dot(q, state.astype(BF16)) * qd
        o = inner + cross
        kdec = (k.astype(F32) * kd_ref[hh]).astype(BF16)
        st_ref[0, hh] = qd[lc - 1:lc, :] * state + _dot_tn(kdec, v)
        mu = jnp.mean(o, axis=-1, keepdims=True)
        oc = o - mu
        var = jnp.mean(oc * oc, axis=-1, keepdims=True)
        on = oc * lax.rsqrt(var + GN_EPS) * gw_ref[:, vs] + gb_ref[:, vs]
        y_ref[0, :, vs] = (g_ref[0, :, vs].astype(F32) * on).astype(y_ref.dtype)


def _retention_tables(log_gamma, lc):
    n = jnp.arange(lc, dtype=F32)
    diff = n[:, None] - n[None, :]
    decay = jnp.where(diff[None] >= 0, jnp.exp(jnp.maximum(diff, 0.0)[None] * log_gamma[:, None, None]), 0.0)
    qdec = jnp.exp((n + 1.0)[None, :, None] * log_gamma[:, None, None])
    kdec = jnp.exp((lc - 1.0 - n)[None, :, None] * log_gamma[:, None, None])
    return decay, qdec, kdec


def _retention(q, k, v, g, gn_w, gn_b, log_gamma, lc, state0=None):
    bx, l, _ = q.shape
    decay, qdec, kdec = _retention_tables(log_gamma, lc)
    has_state = state0 is not None
    seq = lambda w: pl.BlockSpec((1, lc, w), lambda b, c: (b, c, 0))
    whole = lambda a: pl.BlockSpec(a.shape, lambda b, c: (0,) * a.ndim)
    st_spec = pl.BlockSpec((1, RET_HEADS, RET_QK_DIM, RET_V_DIM), lambda b, c: (b, 0, 0, 0))
    gw = gn_w.reshape(1, RET_V_W)
    gb = gn_b.reshape(1, RET_V_W)
    args = [q, k, v, g, decay, qdec, kdec, gw, gb]
    in_specs = [seq(RET_QK_W), seq(RET_QK_W), seq(RET_V_W), seq(RET_V_W),
                whole(decay), whole(qdec), whole(kdec), whole(gw), whole(gb)]
    if has_state:
        args.append(state0)
        in_specs.append(st_spec)
    return pl.pallas_call(
        functools.partial(_retention_kernel, has_state=has_state),
        out_shape=[jax.ShapeDtypeStruct((bx, l, RET_V_W), BF16),
                   jax.ShapeDtypeStruct((bx, RET_HEADS, RET_QK_DIM, RET_V_DIM), F32)],
        grid=(bx, l // lc),
        in_specs=in_specs,
        out_specs=[seq(RET_V_W), st_spec],
        compiler_params=_params("arbitrary", "arbitrary"),
        name="retention",
    )(*args)


def _pair_queries(q2):
    t = q2.shape[0]
    lane = lax.broadcasted_iota(jnp.int32, (t, LANES), 1)
    out = []
    for i in range(2):
        head = (lane >= i * FOX_HEAD_DIM) & (lane < (i + 1) * FOX_HEAD_DIM)
        ones = (lane >= i * BIAS_PIECES) & (lane < (i + 1) * BIAS_PIECES)
        out.append(jnp.concatenate([jnp.where(head, q2, jnp.zeros_like(q2)),
                                    jnp.where(ones, 1.0, 0.0).astype(q2.dtype)], axis=1))
    return out


def _fox_prompt_kernel(q_ref, k_ref, fp_ref, vt_ref, o_ref, m_ref, l_ref, acc_ref, qc_ref, sa_ref, sb_ref, *, t):
    nq = q_ref.shape[1] // t
    steps = [(qi, ki) for qi in range(nq) for ki in range(qi + 1)]
    bufs = (sa_ref, sb_ref)

    def produce(n, i):
        qi, ki = steps[n]
        if ki == 0 and i == 0:
            qc = _pair_queries(q_ref[0, qi * t:(qi + 1) * t, :])
            qc_ref[0] = qc[0]
            qc_ref[1] = qc[1]
        kc = jnp.concatenate([k_ref[0, ki * t:(ki + 1) * t, :], fp_ref[0, ki * t:(ki + 1) * t, :]], axis=1)
        bufs[n % 2][i] = _dot_nt(kc, qc_ref[i])

    def consume(n, i):
        qi, ki = steps[n]
        rows = slice(i * FOX_HEAD_DIM, (i + 1) * FOX_HEAD_DIM)
        s = bufs[n % 2][i]
        if ki == qi:
            key = lax.broadcasted_iota(jnp.int32, (t, t), 0)
            qry = lax.broadcasted_iota(jnp.int32, (t, t), 1)
            s = jnp.where(qry >= key, s, MASK_VALUE)
        smax = jnp.max(s, axis=0, keepdims=True)
        m_new = smax if ki == 0 else jnp.maximum(m_ref[i], smax)
        p = jnp.exp2(s - m_new)
        psum = jnp.sum(p, axis=0, keepdims=True)
        pv = _dot(vt_ref[0, rows, ki * t:(ki + 1) * t], p.astype(BF16))
        if ki == 0:
            l_ref[i] = psum
            acc_ref[rows, :] = pv
        else:
            alpha = jnp.exp2(m_ref[i] - m_new)
            l_ref[i] = alpha * l_ref[i] + psum
            acc_ref[rows, :] = acc_ref[rows, :] * alpha + pv
        m_ref[i] = m_new
        if ki == qi and i == 1:
            out_t = jnp.concatenate(
                [acc_ref[h * FOX_HEAD_DIM:(h + 1) * FOX_HEAD_DIM, :] * (1.0 / l_ref[h]) for h in range(2)], axis=0)
            o_ref[0, qi * t:(qi + 1) * t, :] = out_t.T.astype(o_ref.dtype)

    for i in range(2):
        produce(0, i)
    for n in range(len(steps)):
        for i in range(2):
            if n + 1 < len(steps):
                produce(n + 1, i)
            consume(n, i)


def _fox_prompt(q, k, fp, vt, t):
    b, s, _ = q.shape
    seq = pl.BlockSpec((1, s, LANES), lambda bi, j: (bi, 0, j))
    return pl.pallas_call(
        functools.partial(_fox_prompt_kernel, t=t),
        out_shape=jax.ShapeDtypeStruct((b, s, FOX_W), BF16),
        grid=(b, FOX_PAIRS),
        in_specs=[seq, seq, seq, pl.BlockSpec((1, LANES, s), lambda bi, j: (bi, j, 0))],
        out_specs=seq,
        scratch_shapes=[pltpu.VMEM((2, 1, t), F32), pltpu.VMEM((2, 1, t), F32),
                        pltpu.VMEM((LANES, t), F32), pltpu.VMEM((2, t, 2 * LANES), BF16),
                        pltpu.VMEM((2, t, t), F32), pltpu.VMEM((2, t, t), F32)],
        compiler_params=_params("arbitrary", "arbitrary"),
        name="fox_prompt",
    )(q, k, fp, vt)


def _fox_sample_kernel(q_ref, kc_ref, vc_ref, kn_ref, vn_ref, fp_ref, o_ref):
    l = q_ref.shape[1]
    past = kc_ref.shape[1]
    lane = lax.broadcasted_iota(jnp.int32, (1, LANES), 1)
    head_mask = (lane < FOX_HEAD_DIM, lane >= FOX_HEAD_DIM)
    qc = _pair_queries(q_ref[0])
    kc = jnp.concatenate([kc_ref[0].astype(BF16), fp_ref[0, 0:past, :]], axis=1)
    kn = jnp.concatenate([kn_ref[0], fp_ref[0, past:past + l, :]], axis=1)
    vc = vc_ref[0].astype(BF16)
    vn = vn_ref[0]
    row = lax.broadcasted_iota(jnp.int32, (l, l), 0)
    col = lax.broadcasted_iota(jnp.int32, (l, l), 1)
    out = jnp.zeros((l, LANES), F32)
    for i in range(2):
        s_c = _dot_nt(qc[i], kc)
        s_n = jnp.where(row >= col, _dot_nt(qc[i], kn), MASK_VALUE)
        m = jnp.maximum(jnp.max(s_c, axis=-1, keepdims=True), jnp.max(s_n, axis=-1, keepdims=True))
        p_c = jnp.exp2(s_c - m)
        p_n = jnp.exp2(s_n - m)
        denom = jnp.sum(p_c, axis=-1, keepdims=True) + jnp.sum(p_n, axis=-1, keepdims=True)
        vch = jnp.where(head_mask[i], vc, jnp.zeros_like(vc))
        vnh = jnp.where(head_mask[i], vn, jnp.zeros_like(vn))
        out = out + (_dot(p_c.astype(BF16), vch) + _dot(p_n.astype(BF16), vnh)) * (1.0 / denom)
    o_ref[0] = out.astype(o_ref.dtype)


def _fox_sample(q, kn, vn, cache_k, cache_v, fp):
    b, l, _ = q.shape
    past = cache_k.shape[1]
    new = pl.BlockSpec((1, l, LANES), lambda bi, j: (bi, 0, j))
    old = pl.BlockSpec((1, past, LANES), lambda bi, j: (bi, 0, j))
    return pl.pallas_call(
        _fox_sample_kernel,
        out_shape=jax.ShapeDtypeStruct((b, l, FOX_W), BF16),
        grid=(b, FOX_PAIRS),
        in_specs=[new, old, old, new, new,
                  pl.BlockSpec((1, fp.shape[1], LANES), lambda bi, j: (bi, 0, j))],
        out_specs=new,
        compiler_params=_params("arbitrary", "arbitrary"),
        name="fox_sample",
    )(q, cache_k, cache_v, kn, vn, fp)


def _layer_norm(x, w, b):
    mu = jnp.mean(x, axis=-1, keepdims=True)
    xc = x - mu
    var = jnp.mean(xc * xc, axis=-1, keepdims=True)
    return xc * lax.rsqrt(var + LN_EPS) * w + b


def _first_argmax_rows(x, n):
    rows = lax.broadcasted_iota(jnp.int32, x.shape, 0).astype(F32)
    mx = jnp.max(x, axis=0, keepdims=True)
    idx = jnp.min(jnp.where(x == mx, rows, float(n)), axis=0, keepdims=True)
    return mx, idx.astype(jnp.int32)


def _outproj_kernel(yr_ref, yf_ref, ga_ref, gb_ref, x_ref, mod_ref, wr_ref, wf_ref, wo_ref,
                    lw_ref, lb_ref, rw_ref, rb_ref,
                    x1_ref, h2_ref, eid_ref, gate_ref, gt_ref, *, alpha):
    gb_, tl, d = x_ref.shape
    tm = gb_ * tl
    flat = lambda ref: ref[...].reshape(tm, ref.shape[-1])
    y_ret = _dot(flat(yr_ref), wr_ref[...])
    y_fox = _dot(flat(yf_ref), wf_ref[...])
    mix = flat(ga_ref).astype(F32) * y_ret + flat(gb_ref).astype(F32) * y_fox
    mixed = _dot(mix.astype(BF16), wo_ref[...]).reshape(gb_, tl, d)
    g1 = mod_ref[:, :, 2 * d:3 * d]
    sh2 = mod_ref[:, :, 3 * d:4 * d]
    sc2 = mod_ref[:, :, 4 * d:5 * d]
    x1 = _layer_norm(alpha * x_ref[...] + (1.0 + g1) * mixed, lw_ref[...], lb_ref[...])
    x1_ref[...] = x1
    h2 = (x1 * (1.0 + sc2) + sh2)
    h2_ref[...] = h2.astype(h2_ref.dtype)
    h_hi, h_mid, h_lo = _split3(h2.reshape(tm, d))
    w_hi, w_mid, w_lo = rw_ref[0], rw_ref[1], rw_ref[2]
    lt = (_dot_nt(w_hi, h_hi) + _dot_nt(w_hi, h_mid) + _dot_nt(w_mid, h_hi)
          + _dot_nt(w_hi, h_lo) + _dot_nt(w_lo, h_hi) + _dot_nt(w_mid, h_mid)) + rb_ref[...]
    gl = lt[N_EXPERTS:N_EXPERTS + N_GROUPS, :]
    gmax, gi = _first_argmax_rows(gl, N_GROUPS)
    g_p = 1.0 / jnp.sum(jnp.exp(gl - gmax), axis=0, keepdims=True)
    e_sel = lt[0:EXPERTS_PER_GROUP, :]
    for g in range(1, N_GROUPS):
        e_sel = jnp.where(gi == g, lt[g * EXPERTS_PER_GROUP:(g + 1) * EXPERTS_PER_GROUP, :], e_sel)
    rows = lax.broadcasted_iota(jnp.int32, e_sel.shape, 0)
    m1, i1 = _first_argmax_rows(e_sel, EXPERTS_PER_GROUP)
    m2, i2 = _first_argmax_rows(jnp.where(rows == i1, -jnp.inf, e_sel), EXPERTS_PER_GROUP)
    r = jnp.exp(m2 - m1)
    gate0 = g_p / (1.0 + r)
    gate1 = g_p * r / (1.0 + r)
    eid_ref[0:1, :] = gi * EXPERTS_PER_GROUP + i1
    eid_ref[1:2, :] = gi * EXPERTS_PER_GROUP + i2
    gate_ref[0:1, :] = gate0
    gate_ref[1:2, :] = gate1
    rr = lax.broadcasted_iota(jnp.int32, (LANES, tm), 0)
    gsq = jnp.where(rr == 0, gate0, jnp.where(rr == 1, gate1, 0.0))
    gt_ref[...] = gsq.T


def _outproj(yr, yf, ga, gb, x, mod, wr, wf, wo, lw, lb, rw3, rb, gb_, tl, alpha):
    bx, l, d = x.shape
    m = bx * l
    tm = gb_ * tl
    nl = l // tl
    seq = lambda w: pl.BlockSpec((gb_, tl, w), lambda b, j: (b, j, 0))
    tok = lambda r: pl.BlockSpec((r, tm), lambda b, j: (0, b * nl + j))
    return pl.pallas_call(
        functools.partial(_outproj_kernel, alpha=alpha),
        out_shape=[jax.ShapeDtypeStruct((bx, l, d), F32), jax.ShapeDtypeStruct((bx, l, d), BF16),
                   jax.ShapeDtypeStruct((2, m), jnp.int32), jax.ShapeDtypeStruct((2, m), F32),
                   jax.ShapeDtypeStruct((m, LANES), F32)],
        grid=(bx // gb_, nl),
        in_specs=[seq(RET_V_W), seq(FOX_W), seq(d), seq(d), seq(d),
                  pl.BlockSpec((gb_, 1, mod.shape[-1]), lambda b, j: (b, 0, 0)),
                  _resident(wr.shape), _resident(wf.shape), _resident(wo.shape),
                  _resident(lw.shape), _resident(lb.shape), _resident(rw3.shape), _resident(rb.shape)],
        out_specs=[seq(d), seq(d), tok(2), tok(2),
                   pl.BlockSpec((tm, LANES), lambda b, j: (b * nl + j, 0))],
        compiler_params=_params("arbitrary", "arbitrary"),
        name="outproj",
    )(yr, yf, ga, gb, x, mod, wr, wf, wo, lw, lb, rw3, rb)


MOE_CHUNK = 8


def _ceil_chunk(x):
    return jnp.floor((x + (MOE_CHUNK - 1.0)) * (1.0 / MOE_CHUNK)) * MOE_CHUNK


def _plan_kernel(eid_ref, triu_ref, ld_ref, pc_ref, pre_ref, carry_ref):
    @pl.when(pl.program_id(0) == 0)
    def _():
        carry_ref[...] = jnp.zeros_like(carry_ref)

    t = eid_ref.shape[1]
    experts = lax.broadcasted_iota(jnp.int32, (N_EXPERTS, t), 0)
    hit = [eid_ref[kk:kk + 1, :] == experts for kk in range(2)]
    onehot = [jnp.where(h, 1.0, 0.0) for h in hit]
    onehot_b = [o.astype(BF16) for o in onehot]
    ones = jnp.ones((8, t), BF16)
    cnt_row = (_dot_nt(ones, onehot_b[0]) + _dot_nt(ones, onehot_b[1]))[0:1, :]
    cnt0_col = jnp.sum(onehot[0], axis=1, keepdims=True)
    pc_row = _ceil_chunk(cnt_row)
    e_lane = lax.broadcasted_iota(jnp.int32, (N_EXPERTS, N_EXPERTS), 1)
    e_sub = lax.broadcasted_iota(jnp.int32, (N_EXPERTS, N_EXPERTS), 0)
    lstart_col = jnp.sum(jnp.where(e_lane < e_sub, pc_row, 0.0), axis=1, keepdims=True)
    for kk in range(2):
        before = _dot(onehot_b[kk], triu_ref[...])
        base = lstart_col + cnt0_col if kk == 1 else lstart_col
        row = jnp.sum(jnp.where(hit[kk], before + base, 0.0), axis=0, keepdims=True)
        ld_ref[kk:kk + 1, :] = row.astype(jnp.int32)
    pc_ref[0] = pc_row.astype(jnp.int32)
    pre_ref[0] = carry_ref[...].astype(jnp.int32)
    carry_ref[...] = carry_ref[...] + pc_row


def _plan(eid, t):
    m = eid.shape[1]
    n_tiles = m // t
    triu = jnp.asarray(np.triu(np.ones((t, t), np.float32), 1), BF16)
    per_tile = jax.ShapeDtypeStruct((n_tiles, 1, N_EXPERTS), jnp.int32)
    per_tile_spec = pl.BlockSpec((1, 1, N_EXPERTS), lambda i: (i, 0, 0))
    return pl.pallas_call(
        _plan_kernel,
        out_shape=[jax.ShapeDtypeStruct((2, m), jnp.int32), per_tile, per_tile],
        grid=(n_tiles,),
        in_specs=[pl.BlockSpec((2, t), lambda i: (0, i)), pl.BlockSpec((t, t), lambda i: (0, 0))],
        out_specs=[pl.BlockSpec((2, t), lambda i: (0, i)), per_tile_spec, per_tile_spec],
        scratch_shapes=[pltpu.VMEM((1, N_EXPERTS), F32)],
        compiler_params=_params("arbitrary"),
        name="moe_plan",
    )(eid, triu)


def _for_each_chunk(tile, pc_ref, goff_ref, fn):
    def per_expert(e, local):
        n = lax.div(pc_ref[tile * N_EXPERTS + e], jnp.int32(MOE_CHUNK))
        dst = goff_ref[tile * N_EXPERTS + e]

        def per_chunk(j, c):
            fn(pl.multiple_of(local + j * MOE_CHUNK, MOE_CHUNK), pl.multiple_of(dst + j * MOE_CHUNK, MOE_CHUNK))
            return c

        lax.fori_loop(0, n, per_chunk, 0)
        return local + n * MOE_CHUNK

    lax.fori_loop(0, N_EXPERTS, per_expert, 0)


def _one_hot_rows(ld_row, lp):
    rows = lax.broadcasted_iota(jnp.int32, (lp, ld_row.shape[1]), 0)
    return rows == ld_row


def _dispatch_kernel(pc_ref, goff_ref, tail_ref, ld_ref, h_ref, xs_ref, buf_ref, zero_ref, sem, zero_sem):
    tile = pl.program_id(0)
    n_tiles = pl.num_programs(0)
    slot = tile % 2
    lp = buf_ref.shape[1]
    tb = zero_ref.shape[0]

    def copy(s, local, dst):
        return pltpu.make_async_copy(buf_ref.at[s, pl.ds(local, MOE_CHUNK)], xs_ref.at[pl.ds(dst, MOE_CHUNK)],
                                     sem.at[s])

    def fill_chunk(dst):
        return pltpu.make_async_copy(zero_ref.at[pl.ds(0, MOE_CHUNK)], xs_ref.at[pl.ds(dst, MOE_CHUNK)], zero_sem)

    def fill_block(dst):
        return pltpu.make_async_copy(zero_ref, xs_ref.at[pl.ds(dst, tb)], zero_sem)

    def for_each_fill(on_chunk, on_block):
        def per_expert(e, c):
            first = tail_ref[e]

            def per_chunk(j, cc):
                on_chunk(pl.multiple_of(first + j * MOE_CHUNK, MOE_CHUNK))
                return cc

            lax.fori_loop(0, tail_ref[N_EXPERTS + e], per_chunk, 0)
            return c

        lax.fori_loop(0, N_EXPERTS, per_expert, 0)
        first = tail_ref[2 * N_EXPERTS]

        def per_block(j, c):
            on_block(pl.multiple_of(first + j * tb, tb))
            return c

        lax.fori_loop(0, tail_ref[2 * N_EXPERTS + 1], per_block, 0)

    perm = jnp.where(_one_hot_rows(ld_ref[0:1, :], lp), 1.0,
                     jnp.where(_one_hot_rows(ld_ref[1:2, :], lp), 1.0, 0.0)).astype(BF16)
    buf_ref[slot] = _dot(perm, h_ref[...])
    _for_each_chunk(tile, pc_ref, goff_ref, lambda a, b: copy(slot, a, b).start())

    @pl.when(tile == 0)
    def _():
        zero_ref[...] = jnp.zeros_like(zero_ref)
        for_each_fill(lambda d: fill_chunk(d).start(), lambda d: fill_block(d).start())
        for_each_fill(lambda d: fill_chunk(d).wait(), lambda d: fill_block(d).wait())

    @pl.when(tile > 0)
    def _():
        _for_each_chunk(tile - 1, pc_ref, goff_ref, lambda a, b: copy(1 - slot, a, b).wait())

    @pl.when(tile == n_tiles - 1)
    def _():
        _for_each_chunk(tile, pc_ref, goff_ref, lambda a, b: copy(slot, a, b).wait())


def _dispatch(pc, goff, tails, ld, h2, n_slots, t, lp, tb):
    m, d = h2.shape
    return pl.pallas_call(
        _dispatch_kernel,
        out_shape=jax.ShapeDtypeStruct((n_slots, d), F32),
        grid_spec=pltpu.PrefetchScalarGridSpec(
            num_scalar_prefetch=3,
            grid=(m // t,),
            in_specs=[pl.BlockSpec((2, t), lambda i, *_: (0, i)),
                      pl.BlockSpec((t, d), lambda i, *_: (i, 0))],
            out_specs=pl.BlockSpec(memory_space=pl.ANY),
            scratch_shapes=[pltpu.VMEM((2, lp, d), F32), pltpu.VMEM((tb, d), F32),
                            pltpu.SemaphoreType.DMA((2,)), pltpu.SemaphoreType.DMA]),
        compiler_params=_params("arbitrary"),
        name="moe_dispatch",
    )(pc, goff, tails, ld, h2)


def _expert_kernel(be_ref, nv_ref, x_ref, w1_ref, w2_ref, y_ref):
    del be_ref

    @pl.when(pl.program_id(0) < nv_ref[0])
    def _():
        e = w2_ref.shape[1]
        au = _dot(x_ref[...].astype(BF16), w1_ref[0])
        a = au[:, :e]
        u = au[:, e:]
        y_ref[...] = _dot((a * _sigmoid(a) * u).astype(BF16), w2_ref[0])

    @pl.when(pl.program_id(0) >= nv_ref[0])
    def _():
        y_ref[...] = jnp.zeros_like(y_ref)


def _experts(block_e, n_valid, xs, w1, w2, tb):
    p, d = xs.shape
    e = w2.shape[1]
    rows = lambda i, be, nv: (jnp.minimum(i, nv[0] - 1), 0)
    return pl.pallas_call(
        _expert_kernel,
        out_shape=jax.ShapeDtypeStruct((p, d), F32),
        grid_spec=pltpu.PrefetchScalarGridSpec(
            num_scalar_prefetch=2,
            grid=(p // tb,),
            in_specs=[pl.BlockSpec((tb, d), rows),
                      pl.BlockSpec((1, d, 2 * e), lambda i, be, nv: (be[i], 0, 0)),
                      pl.BlockSpec((1, e, d), lambda i, be, nv: (be[i], 0, 0))],
            out_specs=pl.BlockSpec((tb, d), lambda i, be, nv: (i, 0))),
        compiler_params=_params("arbitrary"),
        name="moe_experts",
    )(block_e, n_valid, xs, w1, w2)


def _combine_kernel(pc_ref, goff_ref, ld_ref, y_hbm, gt_ref, x1_ref, mod_ref, lw_ref, lb_ref, o_ref,
                    buf_ref, sem, *, alpha):
    gb_, tl, d = x1_ref.shape
    nl = pl.num_programs(1)
    n_tiles = pl.num_programs(0) * nl
    tile = pl.program_id(0) * nl + pl.program_id(1)
    slot = tile % 2
    lp = buf_ref.shape[1]

    def copy(s, local, src):
        return pltpu.make_async_copy(y_hbm.at[pl.ds(src, MOE_CHUNK)], buf_ref.at[s, pl.ds(local, MOE_CHUNK)],
                                     sem.at[s])

    @pl.when(tile == 0)
    def _():
        buf_ref[...] = jnp.zeros_like(buf_ref)
        _for_each_chunk(tile, pc_ref, goff_ref, lambda a, b: copy(slot, a, b).start())

    @pl.when(tile + 1 < n_tiles)
    def _():
        _for_each_chunk(tile + 1, pc_ref, goff_ref, lambda a, b: copy(1 - slot, a, b).start())

    _for_each_chunk(tile, pc_ref, goff_ref, lambda a, b: copy(slot, a, b).wait())
    yb = buf_ref[slot].astype(BF16)
    picked = [_dot_tn(jnp.where(_one_hot_rows(ld_ref[kk:kk + 1, :], lp), 1.0, 0.0).astype(BF16), yb)
              for kk in range(2)]
    y = gt_ref[:, 0:1] * picked[0] + gt_ref[:, 1:2] * picked[1]
    g2 = mod_ref[:, :, 5 * d:6 * d]
    o_ref[...] = _layer_norm(alpha * x1_ref[...] + (1.0 + g2) * y.reshape(gb_, tl, d), lw_ref[...], lb_ref[...])


def _combine(pc, goff, ld, y, gt, x1, mod, lw, lb, gb_, tl, alpha, lp):
    bx, l, d = x1.shape
    tm = gb_ * tl
    nl = l // tl
    seq = pl.BlockSpec((gb_, tl, d), lambda b, j, pc, go: (b, j, 0))
    fixed = lambda shape: pl.BlockSpec(shape, lambda b, j, pc, go: (0,) * len(shape), pipeline_mode=pl.Buffered(1))
    return pl.pallas_call(
        functools.partial(_combine_kernel, alpha=alpha),
        out_shape=jax.ShapeDtypeStruct((bx, l, d), F32),
        grid_spec=pltpu.PrefetchScalarGridSpec(
            num_scalar_prefetch=2,
            grid=(bx // gb_, nl),
            in_specs=[pl.BlockSpec((2, tm), lambda b, j, pc, go: (0, b * nl + j)),
                      pl.BlockSpec(memory_space=pl.ANY),
                      pl.BlockSpec((tm, LANES), lambda b, j, pc, go: (b * nl + j, 0)),
                      seq,
                      pl.BlockSpec((gb_, 1, mod.shape[-1]), lambda b, j, pc, go: (b, 0, 0)),
                      fixed(lw.shape), fixed(lb.shape)],
            out_specs=seq,
            scratch_shapes=[pltpu.VMEM((2, lp, d), F32), pltpu.SemaphoreType.DMA((2,))]),
        compiler_params=_params("arbitrary", "arbitrary"),
        name="moe_combine",
    )(pc, goff, ld, y, gt, x1, mod, lw, lb)


def _moe(h2, eid, gt, x1, mod, w1, w2, lw, lb, gb_, tl, alpha, tb):
    bx, l, d = x1.shape
    m = bx * l
    t = gb_ * tl
    n_tiles = m // t
    lp = 2 * t + N_EXPERTS * MOE_CHUNK
    ld, pc, pre = _plan(eid, t)
    pc = pc.reshape(n_tiles, N_EXPERTS)
    pre = pre.reshape(n_tiles, N_EXPERTS)
    total = pre[-1] + pc[-1]
    region = (total + tb - 1) // tb * tb
    gend = jnp.cumsum(region)
    goff = (gend - region)[None, :] + pre
    n_blocks = -(-(2 * m + n_tiles * N_EXPERTS * (MOE_CHUNK - 1) + N_EXPERTS * (tb - 1)) // tb)
    block_row0 = jnp.arange(n_blocks, dtype=jnp.int32) * tb
    block_e = jnp.minimum(jnp.sum((gend[None, :] <= block_row0[:, None]).astype(jnp.int32), axis=1),
                          N_EXPERTS - 1).astype(jnp.int32)
    n_valid = (gend[-1:] // tb).astype(jnp.int32)
    tails = jnp.concatenate([gend - region + total, (region - total) // MOE_CHUNK,
                             gend[-1:], n_blocks - gend[-1:] // tb]).astype(jnp.int32)
    pc = pc.reshape(-1).astype(jnp.int32)
    goff = goff.reshape(-1).astype(jnp.int32)
    xs = _dispatch(pc, goff, tails, ld, h2.reshape(m, d), n_blocks * tb, t, lp, tb)
    y = _experts(block_e, n_valid, xs, w1, w2, tb)
    return _combine(pc, goff, ld, y, gt, x1, mod, lw, lb, gb_, tl, alpha, lp)


def _rotary_tables(pos, reps):
    half = RET_QK_DIM // 2
    inv = ROPE_BASE ** (-jnp.linspace(0.0, 1.0, half, dtype=F32))
    ang = pos.astype(F32)[:, None] * inv[None, :]
    cos = jnp.cos(ang)
    sin = jnp.sin(ang)
    c2 = jnp.concatenate([cos, cos], axis=1)
    s2 = jnp.concatenate([-sin, sin], axis=1)
    kscale = RET_QK_DIM ** -0.5
    tabs = (c2, s2, c2 * kscale, s2 * kscale)
    return tuple(jnp.tile(t, (reps, 1)) for t in tabs)


def _pick(n, pref):
    t = min(n, pref)
    while n % t:
        t //= 2
    return t


def _layer(x, mod, pos, weights, log_gamma, alpha, cache=None):
    (w2, b2, gn_w, gn_b, wr, wf, wo, ln1w, ln1b, rw3, rb, we1, we2, ln2w, ln2b) = weights
    bx, l, d = x.shape
    if cache is None:
        gb_, tl = 1, _pick(l, 512)
    else:
        gb_, tl = _pick(bx, max(1, 512 // l)), l
    tabs = _rotary_tables(pos, gb_ if cache is not None else 1)
    rq, rk, rv, rg, fq, fk, fv, fkb, fvb, lf, ga, gb = _inproj(x, mod, w2, b2, tabs, gb_, tl)

    if cache is None:
        lc = _pick(l, 256)
        y_ret, state = _retention(rq, rk, rv, rg, gn_w, gn_b, log_gamma, lc)
        fp = _fcum(lf, _pick(l, 512))
        y_fox = _fox_prompt(fq, fkb, fp, fvb, _pick(l, 512))
    else:
        state0, cache_k, cache_v, cache_logf = cache
        y_ret, state = _retention(rq, rk, rv, rg, gn_w, gn_b, log_gamma, l, state0)
        past = cache_k.shape[1]
        total = -(-(past + l) // LANES) * LANES
        tf = total
        lf_all = jnp.concatenate(
            [jnp.pad(cache_logf.astype(F32), ((0, 0), (0, 0), (0, LANES - FOX_HEADS))), lf,
             jnp.zeros((bx, total - past - l, LANES), F32)], axis=1)
        fp = _fcum(lf_all, tf)
        y_fox = _fox_sample(fq, fkb, fvb, cache_k.reshape(bx, past, FOX_W), cache_v.reshape(bx, past, FOX_W), fp)

    x1, h2, eid, gate, gt = _outproj(y_ret, y_fox, ga, gb, x, mod, wr, wf, wo, ln1w, ln1b, rw3, rb,
                                     gb_, tl, alpha)
    del gate
    m = bx * l
    out = _moe(h2, eid, gt, x1, mod, we1, we2, ln2w, ln2b, gb_, tl, alpha, tb=256)
    return out, state, fk, fv, lf[:, :, :FOX_HEADS]


def kernel(x_prompt, x_sample, state_ret, cache_fox_k, cache_fox_v, cache_fox_logf, c_prompt, c_sample,
           w_ada, b_ada, w_in, b_in, ret_gn_w, ret_gn_b, w_ret_proj, w_fox_proj, w_o, ln1_w, ln1_b,
           w_rg, b_rg, w_re, b_re, w_e_in, w_e_out, ln2_w, ln2_b):
    depth = w_ada.shape[0]
    d = x_prompt.shape[-1]
    bp, s, _ = x_prompt.shape
    bs, ls, _ = x_sample.shape
    past = cache_fox_k.shape[2]
    alpha = (2 * depth) ** 0.25
    log_gamma = jnp.log1p(-jnp.exp(jnp.linspace(math.log(1.0 / 32), math.log(1.0 / 512), RET_HEADS, dtype=F32)))
    pos_p = jnp.arange(s, dtype=jnp.int32)
    pos_s = past + jnp.arange(ls, dtype=jnp.int32)
    fg = 2 * RET_QK_W + 2 * RET_V_W + 3 * FOX_W

    xp, xs = x_prompt, x_sample
    p_ret, p_k, p_v, p_f, s_ret, s_k, s_v, s_f = [], [], [], [], [], [], [], []
    for li in range(depth):
        w2 = jnp.concatenate([w_in[li][:, :fg], w_in[li][:, fg + FOX_HEADS:],
                              jnp.pad(w_in[li][:, fg:fg + FOX_HEADS], ((0, 0), (0, LANES - FOX_HEADS)))],
                             axis=1).astype(BF16)
        b2 = jnp.concatenate([b_in[li][:fg], b_in[li][fg + FOX_HEADS:],
                              jnp.pad(b_in[li][fg:fg + FOX_HEADS], (0, LANES - FOX_HEADS))]).reshape(1, -1)
        n_rt = N_EXPERTS + N_GROUPS
        rt_rows = -(-n_rt // 8) * 8
        rwt = jnp.pad(jnp.concatenate([w_re[li], w_rg[li]], axis=1).T.astype(F32), ((0, rt_rows - n_rt), (0, 0)))
        r_hi = rwt.astype(BF16)
        r_mid = (rwt - r_hi.astype(F32)).astype(BF16)
        r_lo = (rwt - r_hi.astype(F32) - r_mid.astype(F32)).astype(BF16)
        rw3 = jnp.stack([r_hi, r_mid, r_lo])
        rb = jnp.pad(jnp.concatenate([b_re[li], b_rg[li]]).astype(F32), (0, rt_rows - n_rt)).reshape(rt_rows, 1)
        weights = (w2, b2, ret_gn_w[li], ret_gn_b[li],
                   w_ret_proj[li].astype(BF16), w_fox_proj[li].astype(BF16), w_o[li].astype(BF16),
                   ln1_w[li].reshape(1, d), ln1_b[li].reshape(1, d), rw3, rb,
                   w_e_in[li].astype(BF16), w_e_out[li].astype(BF16),
                   ln2_w[li].reshape(1, d), ln2_b[li].reshape(1, d))
        mod = _ada(jnp.concatenate([c_prompt, c_sample], axis=0), w_ada[li], b_ada[li])
        mod_p = mod[:bp].reshape(bp, 1, 6 * d)
        mod_s = mod[bp:].reshape(bs, 1, 6 * d)
        xp, st, kk, vv, ff = _layer(xp, mod_p, pos_p, weights, log_gamma, alpha)
        p_ret.append(st); p_k.append(kk.reshape(bp, s, FOX_HEADS, FOX_HEAD_DIM))
        p_v.append(vv.reshape(bp, s, FOX_HEADS, FOX_HEAD_DIM)); p_f.append(ff)
        cache = (state_ret[li].astype(F32), cache_fox_k[li], cache_fox_v[li], cache_fox_logf[li])
        xs, st, kk, vv, ff = _layer(xs, mod_s, pos_s, weights, log_gamma, alpha, cache)
        s_ret.append(st); s_k.append(kk.reshape(bs, ls, FOX_HEADS, FOX_HEAD_DIM))
        s_v.append(vv.reshape(bs, ls, FOX_HEADS, FOX_HEAD_DIM)); s_f.append(ff)
    return (xp, xs, jnp.stack(p_ret), jnp.stack(p_k), jnp.stack(p_v), jnp.stack(p_f),
            jnp.stack(s_ret), jnp.stack(s_k), jnp.stack(s_v), jnp.stack(s_f))
```

```python
import functools
import math

import jax
import jax.numpy as jnp
import numpy as np
from jax import lax
from jax.experimental import pallas as pl
from jax.experimental.pallas import tpu as pltpu

F32 = jnp.float32
BF16 = jnp.bfloat16

RET_HEADS = 4
RET_QK_DIM = 128
RET_V_DIM = 256
ROPE_BASE = 10000.0
FOX_HEADS = 16
FOX_HEAD_DIM = 64
N_GROUPS = 4
EXPERTS_PER_GROUP = 8
N_EXPERTS = N_GROUPS * EXPERTS_PER_GROUP
LN_EPS = 1e-5
GN_EPS = 1e-6
RET_QK_W = RET_HEADS * RET_QK_DIM
RET_V_W = RET_HEADS * RET_V_DIM
FOX_W = FOX_HEADS * FOX_HEAD_DIM
FOX_PAIRS = FOX_HEADS // 2
LOG2_E = math.log2(math.e)
FOX_Q_SCALE = FOX_HEAD_DIM ** -0.5 * LOG2_E

LANES = 128
VMEM_LIMIT_BYTES = 56 * 1024 * 1024
MASK_VALUE = -1e30


def _dot(a, b):
    return jnp.dot(a, b, preferred_element_type=F32)


def _dot_nt(a, b):
    return lax.dot_general(a, b, (((1,), (1,)), ((), ())), preferred_element_type=F32)


def _dot_tn(a, b):
    return lax.dot_general(a, b, (((0,), (0,)), ((), ())), preferred_element_type=F32)


def _split3(x):
    hi = x.astype(BF16)
    r1 = x - hi.astype(F32)
    mid = r1.astype(BF16)
    lo = (r1 - mid.astype(F32)).astype(BF16)
    return hi, mid, lo


def _sigmoid(x):
    return 1.0 / (1.0 + jnp.exp(-x))


def _log_sigmoid(x):
    return -(jnp.maximum(-x, 0.0) + jnp.log1p(jnp.exp(-jnp.abs(x))))


def _params(*sem):
    return pltpu.CompilerParams(dimension_semantics=sem, vmem_limit_bytes=VMEM_LIMIT_BYTES)


def _resident(shape):
    nd = len(shape)
    return pl.BlockSpec(shape, lambda *_: (0,) * nd, pipeline_mode=pl.Buffered(1))


def _ada_kernel(c_ref, w_ref, b_ref, o_ref):
    c = c_ref[...]
    s = (c * _sigmoid(c)).astype(BF16)
    o_ref[...] = _dot(s, w_ref[...].astype(BF16)) + b_ref[...]


def _ada(c, w, b):
    n, d = c.shape
    nout = w.shape[1]
    tn = d
    return pl.pallas_call(
        _ada_kernel,
        out_shape=jax.ShapeDtypeStruct((n, nout), F32),
        grid=(nout // tn,),
        in_specs=[pl.BlockSpec((n, d), lambda j: (0, 0)),
                  pl.BlockSpec((d, tn), lambda j: (0, j)),
                  pl.BlockSpec((1, tn), lambda j: (0, j))],
        out_specs=pl.BlockSpec((n, tn), lambda j: (0, j)),
        compiler_params=_params("arbitrary"),
        name="ada",
    )(c, w, b.reshape(1, nout))


def _inproj_kernel(x_ref, mod_ref, w_ref, b_ref, cq_ref, sq_ref, ck_ref, sk_ref,
                   rq_ref, rk_ref, rv_ref, rg_ref, fq_ref, fk_ref, fv_ref, fkb_ref, fvb_ref,
                   lf_ref, ga_ref, gb_ref, *, transposed_v):
    gb_, tl, d = x_ref.shape
    tm = gb_ * tl
    sh = mod_ref[:, :, 0:d]
    sc = mod_ref[:, :, d:2 * d]
    h = (x_ref[...] * (1.0 + sc) + sh).reshape(tm, d).astype(BF16)

    def proj(lo, width):
        return _dot(h, w_ref[:, lo:lo + width]) + b_ref[:, lo:lo + width]

    def put(ref, val):
        ref[...] = val.reshape(ref.shape).astype(ref.dtype)

    def rot(z, c_ref, s_ref):
        c = c_ref[...]
        s = s_ref[...]
        parts = []
        for hh in range(RET_HEADS):
            zh = z[:, hh * RET_QK_DIM:(hh + 1) * RET_QK_DIM]
            parts.append(zh * c + pltpu.roll(zh, RET_QK_DIM // 2, axis=1) * s)
        return jnp.concatenate(parts, axis=1)

    off = 0
    put(rq_ref, rot(proj(off, RET_QK_W), cq_ref, sq_ref)); off += RET_QK_W
    put(rk_ref, rot(proj(off, RET_QK_W), ck_ref, sk_ref)); off += RET_QK_W
    put(rv_ref, proj(off, RET_V_W)); off += RET_V_W
    z = proj(off, RET_V_W); off += RET_V_W
    put(rg_ref, z * _sigmoid(z))
    put(fq_ref, proj(off, FOX_W) * FOX_Q_SCALE); off += FOX_W
    z = proj(off, FOX_W); off += FOX_W
    put(fk_ref, z); put(fkb_ref, z)
    z = proj(off, FOX_W); off += FOX_W
    put(fv_ref, z)
    if transposed_v:
        fvb_ref[0] = z.T.astype(fvb_ref.dtype)
    else:
        put(fvb_ref, z)
    put(ga_ref, _sigmoid(proj(off, d))); off += d
    put(gb_ref, _sigmoid(proj(off, d))); off += d
    put(lf_ref, _log_sigmoid(proj(off, LANES)))


def _inproj(x, mod, w2, b2, tabs, gb_, tl):
    bx, l, d = x.shape
    tm = gb_ * tl
    nw = w2.shape[1]
    grid = (bx // gb_, l // tl)
    if gb_ == 1:
        tab_spec = pl.BlockSpec((tl, LANES), lambda b, j: (j, 0))
    else:
        tab_spec = pl.BlockSpec((tm, LANES), lambda b, j: (0, 0))

    def out(width, dtype):
        return (jax.ShapeDtypeStruct((bx, l, width), dtype),
                pl.BlockSpec((gb_, tl, width), lambda b, j: (b, j, 0)))

    transposed_v = gb_ == 1
    if transposed_v:
        fvb = (jax.ShapeDtypeStruct((bx, FOX_W, l), BF16), pl.BlockSpec((1, FOX_W, tl), lambda b, j: (b, 0, j)))
    else:
        fvb = out(FOX_W, BF16)
    outs = [out(RET_QK_W, BF16), out(RET_QK_W, BF16), out(RET_V_W, BF16), out(RET_V_W, BF16),
            out(FOX_W, BF16), out(FOX_W, F32), out(FOX_W, F32), out(FOX_W, BF16), fvb,
            out(LANES, F32), out(d, BF16), out(d, BF16)]
    return pl.pallas_call(
        functools.partial(_inproj_kernel, transposed_v=transposed_v),
        out_shape=[o[0] for o in outs],
        grid=grid,
        in_specs=[pl.BlockSpec((gb_, tl, d), lambda b, j: (b, j, 0)),
                  pl.BlockSpec((gb_, 1, mod.shape[-1]), lambda b, j: (b, 0, 0)),
                  _resident((d, nw)), _resident((1, nw)),
                  tab_spec, tab_spec, tab_spec, tab_spec],
        out_specs=[o[1] for o in outs],
        compiler_params=_params("arbitrary", "arbitrary"),
        name="inproj",
    )(x, mod, w2, b2, *tabs)


BIAS_PIECES = 3


def _fcum_kernel(lft_ref, triu_ref, fp_ref, carry_ref):
    @pl.when(pl.program_id(1) == 0)
    def _():
        carry_ref[...] = jnp.zeros_like(carry_ref)

    tl = lft_ref.shape[2]
    hi, mid, lo = _split3(lft_ref[0])
    triu = triu_ref[...]
    cum = _dot(hi, triu) + _dot(mid, triu) + _dot(lo, triu) + carry_ref[...]
    carry_ref[...] = cum[:, tl - 1:tl]
    padded = jnp.concatenate([cum * -LOG2_E, jnp.zeros((LANES - FOX_HEADS, tl), F32)], axis=0)
    pieces = _split3(padded.T)
    out = pieces[0].astype(F32)
    for p in range(1, BIAS_PIECES):
        out = out + pltpu.roll(pieces[p].astype(F32), p * FOX_HEADS, axis=1)
    fp_ref[0] = out.astype(fp_ref.dtype)


def _fcum(lft, tl):
    bx, _, l = lft.shape
    triu = jnp.asarray(np.triu(np.ones((tl, tl), np.float32)), BF16)
    return pl.pallas_call(
        _fcum_kernel,
        out_shape=jax.ShapeDtypeStruct((bx, l, LANES), BF16),
        grid=(bx, l // tl),
        in_specs=[pl.BlockSpec((1, FOX_HEADS, tl), lambda b, j: (b, 0, j)),
                  pl.BlockSpec((tl, tl), lambda b, j: (0, 0))],
        out_specs=pl.BlockSpec((1, tl, LANES), lambda b, j: (b, j, 0)),
        scratch_shapes=[pltpu.VMEM((FOX_HEADS, 1), F32)],
        compiler_params=_params("arbitrary", "arbitrary"),
        name="fcum",
    )(lft, triu)


def _retention_kernel(*refs, has_state):
    if has_state:
        (q_ref, k_ref, v_ref, g_ref, dec_ref, qd_ref, kd_ref, gw_ref, gb_ref, s0_ref,
         y_ref, st_ref) = refs
    else:
        (q_ref, k_ref, v_ref, g_ref, dec_ref, qd_ref, kd_ref, gw_ref, gb_ref,
         y_ref, st_ref) = refs
        s0_ref = None
    lc = q_ref.shape[1]

    @pl.when(pl.program_id(1) == 0)
    def _():
        if has_state:
            st_ref[...] = s0_ref[...]
        else:
            st_ref[...] = jnp.zeros_like(st_ref)

    for hh in range(RET_HEADS):
        qs = slice(hh * RET_QK_DIM, (hh + 1) * RET_QK_DIM)
        vs = slice(hh * RET_V_DIM, (hh + 1) * RET_V_DIM)
        q = q_ref[0, :, qs]
        k = k_ref[0, :, qs]
        v = v_ref[0, :, vs]
        state = st_ref[0, hh]
        qd = qd_ref[hh]
        scores = _dot_nt(q, k) * dec_ref[hh]
        inner = _dot(scores.astype(BF16), v)
        cross = _dot(q, state.astype(BF16)) * jnp.concatenate([qd] * (RET_V_DIM // LANES), axis=1)
        o = inner + cross
        kdec = (k.astype(F32) * kd_ref[hh]).astype(BF16)
        st_ref[0, hh] = qd[lc - 1:lc, 0:1] * state + _dot_tn(kdec, v)
        mu = jnp.mean(o, axis=-1, keepdims=True)
        oc = o - mu
        var = jnp.mean(oc * oc, axis=-1, keepdims=True)
        on = oc * lax.rsqrt(var + GN_EPS) * gw_ref[:, vs] + gb_ref[:, vs]
        y_ref[0, :, vs] = (g_ref[0, :, vs].astype(F32) * on).astype(y_ref.dtype)


def _retention_tables(log_gamma, lc):
    n = jnp.arange(lc, dtype=F32)
    diff = n[:, None] - n[None, :]
    decay = jnp.where(diff[None] >= 0, jnp.exp(jnp.maximum(diff, 0.0)[None] * log_gamma[:, None, None]), 0.0)
    qdec = jnp.exp((n + 1.0)[None, :, None] * log_gamma[:, None, None])
    kdec = jnp.exp((lc - 1.0 - n)[None, :, None] * log_gamma[:, None, None])
    lanes = (RET_HEADS, lc, LANES)
    return decay, jnp.broadcast_to(qdec, lanes), jnp.broadcast_to(kdec, lanes)


def _retention(q, k, v, g, gn_w, gn_b, log_gamma, lc, state0=None):
    bx, l, _ = q.shape
    decay, qdec, kdec = _retention_tables(log_gamma, lc)
    has_state = state0 is not None
    seq = lambda w: pl.BlockSpec((1, lc, w), lambda b, c: (b, c, 0))
    whole = lambda a: pl.BlockSpec(a.shape, lambda b, c: (0,) * a.ndim)
    st_spec = pl.BlockSpec((1, RET_HEADS, RET_QK_DIM, RET_V_DIM), lambda b, c: (b, 0, 0, 0))
    gw = gn_w.reshape(1, RET_V_W)
    gb = gn_b.reshape(1, RET_V_W)
    args = [q, k, v, g, decay, qdec, kdec, gw, gb]
    in_specs = [seq(RET_QK_W), seq(RET_QK_W), seq(RET_V_W), seq(RET_V_W),
                whole(decay), whole(qdec), whole(kdec), whole(gw), whole(gb)]
    if has_state:
        args.append(state0)
        in_specs.append(st_spec)
    return pl.pallas_call(
        functools.partial(_retention_kernel, has_state=has_state),
        out_shape=[jax.ShapeDtypeStruct((bx, l, RET_V_W), BF16),
                   jax.ShapeDtypeStruct((bx, RET_HEADS, RET_QK_DIM, RET_V_DIM), F32)],
        grid=(bx, l // lc),
        in_specs=in_specs,
        out_specs=[seq(RET_V_W), st_spec],
        compiler_params=_params("arbitrary", "arbitrary"),
        name="retention",
    )(*args)


def _pair_queries(q2, pair):
    t = q2.shape[0]
    lane = lax.broadcasted_iota(jnp.int32, (t, LANES), 1)
    out = []
    for i in range(2):
        head = (lane >= i * FOX_HEAD_DIM) & (lane < (i + 1) * FOX_HEAD_DIM)
        offset = lane - (2 * pair + i)
        ones = (offset >= 0) & (offset < BIAS_PIECES * FOX_HEADS) & ((offset & (FOX_HEADS - 1)) == 0)
        out.append(jnp.concatenate([jnp.where(head, q2, jnp.zeros_like(q2)),
                                    jnp.where(ones, 1.0, 0.0).astype(q2.dtype)], axis=1))
    return out


def _fox_prompt_kernel(q_ref, k_ref, fp_ref, vt_ref, o_ref, m_ref, l_ref, acc_ref, qc_ref, sa_ref, sb_ref, *, t):
    nq = q_ref.shape[1] // t
    steps = [(qi, ki) for qi in range(nq) for ki in range(qi + 1)]
    bufs = (sa_ref, sb_ref)

    def produce(n, i):
        qi, ki = steps[n]
        if ki == 0 and i == 0:
            qc = _pair_queries(q_ref[0, qi * t:(qi + 1) * t, :], pl.program_id(1))
            qc_ref[0] = qc[0]
            qc_ref[1] = qc[1]
        kc = jnp.concatenate([k_ref[0, ki * t:(ki + 1) * t, :], fp_ref[0, ki * t:(ki + 1) * t, :]], axis=1)
        bufs[n % 2][i] = _dot_nt(kc, qc_ref[i])

    def consume(n, i):
        qi, ki = steps[n]
        rows = slice(i * FOX_HEAD_DIM, (i + 1) * FOX_HEAD_DIM)
        s = bufs[n % 2][i]
        if ki == qi:
            key = lax.broadcasted_iota(jnp.int32, (t, t), 0)
            qry = lax.broadcasted_iota(jnp.int32, (t, t), 1)
            s = jnp.where(qry >= key, s, MASK_VALUE)
        smax = jnp.max(s, axis=0, keepdims=True)
        m_new = smax if ki == 0 else jnp.maximum(m_ref[i], smax)
        p = jnp.exp2(s - m_new)
        psum = jnp.sum(p, axis=0, keepdims=True)
        pv = _dot(vt_ref[0, rows, ki * t:(ki + 1) * t], p.astype(BF16))
        if ki == 0:
            l_ref[i] = psum
            acc_ref[rows, :] = pv
        else:
            alpha = jnp.exp2(m_ref[i] - m_new)
            l_ref[i] = alpha * l_ref[i] + psum
            acc_ref[rows, :] = acc_ref[rows, :] * alpha + pv
        m_ref[i] = m_new
        if ki == qi and i == 1:
            out_t = jnp.concatenate(
                [acc_ref[h * FOX_HEAD_DIM:(h + 1) * FOX_HEAD_DIM, :] * (1.0 / l_ref[h]) for h in range(2)], axis=0)
            o_ref[0, qi * t:(qi + 1) * t, :] = out_t.T.astype(o_ref.dtype)

    for i in range(2):
        produce(0, i)
    for n in range(len(steps)):
        for i in range(2):
            if n + 1 < len(steps):
                produce(n + 1, i)
            consume(n, i)


def _fox_prompt(q, k, fp, vt, t):
    b, s, _ = q.shape
    seq = pl.BlockSpec((1, s, LANES), lambda bi, j: (bi, 0, j))
    return pl.pallas_call(
        functools.partial(_fox_prompt_kernel, t=t),
        out_shape=jax.ShapeDtypeStruct((b, s, FOX_W), BF16),
        grid=(b, FOX_PAIRS),
        in_specs=[seq, seq, pl.BlockSpec((1, s, LANES), lambda bi, j: (bi, 0, 0)),
                  pl.BlockSpec((1, LANES, s), lambda bi, j: (bi, j, 0))],
        out_specs=seq,
        scratch_shapes=[pltpu.VMEM((2, 1, t), F32), pltpu.VMEM((2, 1, t), F32),
                        pltpu.VMEM((LANES, t), F32), pltpu.VMEM((2, t, 2 * LANES), BF16),
                        pltpu.VMEM((2, t, t), F32), pltpu.VMEM((2, t, t), F32)],
        compiler_params=_params("arbitrary", "arbitrary"),
        name="fox_prompt",
    )(q, k, fp, vt)


def _fox_sample_kernel(q_ref, kc_ref, vc_ref, kn_ref, vn_ref, fp_ref, o_ref):
    l = q_ref.shape[1]
    past = kc_ref.shape[1]
    lane = lax.broadcasted_iota(jnp.int32, (1, LANES), 1)
    head_mask = (lane < FOX_HEAD_DIM, lane >= FOX_HEAD_DIM)
    row = lax.broadcasted_iota(jnp.int32, (l, l), 0)
    col = lax.broadcasted_iota(jnp.int32, (l, l), 1)
    fp_old = fp_ref[0, 0:past, :]
    fp_new = fp_ref[0, past:past + l, :]
    for pair in range(FOX_PAIRS):
        lanes = slice(pair * LANES, (pair + 1) * LANES)
        qc = _pair_queries(q_ref[0, :, lanes], pair)
        kc = jnp.concatenate([kc_ref[0, :, lanes].astype(BF16), fp_old], axis=1)
        kn = jnp.concatenate([kn_ref[0, :, lanes], fp_new], axis=1)
        vc = vc_ref[0, :, lanes].astype(BF16)
        vn = vn_ref[0, :, lanes]
        out = jnp.zeros((l, LANES), F32)
        for i in range(2):
            s_c = _dot_nt(qc[i], kc)
            s_n = jnp.where(row >= col, _dot_nt(qc[i], kn), MASK_VALUE)
            m = jnp.maximum(jnp.max(s_c, axis=-1, keepdims=True), jnp.max(s_n, axis=-1, keepdims=True))
            p_c = jnp.exp2(s_c - m)
            p_n = jnp.exp2(s_n - m)
            denom = jnp.sum(p_c, axis=-1, keepdims=True) + jnp.sum(p_n, axis=-1, keepdims=True)
            vch = jnp.where(head_mask[i], vc, jnp.zeros_like(vc))
            vnh = jnp.where(head_mask[i], vn, jnp.zeros_like(vn))
            out = out + (_dot(p_c.astype(BF16), vch) + _dot(p_n.astype(BF16), vnh)) * (1.0 / denom)
        o_ref[0, :, lanes] = out.astype(o_ref.dtype)


def _fox_sample(q, kn, vn, cache_k, cache_v, fp):
    b, l, _ = q.shape
    past = cache_k.shape[1]
    new = pl.BlockSpec((1, l, FOX_W), lambda bi: (bi, 0, 0))
    old = pl.BlockSpec((1, past, FOX_W), lambda bi: (bi, 0, 0))
    return pl.pallas_call(
        _fox_sample_kernel,
        out_shape=jax.ShapeDtypeStruct((b, l, FOX_W), BF16),
        grid=(b,),
        in_specs=[new, old, old, new, new,
                  pl.BlockSpec((1, fp.shape[1], LANES), lambda bi: (bi, 0, 0))],
        out_specs=new,
        compiler_params=_params("arbitrary"),
        name="fox_sample",
    )(q, cache_k, cache_v, kn, vn, fp)


def _layer_norm(x, w, b):
    mu = jnp.mean(x, axis=-1, keepdims=True)
    xc = x - mu
    var = jnp.mean(xc * xc, axis=-1, keepdims=True)
    return xc * lax.rsqrt(var + LN_EPS) * w + b


def _first_argmax_rows(x, n):
    rows = lax.broadcasted_iota(jnp.int32, x.shape, 0).astype(F32)
    mx = jnp.max(x, axis=0, keepdims=True)
    idx = jnp.min(jnp.where(x == mx, rows, float(n)), axis=0, keepdims=True)
    return mx, idx.astype(jnp.int32)


def _outproj_kernel(yr_ref, yf_ref, ga_ref, gb_ref, x_ref, mod_ref, wr_ref, wf_ref, wo_ref,
                    lw_ref, lb_ref, rw_ref, rb_ref,
                    x1_ref, h2_ref, eid_ref, gt_ref, *, alpha):
    gb_, tl, d = x_ref.shape
    if gb_ > 1:
        parts = [(slice(0, gb_ // 2), slice(None)), (slice(gb_ // 2, gb_), slice(None))]
    else:
        parts = [(slice(None), slice(0, tl // 2)), (slice(None), slice(tl // 2, tl))]
    rows = gb_ * tl // 2

    def mixed_branches(part):
        flat = lambda ref: ref[part[0], part[1], :].reshape(rows, ref.shape[-1])
        y_ret = _dot(flat(yr_ref), wr_ref[...])
        y_fox = _dot(flat(yf_ref), wf_ref[...])
        return flat(ga_ref).astype(F32) * y_ret + flat(gb_ref).astype(F32) * y_fox

    def norm_and_route(n, part, mixed):
        bs, ls = part
        mod = lambda k: mod_ref[bs, :, k * d:(k + 1) * d]
        x = x_ref[bs, ls, :]
        x1 = _layer_norm(alpha * x + (1.0 + mod(2)) * mixed.reshape(x.shape), lw_ref[...], lb_ref[...])
        x1_ref[bs, ls, :] = x1
        h2 = x1 * (1.0 + mod(4)) + mod(3)
        h2_ref[bs, ls, :] = h2.astype(h2_ref.dtype)
        h_hi, h_mid, _ = _split3(h2.reshape(rows, d))
        w_hi, w_mid = rw_ref[0], rw_ref[1]
        lt = _dot_nt(w_hi, h_hi) + _dot_nt(w_hi, h_mid) + _dot_nt(w_mid, h_hi) + rb_ref[...]
        gl = lt[N_EXPERTS:N_EXPERTS + N_GROUPS, :]
        gmax, gi = _first_argmax_rows(gl, N_GROUPS)
        g_p = 1.0 / jnp.sum(jnp.exp(gl - gmax), axis=0, keepdims=True)
        e_sel = lt[0:EXPERTS_PER_GROUP, :]
        for g in range(1, N_GROUPS):
            e_sel = jnp.where(gi == g, lt[g * EXPERTS_PER_GROUP:(g + 1) * EXPERTS_PER_GROUP, :], e_sel)
        sub = lax.broadcasted_iota(jnp.int32, e_sel.shape, 0)
        m1, i1 = _first_argmax_rows(e_sel, EXPERTS_PER_GROUP)
        m2, i2 = _first_argmax_rows(jnp.where(sub == i1, -jnp.inf, e_sel), EXPERTS_PER_GROUP)
        r = jnp.exp(m2 - m1)
        gate0 = g_p / (1.0 + r)
        gate1 = g_p * r / (1.0 + r)
        cols = slice(n * rows, (n + 1) * rows)
        eid_ref[0:1, cols] = gi * EXPERTS_PER_GROUP + i1
        eid_ref[1:2, cols] = gi * EXPERTS_PER_GROUP + i2
        rr = lax.broadcasted_iota(jnp.int32, (LANES, rows), 0)
        gsq = jnp.where(rr == 0, gate0, jnp.where(rr == 1, gate1, 0.0))
        gt_ref[cols, :] = gsq.T

    mix = [mixed_branches(part) for part in parts]
    mixed = [_dot(mx.astype(BF16), wo_ref[...]) for mx in mix]
    for n, part in enumerate(parts):
        norm_and_route(n, part, mixed[n])


def _outproj(yr, yf, ga, gb, x, mod, wr, wf, wo, lw, lb, rw3, rb, gb_, tl, alpha):
    bx, l, d = x.shape
    m = bx * l
    tm = gb_ * tl
    nl = l // tl
    seq = lambda w: pl.BlockSpec((gb_, tl, w), lambda b, j: (b, j, 0))
    tok = lambda r: pl.BlockSpec((r, tm), lambda b, j: (0, b * nl + j))
    return pl.pallas_call(
        functools.partial(_outproj_kernel, alpha=alpha),
        out_shape=[jax.ShapeDtypeStruct((bx, l, d), F32), jax.ShapeDtypeStruct((bx, l, d), BF16),
                   jax.ShapeDtypeStruct((2, m), jnp.int32), jax.ShapeDtypeStruct((m, LANES), F32)],
        grid=(bx // gb_, nl),
        in_specs=[seq(RET_V_W), seq(FOX_W), seq(d), seq(d), seq(d),
                  pl.BlockSpec((gb_, 1, mod.shape[-1]), lambda b, j: (b, 0, 0)),
                  _resident(wr.shape), _resident(wf.shape), _resident(wo.shape),
                  _resident(lw.shape), _resident(lb.shape), _resident(rw3.shape), _resident(rb.shape)],
        out_specs=[seq(d), seq(d), tok(2), pl.BlockSpec((tm, LANES), lambda b, j: (b * nl + j, 0))],
        compiler_params=_params("arbitrary", "arbitrary"),
        name="outproj",
    )(yr, yf, ga, gb, x, mod, wr, wf, wo, lw, lb, rw3, rb)


MOE_CHUNK = 8


def _ceil_chunk(x):
    return jnp.floor((x + (MOE_CHUNK - 1.0)) * (1.0 / MOE_CHUNK)) * MOE_CHUNK


def _plan_kernel(eid_ref, triu_ref, ld_ref, pc_ref, pre_ref, carry_ref):
    @pl.when(pl.program_id(0) == 0)
    def _():
        carry_ref[...] = jnp.zeros_like(carry_ref)

    t = eid_ref.shape[1]
    experts = lax.broadcasted_iota(jnp.int32, (N_EXPERTS, t), 0)
    hit = [eid_ref[kk:kk + 1, :] == experts for kk in range(2)]
    onehot = [jnp.where(h, 1.0, 0.0) for h in hit]
    onehot_b = [o.astype(BF16) for o in onehot]
    ones = jnp.ones((8, t), BF16)
    cnt_row = (_dot_nt(ones, onehot_b[0]) + _dot_nt(ones, onehot_b[1]))[0:1, :]
    cnt0_col = jnp.sum(onehot[0], axis=1, keepdims=True)
    pc_row = _ceil_chunk(cnt_row)
    e_lane = lax.broadcasted_iota(jnp.int32, (N_EXPERTS, N_EXPERTS), 1)
    e_sub = lax.broadcasted_iota(jnp.int32, (N_EXPERTS, N_EXPERTS), 0)
    lstart_col = jnp.sum(jnp.where(e_lane < e_sub, pc_row, 0.0), axis=1, keepdims=True)
    for kk in range(2):
        before = _dot(onehot_b[kk], triu_ref[...])
        base = lstart_col + cnt0_col if kk == 1 else lstart_col
        row = jnp.sum(jnp.where(hit[kk], before + base, 0.0), axis=0, keepdims=True)
        ld_ref[kk:kk + 1, :] = row.astype(jnp.int32)
    pc_ref[0] = pc_row.astype(jnp.int32)
    pre_ref[0] = carry_ref[...].astype(jnp.int32)
    carry_ref[...] = carry_ref[...] + pc_row


def _plan(eid, t):
    m = eid.shape[1]
    n_tiles = m // t
    triu = jnp.asarray(np.triu(np.ones((t, t), np.float32), 1), BF16)
    per_tile = jax.ShapeDtypeStruct((n_tiles, 1, N_EXPERTS), jnp.int32)
    per_tile_spec = pl.BlockSpec((1, 1, N_EXPERTS), lambda i: (i, 0, 0))
    return pl.pallas_call(
        _plan_kernel,
        out_shape=[jax.ShapeDtypeStruct((2, m), jnp.int32), per_tile, per_tile],
        grid=(n_tiles,),
        in_specs=[pl.BlockSpec((2, t), lambda i: (0, i)), pl.BlockSpec((t, t), lambda i: (0, 0))],
        out_specs=[pl.BlockSpec((2, t), lambda i: (0, i)), per_tile_spec, per_tile_spec],
        scratch_shapes=[pltpu.VMEM((1, N_EXPERTS), F32)],
        compiler_params=_params("arbitrary"),
        name="moe_plan",
    )(eid, triu)


def _for_each_chunk(tile, pc_ref, goff_ref, fn):
    def per_expert(e, local):
        n = lax.div(pc_ref[tile * N_EXPERTS + e], jnp.int32(MOE_CHUNK))
        dst = goff_ref[tile * N_EXPERTS + e]

        def per_chunk(j, c):
            fn(pl.multiple_of(local + j * MOE_CHUNK, MOE_CHUNK), pl.multiple_of(dst + j * MOE_CHUNK, MOE_CHUNK))
            return c

        lax.fori_loop(0, n, per_chunk, 0)
        return local + n * MOE_CHUNK

    lax.fori_loop(0, N_EXPERTS, per_expert, 0)


def _one_hot_rows(ld_row, lp):
    rows = lax.broadcasted_iota(jnp.int32, (lp, ld_row.shape[1]), 0)
    return rows == ld_row


HIGH_HALF = -65536


def _pack_halves(x):
    half = x.shape[1] // 2
    lo = pltpu.bitcast(x[:, :half], jnp.int32)
    hi = pltpu.bitcast(x[:, half:], jnp.int32)
    return (hi & HIGH_HALF) | lax.shift_right_logical(lo, jnp.int32(16))


def _unpack_halves(w):
    lo = pltpu.bitcast(lax.shift_left(w, jnp.int32(16)), F32)
    hi = pltpu.bitcast(w & HIGH_HALF, F32)
    return lo.astype(BF16), hi.astype(BF16)


def _round_bf16(x):
    return x.astype(BF16).astype(F32)


def _dispatch_kernel(pc_ref, goff_ref, tail_ref, ld_ref, h_ref, xs_ref, buf_ref, zero_ref, sem, zero_sem):
    tile = pl.program_id(0)
    n_tiles = pl.num_programs(0)
    slot = tile % 2
    lp = buf_ref.shape[1]
    tb = zero_ref.shape[0]

    def copy(s, local, dst):
        return pltpu.make_async_copy(buf_ref.at[s, pl.ds(local, MOE_CHUNK)], xs_ref.at[pl.ds(dst, MOE_CHUNK)],
                                     sem.at[s])

    def fill_chunk(dst):
        return pltpu.make_async_copy(zero_ref.at[pl.ds(0, MOE_CHUNK)], xs_ref.at[pl.ds(dst, MOE_CHUNK)], zero_sem)

    def fill_block(dst):
        return pltpu.make_async_copy(zero_ref, xs_ref.at[pl.ds(dst, tb)], zero_sem)

    def for_each_fill(on_chunk, on_block):
        def per_expert(e, c):
            first = tail_ref[e]

            def per_chunk(j, cc):
                on_chunk(pl.multiple_of(first + j * MOE_CHUNK, MOE_CHUNK))
                return cc

            lax.fori_loop(0, tail_ref[N_EXPERTS + e], per_chunk, 0)
            return c

        lax.fori_loop(0, N_EXPERTS, per_expert, 0)
        first = tail_ref[2 * N_EXPERTS]

        def per_block(j, c):
            on_block(pl.multiple_of(first + j * tb, tb))
            return c

        lax.fori_loop(0, tail_ref[2 * N_EXPERTS + 1], per_block, 0)

    perm = jnp.where(_one_hot_rows(ld_ref[0:1, :], lp), 1.0,
                     jnp.where(_one_hot_rows(ld_ref[1:2, :], lp), 1.0, 0.0)).astype(BF16)
    buf_ref[slot] = _pack_halves(_dot(perm, h_ref[...]))
    _for_each_chunk(tile, pc_ref, goff_ref, lambda a, b: copy(slot, a, b).start())

    @pl.when(tile == 0)
    def _():
        zero_ref[...] = jnp.zeros_like(zero_ref)
        for_each_fill(lambda d: fill_chunk(d).start(), lambda d: fill_block(d).start())
        for_each_fill(lambda d: fill_chunk(d).wait(), lambda d: fill_block(d).wait())

    @pl.when(tile > 0)
    def _():
        _for_each_chunk(tile - 1, pc_ref, goff_ref, lambda a, b: copy(1 - slot, a, b).wait())

    @pl.when(tile == n_tiles - 1)
    def _():
        _for_each_chunk(tile, pc_ref, goff_ref, lambda a, b: copy(slot, a, b).wait())


def _dispatch(pc, goff, tails, ld, h2, n_slots, t, lp, tb):
    m, d = h2.shape
    return pl.pallas_call(
        _dispatch_kernel,
        out_shape=jax.ShapeDtypeStruct((n_slots, d // 2), jnp.int32),
        grid_spec=pltpu.PrefetchScalarGridSpec(
            num_scalar_prefetch=3,
            grid=(m // t,),
            in_specs=[pl.BlockSpec((2, t), lambda i, *_: (0, i)),
                      pl.BlockSpec((t, d), lambda i, *_: (i, 0))],
            out_specs=pl.BlockSpec(memory_space=pl.ANY),
            scratch_shapes=[pltpu.VMEM((2, lp, d // 2), jnp.int32), pltpu.VMEM((tb, d // 2), jnp.int32),
                            pltpu.SemaphoreType.DMA((2,)), pltpu.SemaphoreType.DMA]),
        compiler_params=_params("arbitrary"),
        name="moe_dispatch",
    )(pc, goff, tails, ld, h2)


def _expert_kernel(be_ref, nv_ref, x_ref, w1_ref, w2_ref, y_ref):
    del be_ref

    @pl.when(pl.program_id(0) < nv_ref[0])
    def _():
        e = w2_ref.shape[1]
        half = x_ref.shape[1]
        x_lo, x_hi = _unpack_halves(x_ref[...])
        au = _dot(x_lo, w1_ref[0, :half, :]) + _dot(x_hi, w1_ref[0, half:, :])
        a = au[:, :e]
        u = au[:, e:]
        y = _dot((a * _sigmoid(a) * u).astype(BF16), w2_ref[0])
        y_ref[...] = _pack_halves(_round_bf16(y))

    @pl.when(pl.program_id(0) >= nv_ref[0])
    def _():
        y_ref[...] = jnp.zeros_like(y_ref)


def _experts(block_e, n_valid, xs, w1, w2, tb):
    p, half = xs.shape
    d = 2 * half
    e = w2.shape[1]
    rows = lambda i, be, nv: (jnp.minimum(i, nv[0] - 1), 0)
    return pl.pallas_call(
        _expert_kernel,
        out_shape=jax.ShapeDtypeStruct((p, half), jnp.int32),
        grid_spec=pltpu.PrefetchScalarGridSpec(
            num_scalar_prefetch=2,
            grid=(p // tb,),
            in_specs=[pl.BlockSpec((tb, half), rows),
                      pl.BlockSpec((1, d, 2 * e), lambda i, be, nv: (be[i], 0, 0)),
                      pl.BlockSpec((1, e, d), lambda i, be, nv: (be[i], 0, 0))],
            out_specs=pl.BlockSpec((tb, half), lambda i, be, nv: (i, 0))),
        compiler_params=_params("arbitrary"),
        name="moe_experts",
    )(block_e, n_valid, xs, w1, w2)


def _combine_kernel(pc_ref, goff_ref, ld_ref, y_hbm, gt_ref, x1_ref, mod_ref, lw_ref, lb_ref, o_ref,
                    buf_ref, sem, *, alpha):
    gb_, tl, d = x1_ref.shape
    nl = pl.num_programs(1)
    n_tiles = pl.num_programs(0) * nl
    tile = pl.program_id(0) * nl + pl.program_id(1)
    slot = tile % 2
    lp = buf_ref.shape[1]

    def copy(s, local, src):
        return pltpu.make_async_copy(y_hbm.at[pl.ds(src, MOE_CHUNK)], buf_ref.at[s, pl.ds(local, MOE_CHUNK)],
                                     sem.at[s])

    @pl.when(tile == 0)
    def _():
        buf_ref[...] = jnp.zeros_like(buf_ref)
        _for_each_chunk(tile, pc_ref, goff_ref, lambda a, b: copy(slot, a, b).start())

    @pl.when(tile + 1 < n_tiles)
    def _():
        _for_each_chunk(tile + 1, pc_ref, goff_ref, lambda a, b: copy(1 - slot, a, b).start())

    _for_each_chunk(tile, pc_ref, goff_ref, lambda a, b: copy(slot, a, b).wait())
    y_lo, y_hi = _unpack_halves(buf_ref[slot])
    picked = []
    for kk in range(2):
        unsort = jnp.where(_one_hot_rows(ld_ref[kk:kk + 1, :], lp), 1.0, 0.0).astype(BF16)
        picked.append(jnp.concatenate([_dot_tn(unsort, y_lo), _dot_tn(unsort, y_hi)], axis=1))
    y = gt_ref[:, 0:1] * picked[0] + gt_ref[:, 1:2] * picked[1]
    g2 = mod_ref[:, :, 5 * d:6 * d]
    o_ref[...] = _layer_norm(alpha * x1_ref[...] + (1.0 + g2) * y.reshape(gb_, tl, d), lw_ref[...], lb_ref[...])


def _combine(pc, goff, ld, y, gt, x1, mod, lw, lb, gb_, tl, alpha, lp):
    bx, l, d = x1.shape
    tm = gb_ * tl
    nl = l // tl
    seq = pl.BlockSpec((gb_, tl, d), lambda b, j, pc, go: (b, j, 0))
    fixed = lambda shape: pl.BlockSpec(shape, lambda b, j, pc, go: (0,) * len(shape), pipeline_mode=pl.Buffered(1))
    return pl.pallas_call(
        functools.partial(_combine_kernel, alpha=alpha),
        out_shape=jax.ShapeDtypeStruct((bx, l, d), F32),
        grid_spec=pltpu.PrefetchScalarGridSpec(
            num_scalar_prefetch=2,
            grid=(bx // gb_, nl),
            in_specs=[pl.BlockSpec((2, tm), lambda b, j, pc, go: (0, b * nl + j)),
                      pl.BlockSpec(memory_space=pl.ANY),
                      pl.BlockSpec((tm, LANES), lambda b, j, pc, go: (b * nl + j, 0)),
                      seq,
                      pl.BlockSpec((gb_, 1, mod.shape[-1]), lambda b, j, pc, go: (b, 0, 0)),
                      fixed(lw.shape), fixed(lb.shape)],
            out_specs=seq,
            scratch_shapes=[pltpu.VMEM((2, lp, d // 2), jnp.int32), pltpu.SemaphoreType.DMA((2,))]),
        compiler_params=_params("arbitrary", "arbitrary"),
        name="moe_combine",
    )(pc, goff, ld, y, gt, x1, mod, lw, lb)


def _moe(h2, eid, gt, x1, mod, w1, w2, lw, lb, gb_, tl, alpha, tb):
    bx, l, d = x1.shape
    m = bx * l
    t = gb_ * tl
    n_tiles = m // t
    lp = 2 * t + N_EXPERTS * MOE_CHUNK
    ld, pc, pre = _plan(eid, t)
    pc = pc.reshape(n_tiles, N_EXPERTS)
    pre = pre.reshape(n_tiles, N_EXPERTS)
    total = pre[-1] + pc[-1]
    region = (total + tb - 1) // tb * tb
    gend = jnp.cumsum(region)
    goff = (gend - region)[None, :] + pre
    n_blocks = -(-(2 * m + n_tiles * N_EXPERTS * (MOE_CHUNK - 1) + N_EXPERTS * (tb - 1)) // tb)
    block_row0 = jnp.arange(n_blocks, dtype=jnp.int32) * tb
    block_e = jnp.minimum(jnp.sum((gend[None, :] <= block_row0[:, None]).astype(jnp.int32), axis=1),
                          N_EXPERTS - 1).astype(jnp.int32)
    n_valid = (gend[-1:] // tb).astype(jnp.int32)
    tails = jnp.concatenate([gend - region + total, (region - total) // MOE_CHUNK,
                             gend[-1:], n_blocks - gend[-1:] // tb]).astype(jnp.int32)
    pc = pc.reshape(-1).astype(jnp.int32)
    goff = goff.reshape(-1).astype(jnp.int32)
    xs = _dispatch(pc, goff, tails, ld, h2.reshape(m, d), n_blocks * tb, t, lp, tb)
    y = _experts(block_e, n_valid, xs, w1, w2, tb)
    return _combine(pc, goff, ld, y, gt, x1, mod, lw, lb, gb_, tl, alpha, lp)


def _rotary_tables(pos, reps):
    half = RET_QK_DIM // 2
    inv = ROPE_BASE ** (-jnp.linspace(0.0, 1.0, half, dtype=F32))
    ang = pos.astype(F32)[:, None] * inv[None, :]
    cos = jnp.cos(ang)
    sin = jnp.sin(ang)
    c2 = jnp.concatenate([cos, cos], axis=1)
    s2 = jnp.concatenate([-sin, sin], axis=1)
    kscale = RET_QK_DIM ** -0.5
    tabs = (c2, s2, c2 * kscale, s2 * kscale)
    return tuple(jnp.tile(t, (reps, 1)) for t in tabs)


def _pick(n, pref):
    t = min(n, pref)
    while n % t:
        t //= 2
    return t


def _layer(x, mod, pos, weights, log_gamma, alpha, cache=None):
    (w2, b2, gn_w, gn_b, wr, wf, wo, ln1w, ln1b, rw3, rb, we1, we2, ln2w, ln2b) = weights
    bx, l, d = x.shape
    if cache is None:
        gb_, tl = 1, _pick(l, 512)
    else:
        gb_, tl = _pick(bx, max(1, 512 // l)), l
    tabs = _rotary_tables(pos, gb_ if cache is not None else 1)
    rq, rk, rv, rg, fq, fk, fv, fkb, fvb, lf, ga, gb = _inproj(x, mod, w2, b2, tabs, gb_, tl)

    lft = jnp.swapaxes(lf[:, :, :FOX_HEADS], 1, 2)
    if cache is None:
        lc = _pick(l, 256)
        y_ret, state = _retention(rq, rk, rv, rg, gn_w, gn_b, log_gamma, lc)
        fp = _fcum(lft, _pick(l, 512))
        y_fox = _fox_prompt(fq, fkb, fp, fvb, _pick(l, 512))
    else:
        state0, cache_k, cache_v, cache_logf = cache
        y_ret, state = _retention(rq, rk, rv, rg, gn_w, gn_b, log_gamma, l, state0)
        past = cache_k.shape[1]
        total = -(-(past + l) // LANES) * LANES
        lft_all = jnp.concatenate(
            [jnp.swapaxes(cache_logf.astype(F32), 1, 2), lft,
             jnp.zeros((bx, FOX_HEADS, total - past - l), F32)], axis=2)
        fp = _fcum(lft_all, total)
        y_fox = _fox_sample(fq, fkb, fvb, cache_k.reshape(bx, past, FOX_W), cache_v.reshape(bx, past, FOX_W), fp)

    x1, h2, eid, gt = _outproj(y_ret, y_fox, ga, gb, x, mod, wr, wf, wo, ln1w, ln1b, rw3, rb, gb_, tl, alpha)
    m = bx * l
    out = _moe(h2, eid, gt, x1, mod, we1, we2, ln2w, ln2b, gb_, tl, alpha, tb=256)
    return out, state, fk, fv, lf[:, :, :FOX_HEADS]


def kernel(x_prompt, x_sample, state_ret, cache_fox_k, cache_fox_v, cache_fox_logf, c_prompt, c_sample,
           w_ada, b_ada, w_in, b_in, ret_gn_w, ret_gn_b, w_ret_proj, w_fox_proj, w_o, ln1_w, ln1_b,
           w_rg, b_rg, w_re, b_re, w_e_in, w_e_out, ln2_w, ln2_b):
    depth = w_ada.shape[0]
    d = x_prompt.shape[-1]
    bp, s, _ = x_prompt.shape
    bs, ls, _ = x_sample.shape
    past = cache_fox_k.shape[2]
    alpha = (2 * depth) ** 0.25
    log_gamma = jnp.log1p(-jnp.exp(jnp.linspace(math.log(1.0 / 32), math.log(1.0 / 512), RET_HEADS, dtype=F32)))
    pos_p = jnp.arange(s, dtype=jnp.int32)
    pos_s = past + jnp.arange(ls, dtype=jnp.int32)
    fg = 2 * RET_QK_W + 2 * RET_V_W + 3 * FOX_W

    xp, xs = x_prompt, x_sample
    p_ret, p_k, p_v, p_f, s_ret, s_k, s_v, s_f = [], [], [], [], [], [], [], []
    for li in range(depth):
        w2 = jnp.concatenate([w_in[li][:, :fg], w_in[li][:, fg + FOX_HEADS:],
                              jnp.pad(w_in[li][:, fg:fg + FOX_HEADS], ((0, 0), (0, LANES - FOX_HEADS)))],
                             axis=1).astype(BF16)
        b2 = jnp.concatenate([b_in[li][:fg], b_in[li][fg + FOX_HEADS:],
                              jnp.pad(b_in[li][fg:fg + FOX_HEADS], (0, LANES - FOX_HEADS))]).reshape(1, -1)
        n_rt = N_EXPERTS + N_GROUPS
        rt_rows = -(-n_rt // 8) * 8
        rwt = jnp.pad(jnp.concatenate([w_re[li], w_rg[li]], axis=1).T.astype(F32), ((0, rt_rows - n_rt), (0, 0)))
        r_hi = rwt.astype(BF16)
        r_mid = (rwt - r_hi.astype(F32)).astype(BF16)
        rw3 = jnp.stack([r_hi, r_mid])
        rb = jnp.pad(jnp.concatenate([b_re[li], b_rg[li]]).astype(F32), (0, rt_rows - n_rt)).reshape(rt_rows, 1)
        weights = (w2, b2, ret_gn_w[li], ret_gn_b[li],
                   w_ret_proj[li].astype(BF16), w_fox_proj[li].astype(BF16), w_o[li].astype(BF16),
                   ln1_w[li].reshape(1, d), ln1_b[li].reshape(1, d), rw3, rb,
                   w_e_in[li].astype(BF16), w_e_out[li].astype(BF16),
                   ln2_w[li].reshape(1, d), ln2_b[li].reshape(1, d))
        mod = _ada(jnp.concatenate([c_prompt, c_sample], axis=0), w_ada[li], b_ada[li])
        mod_p = mod[:bp].reshape(bp, 1, 6 * d)
        mod_s = mod[bp:].reshape(bs, 1, 6 * d)
        xp, st, kk, vv, ff = _layer(xp, mod_p, pos_p, weights, log_gamma, alpha)
        p_ret.append(st); p_k.append(kk.reshape(bp, s, FOX_HEADS, FOX_HEAD_DIM))
        p_v.append(vv.reshape(bp, s, FOX_HEADS, FOX_HEAD_DIM)); p_f.append(ff)
        cache = (state_ret[li].astype(F32), cache_fox_k[li], cache_fox_v[li], cache_fox_logf[li])
        xs, st, kk, vv, ff = _layer(xs, mod_s, pos_s, weights, log_gamma, alpha, cache)
        s_ret.append(st); s_k.append(kk.reshape(bs, ls, FOX_HEADS, FOX_HEAD_DIM))
        s_v.append(vv.reshape(bs, ls, FOX_HEADS, FOX_HEAD_DIM)); s_f.append(ff)
    return (xp, xs, jnp.stack(p_ret), jnp.stack(p_k), jnp.stack(p_v), jnp.stack(p_f),
            jnp.stack(s_ret), jnp.stack(s_k), jnp.stack(s_v), jnp.stack(s_f))
```

```python
import functools
import math

import jax
import jax.numpy as jnp
import numpy as np
from jax import lax
from jax.experimental import pallas as pl
from jax.experimental.pallas import tpu as pltpu

F32 = jnp.float32
BF16 = jnp.bfloat16

RET_HEADS = 4
RET_QK_DIM = 128
RET_V_DIM = 256
ROPE_BASE = 10000.0
FOX_HEADS = 16
FOX_HEAD_DIM = 64
N_GROUPS = 4
EXPERTS_PER_GROUP = 8
N_EXPERTS = N_GROUPS * EXPERTS_PER_GROUP
LN_EPS = 1e-5
GN_EPS = 1e-6
RET_QK_W = RET_HEADS * RET_QK_DIM
RET_V_W = RET_HEADS * RET_V_DIM
FOX_W = FOX_HEADS * FOX_HEAD_DIM
FOX_PAIRS = FOX_HEADS // 2
LOG2_E = math.log2(math.e)
FOX_Q_SCALE = FOX_HEAD_DIM ** -0.5 * LOG2_E

LANES = 128
VMEM_LIMIT_BYTES = 56 * 1024 * 1024
MASK_VALUE = -1e30


def _dot(a, b):
    return jnp.dot(a, b, preferred_element_type=F32)


def _dot_nt(a, b):
    return lax.dot_general(a, b, (((1,), (1,)), ((), ())), preferred_element_type=F32)


def _dot_tn(a, b):
    return lax.dot_general(a, b, (((0,), (0,)), ((), ())), preferred_element_type=F32)


def _split3(x):
    hi = x.astype(BF16)
    r1 = x - hi.astype(F32)
    mid = r1.astype(BF16)
    lo = (r1 - mid.astype(F32)).astype(BF16)
    return hi, mid, lo


def _sigmoid(x):
    return 1.0 / (1.0 + jnp.exp(-x))


def _log_sigmoid(x):
    return -(jnp.maximum(-x, 0.0) + jnp.log1p(jnp.exp(-jnp.abs(x))))


def _params(*sem):
    return pltpu.CompilerParams(dimension_semantics=sem, vmem_limit_bytes=VMEM_LIMIT_BYTES)


def _resident(shape):
    nd = len(shape)
    return pl.BlockSpec(shape, lambda *_: (0,) * nd, pipeline_mode=pl.Buffered(1))


def _ada_kernel(c_ref, w_ref, b_ref, o_ref):
    c = c_ref[...]
    s = (c * _sigmoid(c)).astype(BF16)
    o_ref[...] = _dot(s, w_ref[...].astype(BF16)) + b_ref[...]


def _ada(c, w, b):
    n, d = c.shape
    nout = w.shape[1]
    tn = d
    return pl.pallas_call(
        _ada_kernel,
        out_shape=jax.ShapeDtypeStruct((n, nout), F32),
        grid=(nout // tn,),
        in_specs=[pl.BlockSpec((n, d), lambda j: (0, 0)),
                  pl.BlockSpec((d, tn), lambda j: (0, j)),
                  pl.BlockSpec((1, tn), lambda j: (0, j))],
        out_specs=pl.BlockSpec((n, tn), lambda j: (0, j)),
        compiler_params=_params("arbitrary"),
        name="ada",
    )(c, w, b.reshape(1, nout))


def _inproj_kernel(x_ref, mod_ref, w_ref, b_ref, cq_ref, sq_ref, ck_ref, sk_ref,
                   rq_ref, rk_ref, rv_ref, rg_ref, fq_ref, fk_ref, fv_ref, fkb_ref, fvb_ref,
                   lf_ref, ga_ref, gb_ref, *, transposed_v):
    gb_, tl, d = x_ref.shape
    tm = gb_ * tl
    sh = mod_ref[:, :, 0:d]
    sc = mod_ref[:, :, d:2 * d]
    h = (x_ref[...] * (1.0 + sc) + sh).reshape(tm, d).astype(BF16)

    def proj(lo, width):
        return _dot(h, w_ref[:, lo:lo + width]) + b_ref[:, lo:lo + width]

    def put(ref, val):
        ref[...] = val.reshape(ref.shape).astype(ref.dtype)

    def rot(z, c_ref, s_ref):
        c = c_ref[...]
        s = s_ref[...]
        parts = []
        for hh in range(RET_HEADS):
            zh = z[:, hh * RET_QK_DIM:(hh + 1) * RET_QK_DIM]
            parts.append(zh * c + pltpu.roll(zh, RET_QK_DIM // 2, axis=1) * s)
        return jnp.concatenate(parts, axis=1)

    off = 0
    put(rq_ref, rot(proj(off, RET_QK_W), cq_ref, sq_ref)); off += RET_QK_W
    put(rk_ref, rot(proj(off, RET_QK_W), ck_ref, sk_ref)); off += RET_QK_W
    put(rv_ref, proj(off, RET_V_W)); off += RET_V_W
    z = proj(off, RET_V_W); off += RET_V_W
    put(rg_ref, z * _sigmoid(z))
    put(fq_ref, proj(off, FOX_W) * FOX_Q_SCALE); off += FOX_W
    z = proj(off, FOX_W); off += FOX_W
    put(fk_ref, z); put(fkb_ref, z)
    z = proj(off, FOX_W); off += FOX_W
    put(fv_ref, z)
    if transposed_v:
        fvb_ref[0] = z.T.astype(fvb_ref.dtype)
    else:
        put(fvb_ref, z)
    put(ga_ref, _sigmoid(proj(off, d))); off += d
    put(gb_ref, _sigmoid(proj(off, d))); off += d
    put(lf_ref, _log_sigmoid(proj(off, LANES)))


def _inproj(x, mod, w2, b2, tabs, gb_, tl):
    bx, l, d = x.shape
    tm = gb_ * tl
    nw = w2.shape[1]
    grid = (bx // gb_, l // tl)
    if gb_ == 1:
        tab_spec = pl.BlockSpec((tl, LANES), lambda b, j: (j, 0))
    else:
        tab_spec = pl.BlockSpec((tm, LANES), lambda b, j: (0, 0))

    def out(width, dtype):
        return (jax.ShapeDtypeStruct((bx, l, width), dtype),
                pl.BlockSpec((gb_, tl, width), lambda b, j: (b, j, 0)))

    transposed_v = gb_ == 1
    if transposed_v:
        fvb = (jax.ShapeDtypeStruct((bx, FOX_W, l), BF16), pl.BlockSpec((1, FOX_W, tl), lambda b, j: (b, 0, j)))
    else:
        fvb = out(FOX_W, BF16)
    outs = [out(RET_QK_W, BF16), out(RET_QK_W, BF16), out(RET_V_W, BF16), out(RET_V_W, BF16),
            out(FOX_W, BF16), out(FOX_W, F32), out(FOX_W, F32), out(FOX_W, BF16), fvb,
            out(LANES, F32), out(d, BF16), out(d, BF16)]
    return pl.pallas_call(
        functools.partial(_inproj_kernel, transposed_v=transposed_v),
        out_shape=[o[0] for o in outs],
        grid=grid,
        in_specs=[pl.BlockSpec((gb_, tl, d), lambda b, j: (b, j, 0)),
                  pl.BlockSpec((gb_, 1, mod.shape[-1]), lambda b, j: (b, 0, 0)),
                  _resident((d, nw)), _resident((1, nw)),
                  tab_spec, tab_spec, tab_spec, tab_spec],
        out_specs=[o[1] for o in outs],
        compiler_params=_params("arbitrary", "arbitrary"),
        name="inproj",
    )(x, mod, w2, b2, *tabs)


BIAS_PIECES = 3


def _fcum_kernel(lft_ref, triu_ref, fp_ref, carry_ref):
    @pl.when(pl.program_id(1) == 0)
    def _():
        carry_ref[...] = jnp.zeros_like(carry_ref)

    tl = lft_ref.shape[2]
    hi, mid, lo = _split3(lft_ref[0])
    triu = triu_ref[...]
    cum = _dot(hi, triu) + _dot(mid, triu) + _dot(lo, triu) + carry_ref[...]
    carry_ref[...] = cum[:, tl - 1:tl]
    padded = jnp.concatenate([cum * -LOG2_E, jnp.zeros((LANES - FOX_HEADS, tl), F32)], axis=0)
    pieces = _split3(padded.T)
    out = pieces[0].astype(F32)
    for p in range(1, BIAS_PIECES):
        out = out + pltpu.roll(pieces[p].astype(F32), p * FOX_HEADS, axis=1)
    fp_ref[0] = out.astype(fp_ref.dtype)


def _fcum(lft, tl):
    bx, _, l = lft.shape
    triu = jnp.asarray(np.triu(np.ones((tl, tl), np.float32)), BF16)
    return pl.pallas_call(
        _fcum_kernel,
        out_shape=jax.ShapeDtypeStruct((bx, l, LANES), BF16),
        grid=(bx, l // tl),
        in_specs=[pl.BlockSpec((1, FOX_HEADS, tl), lambda b, j: (b, 0, j)),
                  pl.BlockSpec((tl, tl), lambda b, j: (0, 0))],
        out_specs=pl.BlockSpec((1, tl, LANES), lambda b, j: (b, j, 0)),
        scratch_shapes=[pltpu.VMEM((FOX_HEADS, 1), F32)],
        compiler_params=_params("arbitrary", "arbitrary"),
        name="fcum",
    )(lft, triu)


def _retention_kernel(*refs, has_state):
    if has_state:
        (q_ref, k_ref, v_ref, g_ref, dec_ref, qd_ref, kd_ref, gw_ref, gb_ref, s0_ref,
         y_ref, st_ref) = refs
    else:
        (q_ref, k_ref, v_ref, g_ref, dec_ref, qd_ref, kd_ref, gw_ref, gb_ref,
         y_ref, st_ref) = refs
        s0_ref = None
    lc = q_ref.shape[1]

    @pl.when(pl.program_id(1) == 0)
    def _():
        if has_state:
            st_ref[...] = s0_ref[...]
        else:
            st_ref[...] = jnp.zeros_like(st_ref)

    for hh in range(RET_HEADS):
        qs = slice(hh * RET_QK_DIM, (hh + 1) * RET_QK_DIM)
        vs = slice(hh * RET_V_DIM, (hh + 1) * RET_V_DIM)
        q = q_ref[0, :, qs]
        k = k_ref[0, :, qs]
        v = v_ref[0, :, vs]
        state = st_ref[0, hh]
        qd = qd_ref[hh]
        scores = _dot_nt(q, k) * dec_ref[hh]
        inner = _dot(scores.astype(BF16), v)
        cross = _dot(q, state.astype(BF16)) * jnp.concatenate([qd] * (RET_V_DIM // LANES), axis=1)
        o = inner + cross
        kdec = (k.astype(F32) * kd_ref[hh]).astype(BF16)
        st_ref[0, hh] = qd[lc - 1:lc, 0:1] * state + _dot_tn(kdec, v)
        mu = jnp.mean(o, axis=-1, keepdims=True)
        oc = o - mu
        var = jnp.mean(oc * oc, axis=-1, keepdims=True)
        on = oc * lax.rsqrt(var + GN_EPS) * gw_ref[:, vs] + gb_ref[:, vs]
        y_ref[0, :, vs] = (g_ref[0, :, vs].astype(F32) * on).astype(y_ref.dtype)


def _retention_tables(log_gamma, lc):
    n = jnp.arange(lc, dtype=F32)
    diff = n[:, None] - n[None, :]
    decay = jnp.where(diff[None] >= 0, jnp.exp(jnp.maximum(diff, 0.0)[None] * log_gamma[:, None, None]), 0.0)
    qdec = jnp.exp((n + 1.0)[None, :, None] * log_gamma[:, None, None])
    kdec = jnp.exp((lc - 1.0 - n)[None, :, None] * log_gamma[:, None, None])
    lanes = (RET_HEADS, lc, LANES)
    return decay, jnp.broadcast_to(qdec, lanes), jnp.broadcast_to(kdec, lanes)


def _retention(q, k, v, g, gn_w, gn_b, log_gamma, lc, state0=None):
    bx, l, _ = q.shape
    decay, qdec, kdec = _retention_tables(log_gamma, lc)
    has_state = state0 is not None
    seq = lambda w: pl.BlockSpec((1, lc, w), lambda b, c: (b, c, 0))
    whole = lambda a: pl.BlockSpec(a.shape, lambda b, c: (0,) * a.ndim)
    st_spec = pl.BlockSpec((1, RET_HEADS, RET_QK_DIM, RET_V_DIM), lambda b, c: (b, 0, 0, 0))
    gw = gn_w.reshape(1, RET_V_W)
    gb = gn_b.reshape(1, RET_V_W)
    args = [q, k, v, g, decay, qdec, kdec, gw, gb]
    in_specs = [seq(RET_QK_W), seq(RET_QK_W), seq(RET_V_W), seq(RET_V_W),
                whole(decay), whole(qdec), whole(kdec), whole(gw), whole(gb)]
    if has_state:
        args.append(state0)
        in_specs.append(st_spec)
    return pl.pallas_call(
        functools.partial(_retention_kernel, has_state=has_state),
        out_shape=[jax.ShapeDtypeStruct((bx, l, RET_V_W), BF16),
                   jax.ShapeDtypeStruct((bx, RET_HEADS, RET_QK_DIM, RET_V_DIM), F32)],
        grid=(bx, l // lc),
        in_specs=in_specs,
        out_specs=[seq(RET_V_W), st_spec],
        compiler_params=_params("arbitrary", "arbitrary"),
        name="retention",
    )(*args)


def _pair_queries(q2, pair):
    t = q2.shape[0]
    lane = lax.broadcasted_iota(jnp.int32, (t, LANES), 1)
    out = []
    for i in range(2):
        head = (lane >= i * FOX_HEAD_DIM) & (lane < (i + 1) * FOX_HEAD_DIM)
        offset = lane - (2 * pair + i)
        ones = (offset >= 0) & (offset < BIAS_PIECES * FOX_HEADS) & ((offset & (FOX_HEADS - 1)) == 0)
        out.append(jnp.concatenate([jnp.where(head, q2, jnp.zeros_like(q2)),
                                    jnp.where(ones, 1.0, 0.0).astype(q2.dtype)], axis=1))
    return out


def _fox_prompt_kernel(q_ref, k_ref, fp_ref, vt_ref, o_ref, m_ref, l_ref, acc_ref, qc_ref, sa_ref, sb_ref, *, t):
    nq = q_ref.shape[1] // t
    steps = [(qi, ki) for qi in range(nq) for ki in range(qi + 1)]
    bufs = (sa_ref, sb_ref)
    half = t // 2

    def tiles(qi, ki):
        if ki == qi:
            return [((0, half), (0, t)), ((half, t), (half, t))]
        return [((0, t), (0, t))]

    def produce(n, i):
        qi, ki = steps[n]
        if ki == 0 and i == 0:
            qc = _pair_queries(q_ref[0, qi * t:(qi + 1) * t, :], pl.program_id(1))
            qc_ref[0] = qc[0]
            qc_ref[1] = qc[1]
        for (k0, k1), (q0, q1) in tiles(qi, ki):
            keys = slice(ki * t + k0, ki * t + k1)
            kc = jnp.concatenate([k_ref[0, keys, :], fp_ref[0, keys, :]], axis=1)
            bufs[n % 2][i, k0:k1, q0:q1] = _dot_nt(kc, qc_ref[i, q0:q1, :])

    def consume(n, i):
        qi, ki = steps[n]
        rows = slice(i * FOX_HEAD_DIM, (i + 1) * FOX_HEAD_DIM)
        for tile_no, ((k0, k1), (q0, q1)) in enumerate(tiles(qi, ki)):
            first = ki == 0 and tile_no == 0
            s = bufs[n % 2][i, k0:k1, q0:q1]
            if ki == qi:
                key = lax.broadcasted_iota(jnp.int32, s.shape, 0) + k0
                qry = lax.broadcasted_iota(jnp.int32, s.shape, 1) + q0
                s = jnp.where(qry >= key, s, MASK_VALUE)
            smax = jnp.max(s, axis=0, keepdims=True)
            m_new = smax if first else jnp.maximum(m_ref[i, :, q0:q1], smax)
            p = jnp.exp2(s - m_new)
            psum = jnp.sum(p, axis=0, keepdims=True)
            pv = _dot(vt_ref[0, rows, ki * t + k0:ki * t + k1], p.astype(BF16))
            if first:
                l_ref[i, :, q0:q1] = psum
                acc_ref[rows, q0:q1] = pv
            else:
                alpha = jnp.exp2(m_ref[i, :, q0:q1] - m_new)
                l_ref[i, :, q0:q1] = alpha * l_ref[i, :, q0:q1] + psum
                acc_ref[rows, q0:q1] = acc_ref[rows, q0:q1] * alpha + pv
            m_ref[i, :, q0:q1] = m_new
        if ki == qi and i == 1:
            out_t = jnp.concatenate(
                [acc_ref[h * FOX_HEAD_DIM:(h + 1) * FOX_HEAD_DIM, :] * (1.0 / l_ref[h]) for h in range(2)], axis=0)
            o_ref[0, qi * t:(qi + 1) * t, :] = out_t.T.astype(o_ref.dtype)

    for i in range(2):
        produce(0, i)
    for n in range(len(steps)):
        for i in range(2):
            if n + 1 < len(steps):
                produce(n + 1, i)
            consume(n, i)


def _fox_prompt(q, k, fp, vt, t):
    b, s, _ = q.shape
    seq = pl.BlockSpec((1, s, LANES), lambda bi, j: (bi, 0, j))
    return pl.pallas_call(
        functools.partial(_fox_prompt_kernel, t=t),
        out_shape=jax.ShapeDtypeStruct((b, s, FOX_W), BF16),
        grid=(b, FOX_PAIRS),
        in_specs=[seq, seq, pl.BlockSpec((1, s, LANES), lambda bi, j: (bi, 0, 0)),
                  pl.BlockSpec((1, LANES, s), lambda bi, j: (bi, j, 0))],
        out_specs=seq,
        scratch_shapes=[pltpu.VMEM((2, 1, t), F32), pltpu.VMEM((2, 1, t), F32),
                        pltpu.VMEM((LANES, t), F32), pltpu.VMEM((2, t, 2 * LANES), BF16),
                        pltpu.VMEM((2, t, t), F32), pltpu.VMEM((2, t, t), F32)],
        compiler_params=_params("arbitrary", "arbitrary"),
        name="fox_prompt",
    )(q, k, fp, vt)


def _fox_sample_kernel(q_ref, kc_ref, vc_ref, kn_ref, vn_ref, fp_ref, o_ref):
    l = q_ref.shape[1]
    past = kc_ref.shape[1]
    lane = lax.broadcasted_iota(jnp.int32, (1, LANES), 1)
    head_mask = (lane < FOX_HEAD_DIM, lane >= FOX_HEAD_DIM)
    row = lax.broadcasted_iota(jnp.int32, (l, l), 0)
    col = lax.broadcasted_iota(jnp.int32, (l, l), 1)
    fp_old = fp_ref[0, 0:past, :]
    fp_new = fp_ref[0, past:past + l, :]
    for pair in range(FOX_PAIRS):
        lanes = slice(pair * LANES, (pair + 1) * LANES)
        qc = _pair_queries(q_ref[0, :, lanes], pair)
        kc = jnp.concatenate([kc_ref[0, :, lanes].astype(BF16), fp_old], axis=1)
        kn = jnp.concatenate([kn_ref[0, :, lanes], fp_new], axis=1)
        vc = vc_ref[0, :, lanes].astype(BF16)
        vn = vn_ref[0, :, lanes]
        out = jnp.zeros((l, LANES), F32)
        for i in range(2):
            s_c = _dot_nt(qc[i], kc)
            s_n = jnp.where(row >= col, _dot_nt(qc[i], kn), MASK_VALUE)
            m = jnp.maximum(jnp.max(s_c, axis=-1, keepdims=True), jnp.max(s_n, axis=-1, keepdims=True))
            p_c = jnp.exp2(s_c - m)
            p_n = jnp.exp2(s_n - m)
            denom = jnp.sum(p_c, axis=-1, keepdims=True) + jnp.sum(p_n, axis=-1, keepdims=True)
            vch = jnp.where(head_mask[i], vc, jnp.zeros_like(vc))
            vnh = jnp.where(head_mask[i], vn, jnp.zeros_like(vn))
            out = out + (_dot(p_c.astype(BF16), vch) + _dot(p_n.astype(BF16), vnh)) * (1.0 / denom)
        o_ref[0, :, lanes] = out.astype(o_ref.dtype)


def _fox_sample(q, kn, vn, cache_k, cache_v, fp):
    b, l, _ = q.shape
    past = cache_k.shape[1]
    new = pl.BlockSpec((1, l, FOX_W), lambda bi: (bi, 0, 0))
    old = pl.BlockSpec((1, past, FOX_W), lambda bi: (bi, 0, 0))
    return pl.pallas_call(
        _fox_sample_kernel,
        out_shape=jax.ShapeDtypeStruct((b, l, FOX_W), BF16),
        grid=(b,),
        in_specs=[new, old, old, new, new,
                  pl.BlockSpec((1, fp.shape[1], LANES), lambda bi: (bi, 0, 0))],
        out_specs=new,
        compiler_params=_params("arbitrary"),
        name="fox_sample",
    )(q, cache_k, cache_v, kn, vn, fp)


def _layer_norm(x, w, b):
    mu = jnp.mean(x, axis=-1, keepdims=True)
    xc = x - mu
    var = jnp.mean(xc * xc, axis=-1, keepdims=True)
    return xc * lax.rsqrt(var + LN_EPS) * w + b


def _first_argmax_rows(x, n):
    rows = lax.broadcasted_iota(jnp.int32, x.shape, 0).astype(F32)
    mx = jnp.max(x, axis=0, keepdims=True)
    idx = jnp.min(jnp.where(x == mx, rows, float(n)), axis=0, keepdims=True)
    return mx, idx.astype(jnp.int32)


def _outproj_kernel(yr_ref, yf_ref, ga_ref, gb_ref, x_ref, mod_ref, wr_ref, wf_ref, wo_ref,
                    lw_ref, lb_ref, rw_ref, rb_ref,
                    x1_ref, h2_ref, eid_ref, gate_ref, *, alpha):
    gb_, tl, d = x_ref.shape
    if gb_ > 1:
        parts = [(slice(0, gb_ // 2), slice(None)), (slice(gb_ // 2, gb_), slice(None))]
    else:
        parts = [(slice(None), slice(0, tl // 2)), (slice(None), slice(tl // 2, tl))]
    rows = gb_ * tl // 2

    def mixed_branches(part):
        flat = lambda ref: ref[part[0], part[1], :].reshape(rows, ref.shape[-1])
        y_ret = _dot(flat(yr_ref), wr_ref[...])
        y_fox = _dot(flat(yf_ref), wf_ref[...])
        return flat(ga_ref).astype(F32) * y_ret + flat(gb_ref).astype(F32) * y_fox

    def norm_and_route(n, part, mixed):
        bs, ls = part
        mod = lambda k: mod_ref[bs, :, k * d:(k + 1) * d]
        x = x_ref[bs, ls, :]
        x1 = _layer_norm(alpha * x + (1.0 + mod(2)) * mixed.reshape(x.shape), lw_ref[...], lb_ref[...])
        x1_ref[bs, ls, :] = x1
        h2 = x1 * (1.0 + mod(4)) + mod(3)
        h2_ref[bs, ls, :] = h2.astype(h2_ref.dtype)
        h_hi, h_mid, _ = _split3(h2.reshape(rows, d))
        w_hi, w_mid = rw_ref[0], rw_ref[1]
        lt = _dot_nt(w_hi, h_hi) + _dot_nt(w_hi, h_mid) + _dot_nt(w_mid, h_hi) + rb_ref[...]
        gl = lt[N_EXPERTS:N_EXPERTS + N_GROUPS, :]
        gmax, gi = _first_argmax_rows(gl, N_GROUPS)
        g_p = 1.0 / jnp.sum(jnp.exp(gl - gmax), axis=0, keepdims=True)
        e_sel = lt[0:EXPERTS_PER_GROUP, :]
        for g in range(1, N_GROUPS):
            e_sel = jnp.where(gi == g, lt[g * EXPERTS_PER_GROUP:(g + 1) * EXPERTS_PER_GROUP, :], e_sel)
        sub = lax.broadcasted_iota(jnp.int32, e_sel.shape, 0)
        m1, i1 = _first_argmax_rows(e_sel, EXPERTS_PER_GROUP)
        m2, i2 = _first_argmax_rows(jnp.where(sub == i1, -jnp.inf, e_sel), EXPERTS_PER_GROUP)
        r = jnp.exp(m2 - m1)
        gate0 = g_p / (1.0 + r)
        gate1 = g_p * r / (1.0 + r)
        cols = slice(n * rows, (n + 1) * rows)
        eid_ref[0:1, cols] = gi * EXPERTS_PER_GROUP + i1
        eid_ref[1:2, cols] = gi * EXPERTS_PER_GROUP + i2
        gate_ref[0:1, cols] = gate0
        gate_ref[1:2, cols] = gate1

    mix = [mixed_branches(part) for part in parts]
    mixed = [_dot(mx.astype(BF16), wo_ref[...]) for mx in mix]
    for n, part in enumerate(parts):
        norm_and_route(n, part, mixed[n])


def _outproj(yr, yf, ga, gb, x, mod, wr, wf, wo, lw, lb, rw3, rb, gb_, tl, alpha):
    bx, l, d = x.shape
    m = bx * l
    tm = gb_ * tl
    nl = l // tl
    seq = lambda w: pl.BlockSpec((gb_, tl, w), lambda b, j: (b, j, 0))
    tok = lambda r: pl.BlockSpec((r, tm), lambda b, j: (0, b * nl + j))
    return pl.pallas_call(
        functools.partial(_outproj_kernel, alpha=alpha),
        out_shape=[jax.ShapeDtypeStruct((bx, l, d), F32), jax.ShapeDtypeStruct((bx, l, d), BF16),
                   jax.ShapeDtypeStruct((2, m), jnp.int32), jax.ShapeDtypeStruct((2, m), F32)],
        grid=(bx // gb_, nl),
        in_specs=[seq(RET_V_W), seq(FOX_W), seq(d), seq(d), seq(d),
                  pl.BlockSpec((gb_, 1, mod.shape[-1]), lambda b, j: (b, 0, 0)),
                  _resident(wr.shape), _resident(wf.shape), _resident(wo.shape),
                  _resident(lw.shape), _resident(lb.shape), _resident(rw3.shape), _resident(rb.shape)],
        out_specs=[seq(d), seq(d), tok(2), tok(2)],
        compiler_params=_params("arbitrary", "arbitrary"),
        name="outproj",
    )(yr, yf, ga, gb, x, mod, wr, wf, wo, lw, lb, rw3, rb)


MOE_CHUNK = 8


def _ceil_chunk(x):
    return jnp.floor((x + (MOE_CHUNK - 1.0)) * (1.0 / MOE_CHUNK)) * MOE_CHUNK


def _plan_kernel(eid_ref, triu_ref, ld_ref, pc_ref, pre_ref, carry_ref):
    @pl.when(pl.program_id(0) == 0)
    def _():
        carry_ref[...] = jnp.zeros_like(carry_ref)

    t = eid_ref.shape[1]
    experts = lax.broadcasted_iota(jnp.int32, (N_EXPERTS, t), 0)
    hit = [eid_ref[kk:kk + 1, :] == experts for kk in range(2)]
    onehot = [jnp.where(h, 1.0, 0.0) for h in hit]
    onehot_b = [o.astype(BF16) for o in onehot]
    ones = jnp.ones((8, t), BF16)
    cnt_row = (_dot_nt(ones, onehot_b[0]) + _dot_nt(ones, onehot_b[1]))[0:1, :]
    cnt0_col = jnp.sum(onehot[0], axis=1, keepdims=True)
    pc_row = _ceil_chunk(cnt_row)
    e_lane = lax.broadcasted_iota(jnp.int32, (N_EXPERTS, N_EXPERTS), 1)
    e_sub = lax.broadcasted_iota(jnp.int32, (N_EXPERTS, N_EXPERTS), 0)
    lstart_col = jnp.sum(jnp.where(e_lane < e_sub, pc_row, 0.0), axis=1, keepdims=True)
    for kk in range(2):
        before = _dot(onehot_b[kk], triu_ref[...])
        base = lstart_col + cnt0_col if kk == 1 else lstart_col
        row = jnp.sum(jnp.where(hit[kk], before + base, 0.0), axis=0, keepdims=True)
        ld_ref[kk:kk + 1, :] = row.astype(jnp.int32)
    pc_ref[0] = pc_row.astype(jnp.int32)
    pre_ref[0] = carry_ref[...].astype(jnp.int32)
    carry_ref[...] = carry_ref[...] + pc_row


def _plan(eid, t):
    m = eid.shape[1]
    n_tiles = m // t
    triu = jnp.asarray(np.triu(np.ones((t, t), np.float32), 1), BF16)
    per_tile = jax.ShapeDtypeStruct((n_tiles, 1, N_EXPERTS), jnp.int32)
    per_tile_spec = pl.BlockSpec((1, 1, N_EXPERTS), lambda i: (i, 0, 0))
    return pl.pallas_call(
        _plan_kernel,
        out_shape=[jax.ShapeDtypeStruct((2, m), jnp.int32), per_tile, per_tile],
        grid=(n_tiles,),
        in_specs=[pl.BlockSpec((2, t), lambda i: (0, i)), pl.BlockSpec((t, t), lambda i: (0, 0))],
        out_specs=[pl.BlockSpec((2, t), lambda i: (0, i)), per_tile_spec, per_tile_spec],
        scratch_shapes=[pltpu.VMEM((1, N_EXPERTS), F32)],
        compiler_params=_params("arbitrary"),
        name="moe_plan",
    )(eid, triu)


def _for_each_chunk(tile, pc_ref, goff_ref, fn):
    def per_expert(e, local):
        n = lax.div(pc_ref[tile * N_EXPERTS + e], jnp.int32(MOE_CHUNK))
        dst = goff_ref[tile * N_EXPERTS + e]

        def per_chunk(j, c):
            fn(pl.multiple_of(local + j * MOE_CHUNK, MOE_CHUNK), pl.multiple_of(dst + j * MOE_CHUNK, MOE_CHUNK))
            return c

        lax.fori_loop(0, n, per_chunk, 0)
        return local + n * MOE_CHUNK

    return lax.fori_loop(0, N_EXPERTS, per_expert, 0)


def _one_hot_rows(ld_row, lp):
    rows = lax.broadcasted_iota(jnp.int32, (lp, ld_row.shape[1]), 0)
    return rows == ld_row


HIGH_HALF = -65536


def _pack_halves(x):
    half = x.shape[1] // 2
    lo = pltpu.bitcast(x[:, :half], jnp.int32)
    hi = pltpu.bitcast(x[:, half:], jnp.int32)
    return (hi & HIGH_HALF) | lax.shift_right_logical(lo, jnp.int32(16))


def _unpack_halves(w):
    lo = pltpu.bitcast(lax.shift_left(w, jnp.int32(16)), F32)
    hi = pltpu.bitcast(w & HIGH_HALF, F32)
    return lo.astype(BF16), hi.astype(BF16)


def _round_bf16(x):
    return x.astype(BF16).astype(F32)


def _dispatch_kernel(pc_ref, goff_ref, tail_ref, ld_ref, h_ref, xs_ref, buf_ref, zero_ref, rows_ref, sem,
                     zero_sem):
    tile = pl.program_id(0)
    n_tiles = pl.num_programs(0)
    slot = tile % 2
    lp = buf_ref.shape[1]
    tb = zero_ref.shape[0]

    def copy(s, local, dst):
        return pltpu.make_async_copy(buf_ref.at[s, pl.ds(local, MOE_CHUNK)], xs_ref.at[pl.ds(dst, MOE_CHUNK)],
                                     sem.at[s])

    def fill_chunk(dst):
        return pltpu.make_async_copy(zero_ref.at[pl.ds(0, MOE_CHUNK)], xs_ref.at[pl.ds(dst, MOE_CHUNK)], zero_sem)

    def fill_block(dst):
        return pltpu.make_async_copy(zero_ref, xs_ref.at[pl.ds(dst, tb)], zero_sem)

    def for_each_fill(on_chunk, on_block):
        def per_expert(e, c):
            first = tail_ref[e]

            def per_chunk(j, cc):
                on_chunk(pl.multiple_of(first + j * MOE_CHUNK, MOE_CHUNK))
                return cc

            lax.fori_loop(0, tail_ref[N_EXPERTS + e], per_chunk, 0)
            return c

        lax.fori_loop(0, N_EXPERTS, per_expert, 0)
        first = tail_ref[2 * N_EXPERTS]

        def per_block(j, c):
            on_block(pl.multiple_of(first + j * tb, tb))
            return c

        lax.fori_loop(0, tail_ref[2 * N_EXPERTS + 1], per_block, 0)

    perm = jnp.where(_one_hot_rows(ld_ref[0:1, :], lp), 1.0,
                     jnp.where(_one_hot_rows(ld_ref[1:2, :], lp), 1.0, 0.0)).astype(BF16)
    buf_ref[slot] = _pack_halves(_dot(perm, h_ref[...]))
    rows_ref[slot] = _for_each_chunk(tile, pc_ref, goff_ref, lambda a, b: copy(slot, a, b).start())

    def wait_rows(s):
        n = pl.multiple_of(rows_ref[s], MOE_CHUNK)
        pltpu.make_async_copy(buf_ref.at[s, pl.ds(0, n)], xs_ref.at[pl.ds(0, n)], sem.at[s]).wait()

    @pl.when(tile == 0)
    def _():
        zero_ref[...] = jnp.zeros_like(zero_ref)
        for_each_fill(lambda d: fill_chunk(d).start(), lambda d: fill_block(d).start())
        for_each_fill(lambda d: fill_chunk(d).wait(), lambda d: fill_block(d).wait())

    @pl.when(tile > 0)
    def _():
        wait_rows(1 - slot)

    @pl.when(tile == n_tiles - 1)
    def _():
        wait_rows(slot)


def _dispatch(pc, goff, tails, ld, h2, n_slots, t, lp, tb):
    m, d = h2.shape
    return pl.pallas_call(
        _dispatch_kernel,
        out_shape=jax.ShapeDtypeStruct((n_slots, d // 2), jnp.int32),
        grid_spec=pltpu.PrefetchScalarGridSpec(
            num_scalar_prefetch=3,
            grid=(m // t,),
            in_specs=[pl.BlockSpec((2, t), lambda i, *_: (0, i)),
                      pl.BlockSpec((t, d), lambda i, *_: (i, 0))],
            out_specs=pl.BlockSpec(memory_space=pl.ANY),
            scratch_shapes=[pltpu.VMEM((2, lp, d // 2), jnp.int32), pltpu.VMEM((tb, d // 2), jnp.int32),
                            pltpu.SMEM((2,), jnp.int32),
                            pltpu.SemaphoreType.DMA((2,)), pltpu.SemaphoreType.DMA]),
        compiler_params=_params("arbitrary"),
        name="moe_dispatch",
    )(pc, goff, tails, ld, h2)


def _expert_kernel(be_ref, nv_ref, x_ref, w1_ref, w2_ref, y_ref):
    del be_ref

    @pl.when(pl.program_id(0) < nv_ref[0])
    def _():
        e = w2_ref.shape[1]
        half = x_ref.shape[1]
        x_lo, x_hi = _unpack_halves(x_ref[...])
        au = _dot(x_lo, w1_ref[0, :half, :]) + _dot(x_hi, w1_ref[0, half:, :])
        a = au[:, :e]
        u = au[:, e:]
        y = _dot((a * _sigmoid(a) * u).astype(BF16), w2_ref[0])
        y_ref[...] = _pack_halves(_round_bf16(y))

    @pl.when(pl.program_id(0) >= nv_ref[0])
    def _():
        y_ref[...] = jnp.zeros_like(y_ref)


def _experts(block_e, n_valid, xs, w1, w2, tb):
    p, half = xs.shape
    d = 2 * half
    e = w2.shape[1]
    rows = lambda i, be, nv: (jnp.minimum(i, nv[0] - 1), 0)
    return pl.pallas_call(
        _expert_kernel,
        out_shape=jax.ShapeDtypeStruct((p, half), jnp.int32),
        grid_spec=pltpu.PrefetchScalarGridSpec(
            num_scalar_prefetch=2,
            grid=(p // tb,),
            in_specs=[pl.BlockSpec((tb, half), rows),
                      pl.BlockSpec((1, d, 2 * e), lambda i, be, nv: (be[i], 0, 0)),
                      pl.BlockSpec((1, e, d), lambda i, be, nv: (be[i], 0, 0))],
            out_specs=pl.BlockSpec((tb, half), lambda i, be, nv: (i, 0))),
        compiler_params=_params("arbitrary"),
        name="moe_experts",
    )(block_e, n_valid, xs, w1, w2)


def _combine_kernel(pc_ref, goff_ref, ld_ref, gate_ref, y_hbm, x1_ref, mod_ref, lw_ref, lb_ref, o_ref,
                    buf_ref, rows_ref, sem, *, alpha):
    gb_, tl, d = x1_ref.shape
    nl = pl.num_programs(1)
    n_tiles = pl.num_programs(0) * nl
    tile = pl.program_id(0) * nl + pl.program_id(1)
    slot = tile % 2
    lp = buf_ref.shape[1]

    def copy(s, local, src):
        return pltpu.make_async_copy(y_hbm.at[pl.ds(src, MOE_CHUNK)], buf_ref.at[s, pl.ds(local, MOE_CHUNK)],
                                     sem.at[s])

    @pl.when(tile == 0)
    def _():
        buf_ref[...] = jnp.zeros_like(buf_ref)
        rows_ref[slot] = _for_each_chunk(tile, pc_ref, goff_ref, lambda a, b: copy(slot, a, b).start())

    @pl.when(tile + 1 < n_tiles)
    def _():
        rows_ref[1 - slot] = _for_each_chunk(tile + 1, pc_ref, goff_ref,
                                             lambda a, b: copy(1 - slot, a, b).start())

    n = pl.multiple_of(rows_ref[slot], MOE_CHUNK)
    pltpu.make_async_copy(y_hbm.at[pl.ds(0, n)], buf_ref.at[slot, pl.ds(0, n)], sem.at[slot]).wait()
    y_lo, y_hi = _unpack_halves(buf_ref[slot])
    zero = jnp.zeros((), F32)
    unsort = jnp.where(_one_hot_rows(ld_ref[0:1, :], lp), gate_ref[0:1, :],
                       jnp.where(_one_hot_rows(ld_ref[1:2, :], lp), gate_ref[1:2, :], zero)).astype(BF16)
    y = jnp.concatenate([_dot_tn(unsort, y_lo), _dot_tn(unsort, y_hi)], axis=1)
    g2 = mod_ref[:, :, 5 * d:6 * d]
    o_ref[...] = _layer_norm(alpha * x1_ref[...] + (1.0 + g2) * y.reshape(gb_, tl, d), lw_ref[...], lb_ref[...])


def _combine(pc, goff, ld, gate, y, x1, mod, lw, lb, gb_, tl, alpha, lp):
    bx, l, d = x1.shape
    tm = gb_ * tl
    nl = l // tl
    seq = pl.BlockSpec((gb_, tl, d), lambda b, j, pc, go: (b, j, 0))
    tok = pl.BlockSpec((2, tm), lambda b, j, pc, go: (0, b * nl + j))
    return pl.pallas_call(
        functools.partial(_combine_kernel, alpha=alpha),
        out_shape=jax.ShapeDtypeStruct((bx, l, d), F32),
        grid_spec=pltpu.PrefetchScalarGridSpec(
            num_scalar_prefetch=2,
            grid=(bx // gb_, nl),
            in_specs=[tok, tok,
                      pl.BlockSpec(memory_space=pl.ANY),
                      seq,
                      pl.BlockSpec((gb_, 1, mod.shape[-1]), lambda b, j, pc, go: (b, 0, 0)),
                      _resident(lw.shape), _resident(lb.shape)],
            out_specs=seq,
            scratch_shapes=[pltpu.VMEM((2, lp, d // 2), jnp.int32), pltpu.SMEM((2,), jnp.int32),
                            pltpu.SemaphoreType.DMA((2,))]),
        compiler_params=_params("arbitrary", "arbitrary"),
        name="moe_combine",
    )(pc, goff, ld, gate, y, x1, mod, lw, lb)


def _moe(h2, eid, gate, x1, mod, w1, w2, lw, lb, gb_, tl, alpha, tb):
    bx, l, d = x1.shape
    m = bx * l
    t = gb_ * tl
    n_tiles = m // t
    lp = 2 * t + N_EXPERTS * MOE_CHUNK
    ld, pc, pre = _plan(eid, t)
    pc = pc.reshape(n_tiles, N_EXPERTS)
    pre = pre.reshape(n_tiles, N_EXPERTS)
    total = pre[-1] + pc[-1]
    region = (total + tb - 1) // tb * tb
    gend = jnp.cumsum(region)
    goff = (gend - region)[None, :] + pre
    n_blocks = -(-(2 * m + n_tiles * N_EXPERTS * (MOE_CHUNK - 1) + N_EXPERTS * (tb - 1)) // tb)
    block_row0 = jnp.arange(n_blocks, dtype=jnp.int32) * tb
    block_e = jnp.minimum(jnp.sum((gend[None, :] <= block_row0[:, None]).astype(jnp.int32), axis=1),
                          N_EXPERTS - 1).astype(jnp.int32)
    n_valid = (gend[-1:] // tb).astype(jnp.int32)
    tails = jnp.concatenate([gend - region + total, (region - total) // MOE_CHUNK,
                             gend[-1:], n_blocks - gend[-1:] // tb]).astype(jnp.int32)
    pc = pc.reshape(-1).astype(jnp.int32)
    goff = goff.reshape(-1).astype(jnp.int32)
    xs = _dispatch(pc, goff, tails, ld, h2.reshape(m, d), n_blocks * tb, t, lp, tb)
    y = _experts(block_e, n_valid, xs, w1, w2, tb)
    return _combine(pc, goff, ld, gate, y, x1, mod, lw, lb, gb_, tl, alpha, lp)


def _rotary_tables(pos, reps):
    half = RET_QK_DIM // 2
    inv = ROPE_BASE ** (-jnp.linspace(0.0, 1.0, half, dtype=F32))
    ang = pos.astype(F32)[:, None] * inv[None, :]
    cos = jnp.cos(ang)
    sin = jnp.sin(ang)
    c2 = jnp.concatenate([cos, cos], axis=1)
    s2 = jnp.concatenate([-sin, sin], axis=1)
    kscale = RET_QK_DIM ** -0.5
    tabs = (c2, s2, c2 * kscale, s2 * kscale)
    return tuple(jnp.tile(t, (reps, 1)) for t in tabs)


def _pick(n, pref):
    t = min(n, pref)
    while n % t:
        t //= 2
    return t


def _layer(x, mod, pos, weights, log_gamma, alpha, cache=None):
    (w2, b2, gn_w, gn_b, wr, wf, wo, ln1w, ln1b, rw3, rb, we1, we2, ln2w, ln2b) = weights
    bx, l, d = x.shape
    if cache is None:
        gb_, tl = 1, _pick(l, 512)
    else:
        gb_, tl = _pick(bx, max(1, 512 // l)), l
    tabs = _rotary_tables(pos, gb_ if cache is not None else 1)
    rq, rk, rv, rg, fq, fk, fv, fkb, fvb, lf, ga, gb = _inproj(x, mod, w2, b2, tabs, gb_, tl)

    lft = jnp.swapaxes(lf[:, :, :FOX_HEADS], 1, 2)
    if cache is None:
        lc = _pick(l, 256)
        y_ret, state = _retention(rq, rk, rv, rg, gn_w, gn_b, log_gamma, lc)
        fp = _fcum(lft, _pick(l, 512))
        y_fox = _fox_prompt(fq, fkb, fp, fvb, _pick(l, 512))
    else:
        state0, cache_k, cache_v, cache_logf = cache
        y_ret, state = _retention(rq, rk, rv, rg, gn_w, gn_b, log_gamma, l, state0)
        past = cache_k.shape[1]
        total = -(-(past + l) // LANES) * LANES
        lft_all = jnp.concatenate(
            [jnp.swapaxes(cache_logf.astype(F32), 1, 2), lft,
             jnp.zeros((bx, FOX_HEADS, total - past - l), F32)], axis=2)
        fp = _fcum(lft_all, total)
        y_fox = _fox_sample(fq, fkb, fvb, cache_k.reshape(bx, past, FOX_W), cache_v.reshape(bx, past, FOX_W), fp)

    x1, h2, eid, gate = _outproj(y_ret, y_fox, ga, gb, x, mod, wr, wf, wo, ln1w, ln1b, rw3, rb, gb_, tl, alpha)
    m = bx * l
    out = _moe(h2, eid, gate, x1, mod, we1, we2, ln2w, ln2b, gb_, tl, alpha, tb=512)
    return out, state, fk, fv, lf[:, :, :FOX_HEADS]


def kernel(x_prompt, x_sample, state_ret, cache_fox_k, cache_fox_v, cache_fox_logf, c_prompt, c_sample,
           w_ada, b_ada, w_in, b_in, ret_gn_w, ret_gn_b, w_ret_proj, w_fox_proj, w_o, ln1_w, ln1_b,
           w_rg, b_rg, w_re, b_re, w_e_in, w_e_out, ln2_w, ln2_b):
    depth = w_ada.shape[0]
    d = x_prompt.shape[-1]
    bp, s, _ = x_prompt.shape
    bs, ls, _ = x_sample.shape
    past = cache_fox_k.shape[2]
    alpha = (2 * depth) ** 0.25
    log_gamma = jnp.log1p(-jnp.exp(jnp.linspace(math.log(1.0 / 32), math.log(1.0 / 512), RET_HEADS, dtype=F32)))
    pos_p = jnp.arange(s, dtype=jnp.int32)
    pos_s = past + jnp.arange(ls, dtype=jnp.int32)
    fg = 2 * RET_QK_W + 2 * RET_V_W + 3 * FOX_W

    xp, xs = x_prompt, x_sample
    p_ret, p_k, p_v, p_f, s_ret, s_k, s_v, s_f = [], [], [], [], [], [], [], []
    for li in range(depth):
        w2 = jnp.concatenate([w_in[li][:, :fg], w_in[li][:, fg + FOX_HEADS:],
                              jnp.pad(w_in[li][:, fg:fg + FOX_HEADS], ((0, 0), (0, LANES - FOX_HEADS)))],
                             axis=1).astype(BF16)
        b2 = jnp.concatenate([b_in[li][:fg], b_in[li][fg + FOX_HEADS:],
                              jnp.pad(b_in[li][fg:fg + FOX_HEADS], (0, LANES - FOX_HEADS))]).reshape(1, -1)
        n_rt = N_EXPERTS + N_GROUPS
        rt_rows = -(-n_rt // 8) * 8
        rwt = jnp.pad(jnp.concatenate([w_re[li], w_rg[li]], axis=1).T.astype(F32), ((0, rt_rows - n_rt), (0, 0)))
        r_hi = rwt.astype(BF16)
        r_mid = (rwt - r_hi.astype(F32)).astype(BF16)
        rw3 = jnp.stack([r_hi, r_mid])
        rb = jnp.pad(jnp.concatenate([b_re[li], b_rg[li]]).astype(F32), (0, rt_rows - n_rt)).reshape(rt_rows, 1)
        weights = (w2, b2, ret_gn_w[li], ret_gn_b[li],
                   w_ret_proj[li].astype(BF16), w_fox_proj[li].astype(BF16), w_o[li].astype(BF16),
                   ln1_w[li].reshape(1, d), ln1_b[li].reshape(1, d), rw3, rb,
                   w_e_in[li].astype(BF16), w_e_out[li].astype(BF16),
                   ln2_w[li].reshape(1, d), ln2_b[li].reshape(1, d))
        mod = _ada(jnp.concatenate([c_prompt, c_sample], axis=0), w_ada[li], b_ada[li])
        mod_p = mod[:bp].reshape(bp, 1, 6 * d)
        mod_s = mod[bp:].reshape(bs, 1, 6 * d)
        xp, st, kk, vv, ff = _layer(xp, mod_p, pos_p, weights, log_gamma, alpha)
        p_ret.append(st); p_k.append(kk.reshape(bp, s, FOX_HEADS, FOX_HEAD_DIM))
        p_v.append(vv.reshape(bp, s, FOX_HEADS, FOX_HEAD_DIM)); p_f.append(ff)
        cache = (state_ret[li].astype(F32), cache_fox_k[li], cache_fox_v[li], cache_fox_logf[li])
        xs, st, kk, vv, ff = _layer(xs, mod_s, pos_s, weights, log_gamma, alpha, cache)
        s_ret.append(st); s_k.append(kk.reshape(bs, ls, FOX_HEADS, FOX_HEAD_DIM))
        s_v.append(vv.reshape(bs, ls, FOX_HEADS, FOX_HEAD_DIM)); s_f.append(ff)
    return (xp, xs, jnp.stack(p_ret), jnp.stack(p_k), jnp.stack(p_v), jnp.stack(p_f),
            jnp.stack(s_ret), jnp.stack(s_k), jnp.stack(s_v), jnp.stack(s_f))
```

```python
import functools
import math

import jax
import jax.numpy as jnp
import numpy as np
from jax import lax
from jax.experimental import pallas as pl
from jax.experimental.pallas import tpu as pltpu

F32 = jnp.float32
BF16 = jnp.bfloat16

RET_HEADS = 4
RET_QK_DIM = 128
RET_V_DIM = 256
ROPE_BASE = 10000.0
FOX_HEADS = 16
FOX_HEAD_DIM = 64
N_GROUPS = 4
EXPERTS_PER_GROUP = 8
N_EXPERTS = N_GROUPS * EXPERTS_PER_GROUP
LN_EPS = 1e-5
GN_EPS = 1e-6
RET_QK_W = RET_HEADS * RET_QK_DIM
RET_V_W = RET_HEADS * RET_V_DIM
FOX_W = FOX_HEADS * FOX_HEAD_DIM
FOX_PAIRS = FOX_HEADS // 2
LOG2_E = math.log2(math.e)
FOX_Q_SCALE = FOX_HEAD_DIM ** -0.5 * LOG2_E

LANES = 128
VMEM_LIMIT_BYTES = 56 * 1024 * 1024
MASK_VALUE = -1e30


def _dot(a, b):
    return jnp.dot(a, b, preferred_element_type=F32)


def _dot_nt(a, b):
    return lax.dot_general(a, b, (((1,), (1,)), ((), ())), preferred_element_type=F32)


def _dot_tn(a, b):
    return lax.dot_general(a, b, (((0,), (0,)), ((), ())), preferred_element_type=F32)


def _split3(x):
    hi = x.astype(BF16)
    r1 = x - hi.astype(F32)
    mid = r1.astype(BF16)
    lo = (r1 - mid.astype(F32)).astype(BF16)
    return hi, mid, lo


def _sigmoid(x):
    return 1.0 / (1.0 + jnp.exp(-x))


def _log_sigmoid(x):
    return -(jnp.maximum(-x, 0.0) + jnp.log1p(jnp.exp(-jnp.abs(x))))


def _params(*sem):
    return pltpu.CompilerParams(dimension_semantics=sem, vmem_limit_bytes=VMEM_LIMIT_BYTES)


def _resident(shape):
    nd = len(shape)
    return pl.BlockSpec(shape, lambda *_: (0,) * nd, pipeline_mode=pl.Buffered(1))


def _ada_kernel(c_ref, w_ref, b_ref, o_ref):
    c = c_ref[...]
    s = (c * _sigmoid(c)).astype(BF16)
    o_ref[...] = _dot(s, w_ref[...].astype(BF16)) + b_ref[...]


def _ada(c, w, b):
    n, d = c.shape
    nout = w.shape[1]
    tn = d
    return pl.pallas_call(
        _ada_kernel,
        out_shape=jax.ShapeDtypeStruct((n, nout), F32),
        grid=(nout // tn,),
        in_specs=[pl.BlockSpec((n, d), lambda j: (0, 0)),
                  pl.BlockSpec((d, tn), lambda j: (0, j)),
                  pl.BlockSpec((1, tn), lambda j: (0, j))],
        out_specs=pl.BlockSpec((n, tn), lambda j: (0, j)),
        compiler_params=_params("arbitrary"),
        name="ada",
    )(c, w, b.reshape(1, nout))


def _inproj_kernel(x_ref, mod_ref, w_ref, b_ref, cq_ref, sq_ref, ck_ref, sk_ref,
                   rq_ref, rk_ref, rv_ref, rg_ref, fq_ref, fk_ref, fv_ref, fkb_ref, fvb_ref,
                   lf_ref, ga_ref, gb_ref, *, transposed_v):
    gb_, tl, d = x_ref.shape
    tm = gb_ * tl
    sh = mod_ref[:, :, 0:d]
    sc = mod_ref[:, :, d:2 * d]
    h = (x_ref[...] * (1.0 + sc) + sh).reshape(tm, d).astype(BF16)

    def proj(lo, width):
        return _dot(h, w_ref[:, lo:lo + width]) + b_ref[:, lo:lo + width]

    def put(ref, val):
        ref[...] = val.reshape(ref.shape).astype(ref.dtype)

    def rot(z, c_ref, s_ref):
        c = c_ref[...]
        s = s_ref[...]
        parts = []
        for hh in range(RET_HEADS):
            zh = z[:, hh * RET_QK_DIM:(hh + 1) * RET_QK_DIM]
            parts.append(zh * c + pltpu.roll(zh, RET_QK_DIM // 2, axis=1) * s)
        return jnp.concatenate(parts, axis=1)

    off = 0
    put(rq_ref, rot(proj(off, RET_QK_W), cq_ref, sq_ref)); off += RET_QK_W
    put(rk_ref, rot(proj(off, RET_QK_W), ck_ref, sk_ref)); off += RET_QK_W
    put(rv_ref, proj(off, RET_V_W)); off += RET_V_W
    z = proj(off, RET_V_W); off += RET_V_W
    put(rg_ref, z * _sigmoid(z))
    put(fq_ref, proj(off, FOX_W) * FOX_Q_SCALE); off += FOX_W
    z = proj(off, FOX_W); off += FOX_W
    put(fk_ref, z); put(fkb_ref, z)
    z = proj(off, FOX_W); off += FOX_W
    put(fv_ref, z)
    if transposed_v:
        fvb_ref[0] = z.T.astype(fvb_ref.dtype)
    else:
        put(fvb_ref, z)
    put(ga_ref, _sigmoid(proj(off, d))); off += d
    put(gb_ref, _sigmoid(proj(off, d))); off += d
    put(lf_ref, _log_sigmoid(proj(off, LANES)))


def _inproj(x, mod, w2, b2, tabs, gb_, tl):
    bx, l, d = x.shape
    tm = gb_ * tl
    nw = w2.shape[1]
    grid = (bx // gb_, l // tl)
    if gb_ == 1:
        tab_spec = pl.BlockSpec((tl, LANES), lambda b, j: (j, 0))
    else:
        tab_spec = pl.BlockSpec((tm, LANES), lambda b, j: (0, 0))

    def out(width, dtype):
        return (jax.ShapeDtypeStruct((bx, l, width), dtype),
                pl.BlockSpec((gb_, tl, width), lambda b, j: (b, j, 0)))

    transposed_v = gb_ == 1
    if transposed_v:
        fvb = (jax.ShapeDtypeStruct((bx, FOX_W, l), BF16), pl.BlockSpec((1, FOX_W, tl), lambda b, j: (b, 0, j)))
    else:
        fvb = out(FOX_W, BF16)
    outs = [out(RET_QK_W, BF16), out(RET_QK_W, BF16), out(RET_V_W, BF16), out(RET_V_W, BF16),
            out(FOX_W, BF16), out(FOX_W, F32), out(FOX_W, F32), out(FOX_W, BF16), fvb,
            out(LANES, F32), out(d, BF16), out(d, BF16)]
    return pl.pallas_call(
        functools.partial(_inproj_kernel, transposed_v=transposed_v),
        out_shape=[o[0] for o in outs],
        grid=grid,
        in_specs=[pl.BlockSpec((gb_, tl, d), lambda b, j: (b, j, 0)),
                  pl.BlockSpec((gb_, 1, mod.shape[-1]), lambda b, j: (b, 0, 0)),
                  _resident((d, nw)), _resident((1, nw)),
                  tab_spec, tab_spec, tab_spec, tab_spec],
        out_specs=[o[1] for o in outs],
        compiler_params=_params("arbitrary", "arbitrary"),
        name="inproj",
    )(x, mod, w2, b2, *tabs)


BIAS_PIECES = 3


def _fcum_kernel(lft_ref, triu_ref, fp_ref, carry_ref):
    @pl.when(pl.program_id(1) == 0)
    def _():
        carry_ref[...] = jnp.zeros_like(carry_ref)

    tl = lft_ref.shape[2]
    hi, mid, lo = _split3(lft_ref[0])
    triu = triu_ref[...]
    cum = _dot(hi, triu) + _dot(mid, triu) + _dot(lo, triu) + carry_ref[...]
    carry_ref[...] = cum[:, tl - 1:tl]
    padded = jnp.concatenate([cum * -LOG2_E, jnp.zeros((LANES - FOX_HEADS, tl), F32)], axis=0)
    pieces = _split3(padded.T)
    out = pieces[0].astype(F32)
    for p in range(1, BIAS_PIECES):
        out = out + pltpu.roll(pieces[p].astype(F32), p * FOX_HEADS, axis=1)
    fp_ref[0] = out.astype(fp_ref.dtype)


def _fcum(lft, tl):
    bx, _, l = lft.shape
    triu = jnp.asarray(np.triu(np.ones((tl, tl), np.float32)), BF16)
    return pl.pallas_call(
        _fcum_kernel,
        out_shape=jax.ShapeDtypeStruct((bx, l, LANES), BF16),
        grid=(bx, l // tl),
        in_specs=[pl.BlockSpec((1, FOX_HEADS, tl), lambda b, j: (b, 0, j)),
                  pl.BlockSpec((tl, tl), lambda b, j: (0, 0))],
        out_specs=pl.BlockSpec((1, tl, LANES), lambda b, j: (b, j, 0)),
        scratch_shapes=[pltpu.VMEM((FOX_HEADS, 1), F32)],
        compiler_params=_params("arbitrary", "arbitrary"),
        name="fcum",
    )(lft, triu)


def _retention_kernel(*refs, has_state):
    if has_state:
        (q_ref, k_ref, v_ref, g_ref, dec_ref, qd_ref, kd_ref, gw_ref, gb_ref, s0_ref,
         y_ref, st_ref) = refs
    else:
        (q_ref, k_ref, v_ref, g_ref, dec_ref, qd_ref, kd_ref, gw_ref, gb_ref,
         y_ref, st_ref) = refs
        s0_ref = None
    lc = q_ref.shape[1]

    @pl.when(pl.program_id(1) == 0)
    def _():
        if has_state:
            st_ref[...] = s0_ref[...]
        else:
            st_ref[...] = jnp.zeros_like(st_ref)

    for hh in range(RET_HEADS):
        qs = slice(hh * RET_QK_DIM, (hh + 1) * RET_QK_DIM)
        vs = slice(hh * RET_V_DIM, (hh + 1) * RET_V_DIM)
        q = q_ref[0, :, qs]
        k = k_ref[0, :, qs]
        v = v_ref[0, :, vs]
        state = st_ref[0, hh]
        qd = qd_ref[hh]
        scores = _dot_nt(q, k) * dec_ref[hh]
        inner = _dot(scores.astype(BF16), v)
        cross = _dot(q, state.astype(BF16)) * jnp.concatenate([qd] * (RET_V_DIM // LANES), axis=1)
        o = inner + cross
        kdec = (k.astype(F32) * kd_ref[hh]).astype(BF16)
        st_ref[0, hh] = qd[lc - 1:lc, 0:1] * state + _dot_tn(kdec, v)
        mu = jnp.mean(o, axis=-1, keepdims=True)
        oc = o - mu
        var = jnp.mean(oc * oc, axis=-1, keepdims=True)
        on = oc * lax.rsqrt(var + GN_EPS) * gw_ref[:, vs] + gb_ref[:, vs]
        y_ref[0, :, vs] = (g_ref[0, :, vs].astype(F32) * on).astype(y_ref.dtype)


def _retention_tables(log_gamma, lc):
    n = jnp.arange(lc, dtype=F32)
    diff = n[:, None] - n[None, :]
    decay = jnp.where(diff[None] >= 0, jnp.exp(jnp.maximum(diff, 0.0)[None] * log_gamma[:, None, None]), 0.0)
    qdec = jnp.exp((n + 1.0)[None, :, None] * log_gamma[:, None, None])
    kdec = jnp.exp((lc - 1.0 - n)[None, :, None] * log_gamma[:, None, None])
    lanes = (RET_HEADS, lc, LANES)
    return decay, jnp.broadcast_to(qdec, lanes), jnp.broadcast_to(kdec, lanes)


def _retention(q, k, v, g, gn_w, gn_b, log_gamma, lc, state0=None):
    bx, l, _ = q.shape
    decay, qdec, kdec = _retention_tables(log_gamma, lc)
    has_state = state0 is not None
    seq = lambda w: pl.BlockSpec((1, lc, w), lambda b, c: (b, c, 0))
    whole = lambda a: pl.BlockSpec(a.shape, lambda b, c: (0,) * a.ndim)
    st_spec = pl.BlockSpec((1, RET_HEADS, RET_QK_DIM, RET_V_DIM), lambda b, c: (b, 0, 0, 0))
    gw = gn_w.reshape(1, RET_V_W)
    gb = gn_b.reshape(1, RET_V_W)
    args = [q, k, v, g, decay, qdec, kdec, gw, gb]
    in_specs = [seq(RET_QK_W), seq(RET_QK_W), seq(RET_V_W), seq(RET_V_W),
                whole(decay), whole(qdec), whole(kdec), whole(gw), whole(gb)]
    if has_state:
        args.append(state0)
        in_specs.append(st_spec)
    return pl.pallas_call(
        functools.partial(_retention_kernel, has_state=has_state),
        out_shape=[jax.ShapeDtypeStruct((bx, l, RET_V_W), BF16),
                   jax.ShapeDtypeStruct((bx, RET_HEADS, RET_QK_DIM, RET_V_DIM), F32)],
        grid=(bx, l // lc),
        in_specs=in_specs,
        out_specs=[seq(RET_V_W), st_spec],
        compiler_params=_params("arbitrary", "arbitrary"),
        name="retention",
    )(*args)


def _pair_queries(q2, pair):
    t = q2.shape[0]
    lane = lax.broadcasted_iota(jnp.int32, (t, LANES), 1)
    out = []
    for i in range(2):
        head = (lane >= i * FOX_HEAD_DIM) & (lane < (i + 1) * FOX_HEAD_DIM)
        offset = lane - (2 * pair + i)
        ones = (offset >= 0) & (offset < BIAS_PIECES * FOX_HEADS) & ((offset & (FOX_HEADS - 1)) == 0)
        out.append(jnp.concatenate([jnp.where(head, q2, jnp.zeros_like(q2)),
                                    jnp.where(ones, 1.0, 0.0).astype(q2.dtype)], axis=1))
    return out


def _fox_prompt_kernel(q_ref, k_ref, fp_ref, vt_ref, o_ref, m_ref, l_ref, acc_ref, qc_ref, sa_ref, sb_ref, *, t):
    nq = q_ref.shape[1] // t
    steps = [(qi, ki) for qi in range(nq) for ki in range(qi + 1)]
    bufs = (sa_ref, sb_ref)
    half = t // 2

    def tiles(qi, ki):
        if ki == qi:
            return [((0, half), (0, t)), ((half, t), (half, t))]
        return [((0, t), (0, t))]

    def produce(n, i):
        qi, ki = steps[n]
        if ki == 0 and i == 0:
            qc = _pair_queries(q_ref[0, qi * t:(qi + 1) * t, :], pl.program_id(1))
            qc_ref[0] = qc[0]
            qc_ref[1] = qc[1]
        for (k0, k1), (q0, q1) in tiles(qi, ki):
            keys = slice(ki * t + k0, ki * t + k1)
            kc = jnp.concatenate([k_ref[0, keys, :], fp_ref[0, keys, :]], axis=1)
            bufs[n % 2][i, k0:k1, q0:q1] = _dot_nt(kc, qc_ref[i, q0:q1, :])

    def consume(n, i):
        qi, ki = steps[n]
        rows = slice(i * FOX_HEAD_DIM, (i + 1) * FOX_HEAD_DIM)
        for tile_no, ((k0, k1), (q0, q1)) in enumerate(tiles(qi, ki)):
            first = ki == 0 and tile_no == 0
            s = bufs[n % 2][i, k0:k1, q0:q1]
            if ki == qi:
                key = lax.broadcasted_iota(jnp.int32, s.shape, 0) + k0
                qry = lax.broadcasted_iota(jnp.int32, s.shape, 1) + q0
                s = jnp.where(qry >= key, s, MASK_VALUE)
            smax = jnp.max(s, axis=0, keepdims=True)
            m_new = smax if first else jnp.maximum(m_ref[i, :, q0:q1], smax)
            p = jnp.exp2(s - m_new)
            psum = jnp.sum(p, axis=0, keepdims=True)
            pv = _dot(vt_ref[0, rows, ki * t + k0:ki * t + k1], p.astype(BF16))
            if first:
                l_ref[i, :, q0:q1] = psum
                acc_ref[rows, q0:q1] = pv
            else:
                alpha = jnp.exp2(m_ref[i, :, q0:q1] - m_new)
                l_ref[i, :, q0:q1] = alpha * l_ref[i, :, q0:q1] + psum
                acc_ref[rows, q0:q1] = acc_ref[rows, q0:q1] * alpha + pv
            m_ref[i, :, q0:q1] = m_new
        if ki == qi and i == 1:
            out_t = jnp.concatenate(
                [acc_ref[h * FOX_HEAD_DIM:(h + 1) * FOX_HEAD_DIM, :] * (1.0 / l_ref[h]) for h in range(2)], axis=0)
            o_ref[0, qi * t:(qi + 1) * t, :] = out_t.T.astype(o_ref.dtype)

    for i in range(2):
        produce(0, i)
    for n in range(len(steps)):
        for i in range(2):
            if n + 1 < len(steps):
                produce(n + 1, i)
            consume(n, i)


def _fox_prompt(q, k, fp, vt, t):
    b, s, _ = q.shape
    seq = pl.BlockSpec((1, s, LANES), lambda bi, j: (bi, 0, j))
    return pl.pallas_call(
        functools.partial(_fox_prompt_kernel, t=t),
        out_shape=jax.ShapeDtypeStruct((b, s, FOX_W), BF16),
        grid=(b, FOX_PAIRS),
        in_specs=[seq, seq, pl.BlockSpec((1, s, LANES), lambda bi, j: (bi, 0, 0)),
                  pl.BlockSpec((1, LANES, s), lambda bi, j: (bi, j, 0))],
        out_specs=seq,
        scratch_shapes=[pltpu.VMEM((2, 1, t), F32), pltpu.VMEM((2, 1, t), F32),
                        pltpu.VMEM((LANES, t), F32), pltpu.VMEM((2, t, 2 * LANES), BF16),
                        pltpu.VMEM((2, t, t), F32), pltpu.VMEM((2, t, t), F32)],
        compiler_params=_params("arbitrary", "arbitrary"),
        name="fox_prompt",
    )(q, k, fp, vt)


def _fox_sample_kernel(q_ref, kc_ref, vc_ref, kn_ref, vn_ref, fp_ref, o_ref):
    l = q_ref.shape[1]
    past = kc_ref.shape[1]
    lane = lax.broadcasted_iota(jnp.int32, (1, LANES), 1)
    head_mask = (lane < FOX_HEAD_DIM, lane >= FOX_HEAD_DIM)
    row = lax.broadcasted_iota(jnp.int32, (l, l), 0)
    col = lax.broadcasted_iota(jnp.int32, (l, l), 1)
    fp_old = fp_ref[0, 0:past, :]
    fp_new = fp_ref[0, past:past + l, :]
    for pair in range(FOX_PAIRS):
        lanes = slice(pair * LANES, (pair + 1) * LANES)
        qc = _pair_queries(q_ref[0, :, lanes], pair)
        kc = jnp.concatenate([kc_ref[0, :, lanes].astype(BF16), fp_old], axis=1)
        kn = jnp.concatenate([kn_ref[0, :, lanes], fp_new], axis=1)
        vc = vc_ref[0, :, lanes].astype(BF16)
        vn = vn_ref[0, :, lanes]
        out = jnp.zeros((l, LANES), F32)
        for i in range(2):
            s_c = _dot_nt(qc[i], kc)
            s_n = jnp.where(row >= col, _dot_nt(qc[i], kn), MASK_VALUE)
            m = jnp.maximum(jnp.max(s_c, axis=-1, keepdims=True), jnp.max(s_n, axis=-1, keepdims=True))
            p_c = jnp.exp2(s_c - m)
            p_n = jnp.exp2(s_n - m)
            denom = jnp.sum(p_c, axis=-1, keepdims=True) + jnp.sum(p_n, axis=-1, keepdims=True)
            vch = jnp.where(head_mask[i], vc, jnp.zeros_like(vc))
            vnh = jnp.where(head_mask[i], vn, jnp.zeros_like(vn))
            out = out + (_dot(p_c.astype(BF16), vch) + _dot(p_n.astype(BF16), vnh)) * (1.0 / denom)
        o_ref[0, :, lanes] = out.astype(o_ref.dtype)


def _fox_sample(q, kn, vn, cache_k, cache_v, fp):
    b, l, _ = q.shape
    past = cache_k.shape[1]
    new = pl.BlockSpec((1, l, FOX_W), lambda bi: (bi, 0, 0))
    old = pl.BlockSpec((1, past, FOX_W), lambda bi: (bi, 0, 0))
    return pl.pallas_call(
        _fox_sample_kernel,
        out_shape=jax.ShapeDtypeStruct((b, l, FOX_W), BF16),
        grid=(b,),
        in_specs=[new, old, old, new, new,
                  pl.BlockSpec((1, fp.shape[1], LANES), lambda bi: (bi, 0, 0))],
        out_specs=new,
        compiler_params=_params("arbitrary"),
        name="fox_sample",
    )(q, cache_k, cache_v, kn, vn, fp)


def _layer_norm(x, w, b):
    mu = jnp.mean(x, axis=-1, keepdims=True)
    xc = x - mu
    var = jnp.mean(xc * xc, axis=-1, keepdims=True)
    return xc * lax.rsqrt(var + LN_EPS) * w + b


def _first_argmax_rows(x, n):
    rows = lax.broadcasted_iota(jnp.int32, x.shape, 0).astype(F32)
    mx = jnp.max(x, axis=0, keepdims=True)
    idx = jnp.min(jnp.where(x == mx, rows, float(n)), axis=0, keepdims=True)
    return mx, idx.astype(jnp.int32)


def _outproj_kernel(yr_ref, yf_ref, ga_ref, gb_ref, x_ref, mod_ref, wr_ref, wf_ref, wo_ref,
                    lw_ref, lb_ref, rw_ref, rb_ref,
                    x1_ref, h2_ref, eid_ref, gate_ref, *, alpha):
    gb_, tl, d = x_ref.shape
    if gb_ > 1:
        parts = [(slice(0, gb_ // 2), slice(None)), (slice(gb_ // 2, gb_), slice(None))]
    else:
        parts = [(slice(None), slice(0, tl // 2)), (slice(None), slice(tl // 2, tl))]
    rows = gb_ * tl // 2

    def mixed_branches(part):
        flat = lambda ref: ref[part[0], part[1], :].reshape(rows, ref.shape[-1])
        y_ret = _dot(flat(yr_ref), wr_ref[...])
        y_fox = _dot(flat(yf_ref), wf_ref[...])
        return flat(ga_ref).astype(F32) * y_ret + flat(gb_ref).astype(F32) * y_fox

    def norm_and_route(n, part, mixed):
        bs, ls = part
        mod = lambda k: mod_ref[bs, :, k * d:(k + 1) * d]
        x = x_ref[bs, ls, :]
        x1 = _layer_norm(alpha * x + (1.0 + mod(2)) * mixed.reshape(x.shape), lw_ref[...], lb_ref[...])
        x1_ref[bs, ls, :] = x1
        h2 = x1 * (1.0 + mod(4)) + mod(3)
        h2_ref[n * rows:(n + 1) * rows, :] = h2.reshape(rows, d).astype(h2_ref.dtype)
        h_hi, h_mid, _ = _split3(h2.reshape(rows, d))
        w_hi, w_mid = rw_ref[0], rw_ref[1]
        lt = _dot_nt(w_hi, h_hi) + _dot_nt(w_hi, h_mid) + _dot_nt(w_mid, h_hi) + rb_ref[...]
        gl = lt[N_EXPERTS:N_EXPERTS + N_GROUPS, :]
        gmax, gi = _first_argmax_rows(gl, N_GROUPS)
        g_p = 1.0 / jnp.sum(jnp.exp(gl - gmax), axis=0, keepdims=True)
        e_sel = lt[0:EXPERTS_PER_GROUP, :]
        for g in range(1, N_GROUPS):
            e_sel = jnp.where(gi == g, lt[g * EXPERTS_PER_GROUP:(g + 1) * EXPERTS_PER_GROUP, :], e_sel)
        sub = lax.broadcasted_iota(jnp.int32, e_sel.shape, 0)
        m1, i1 = _first_argmax_rows(e_sel, EXPERTS_PER_GROUP)
        m2, i2 = _first_argmax_rows(jnp.where(sub == i1, -jnp.inf, e_sel), EXPERTS_PER_GROUP)
        r = jnp.exp(m2 - m1)
        gate0 = g_p / (1.0 + r)
        gate1 = g_p * r / (1.0 + r)
        cols = slice(n * rows, (n + 1) * rows)
        eid_ref[0:1, cols] = gi * EXPERTS_PER_GROUP + i1
        eid_ref[1:2, cols] = gi * EXPERTS_PER_GROUP + i2
        gate_ref[0:1, cols] = gate0
        gate_ref[1:2, cols] = gate1

    mix = [mixed_branches(part) for part in parts]
    mixed = [_dot(mx.astype(BF16), wo_ref[...]) for mx in mix]
    for n, part in enumerate(parts):
        norm_and_route(n, part, mixed[n])


def _outproj(yr, yf, ga, gb, x, mod, wr, wf, wo, lw, lb, rw3, rb, gb_, tl, alpha):
    bx, l, d = x.shape
    m = bx * l
    tm = gb_ * tl
    nl = l // tl
    seq = lambda w: pl.BlockSpec((gb_, tl, w), lambda b, j: (b, j, 0))
    tok = lambda r: pl.BlockSpec((r, tm), lambda b, j: (0, b * nl + j))
    return pl.pallas_call(
        functools.partial(_outproj_kernel, alpha=alpha),
        out_shape=[jax.ShapeDtypeStruct((bx, l, d), F32), jax.ShapeDtypeStruct((m, d), BF16),
                   jax.ShapeDtypeStruct((2, m), jnp.int32), jax.ShapeDtypeStruct((2, m), F32)],
        grid=(bx // gb_, nl),
        in_specs=[seq(RET_V_W), seq(FOX_W), seq(d), seq(d), seq(d),
                  pl.BlockSpec((gb_, 1, mod.shape[-1]), lambda b, j: (b, 0, 0)),
                  _resident(wr.shape), _resident(wf.shape), _resident(wo.shape),
                  _resident(lw.shape), _resident(lb.shape), _resident(rw3.shape), _resident(rb.shape)],
        out_specs=[seq(d), pl.BlockSpec((tm, d), lambda b, j: (b * nl + j, 0)), tok(2), tok(2)],
        compiler_params=_params("arbitrary", "arbitrary"),
        name="outproj",
    )(yr, yf, ga, gb, x, mod, wr, wf, wo, lw, lb, rw3, rb)


MOE_CHUNK = 8


def _ceil_chunk(x):
    return jnp.floor((x + (MOE_CHUNK - 1.0)) * (1.0 / MOE_CHUNK)) * MOE_CHUNK


def _plan_kernel(eid_ref, triu_ref, ld_ref, pc_ref, pre_ref, carry_ref):
    @pl.when(pl.program_id(0) == 0)
    def _():
        carry_ref[...] = jnp.zeros_like(carry_ref)

    t = eid_ref.shape[1]
    experts = lax.broadcasted_iota(jnp.int32, (N_EXPERTS, t), 0)
    hit = [eid_ref[kk:kk + 1, :] == experts for kk in range(2)]
    onehot = [jnp.where(h, 1.0, 0.0) for h in hit]
    onehot_b = [o.astype(BF16) for o in onehot]
    ones = jnp.ones((8, t), BF16)
    cnt_row = (_dot_nt(ones, onehot_b[0]) + _dot_nt(ones, onehot_b[1]))[0:1, :]
    cnt0_col = jnp.sum(onehot[0], axis=1, keepdims=True)
    pc_row = _ceil_chunk(cnt_row)
    e_lane = lax.broadcasted_iota(jnp.int32, (N_EXPERTS, N_EXPERTS), 1)
    e_sub = lax.broadcasted_iota(jnp.int32, (N_EXPERTS, N_EXPERTS), 0)
    lstart_col = jnp.sum(jnp.where(e_lane < e_sub, pc_row, 0.0), axis=1, keepdims=True)
    for kk in range(2):
        before = _dot(onehot_b[kk], triu_ref[...])
        base = lstart_col + cnt0_col if kk == 1 else lstart_col
        row = jnp.sum(jnp.where(hit[kk], before + base, 0.0), axis=0, keepdims=True)
        ld_ref[kk:kk + 1, :] = row.astype(jnp.int32)
    pc_ref[0] = pc_row.astype(jnp.int32)
    pre_ref[0] = carry_ref[...].astype(jnp.int32)
    carry_ref[...] = carry_ref[...] + pc_row


def _plan(eid, t):
    m = eid.shape[1]
    n_tiles = m // t
    triu = jnp.asarray(np.triu(np.ones((t, t), np.float32), 1), BF16)
    per_tile = jax.ShapeDtypeStruct((n_tiles, 1, N_EXPERTS), jnp.int32)
    per_tile_spec = pl.BlockSpec((1, 1, N_EXPERTS), lambda i: (i, 0, 0))
    return pl.pallas_call(
        _plan_kernel,
        out_shape=[jax.ShapeDtypeStruct((2, m), jnp.int32), per_tile, per_tile],
        grid=(n_tiles,),
        in_specs=[pl.BlockSpec((2, t), lambda i: (0, i)), pl.BlockSpec((t, t), lambda i: (0, 0))],
        out_specs=[pl.BlockSpec((2, t), lambda i: (0, i)), per_tile_spec, per_tile_spec],
        scratch_shapes=[pltpu.VMEM((1, N_EXPERTS), F32)],
        compiler_params=_params("arbitrary"),
        name="moe_plan",
    )(eid, triu)


def _for_each_chunk(tile, pc_ref, goff_ref, fn):
    def per_expert(e, local):
        n = lax.div(pc_ref[tile * N_EXPERTS + e], jnp.int32(MOE_CHUNK))
        dst = goff_ref[tile * N_EXPERTS + e]

        def per_chunk(j, c):
            fn(pl.multiple_of(local + j * MOE_CHUNK, MOE_CHUNK), pl.multiple_of(dst + j * MOE_CHUNK, MOE_CHUNK))
            return c

        lax.fori_loop(0, n, per_chunk, 0)
        return local + n * MOE_CHUNK

    return lax.fori_loop(0, N_EXPERTS, per_expert, 0)


def _one_hot_rows(ld_row, lp):
    rows = lax.broadcasted_iota(jnp.int32, (lp, ld_row.shape[1]), 0)
    return rows == ld_row


HIGH_HALF = -65536


def _pack_halves(x):
    half = x.shape[1] // 2
    lo = pltpu.bitcast(x[:, :half], jnp.int32)
    hi = pltpu.bitcast(x[:, half:], jnp.int32)
    return (hi & HIGH_HALF) | lax.shift_right_logical(lo, jnp.int32(16))


def _unpack_halves(w):
    lo = pltpu.bitcast(lax.shift_left(w, jnp.int32(16)), F32)
    hi = pltpu.bitcast(w & HIGH_HALF, F32)
    return lo.astype(BF16), hi.astype(BF16)


def _round_bf16(x):
    return x.astype(BF16).astype(F32)


def _dispatch_kernel(pc_ref, goff_ref, tail_ref, ld_ref, *rest, first_tiles):
    h_refs = rest[:len(first_tiles)]
    xs_ref, buf_ref, zero_ref, rows_ref, sem, zero_sem = rest[len(first_tiles):]
    tile = pl.program_id(0)
    n_tiles = pl.num_programs(0)
    slot = tile % 2
    lp = buf_ref.shape[1]
    tb = zero_ref.shape[0]

    def copy(s, local, dst):
        return pltpu.make_async_copy(buf_ref.at[s, pl.ds(local, MOE_CHUNK)], xs_ref.at[pl.ds(dst, MOE_CHUNK)],
                                     sem.at[s])

    def fill_chunk(dst):
        return pltpu.make_async_copy(zero_ref.at[pl.ds(0, MOE_CHUNK)], xs_ref.at[pl.ds(dst, MOE_CHUNK)], zero_sem)

    def fill_block(dst):
        return pltpu.make_async_copy(zero_ref, xs_ref.at[pl.ds(dst, tb)], zero_sem)

    def for_each_fill(on_chunk, on_block):
        def per_expert(e, c):
            first = tail_ref[e]

            def per_chunk(j, cc):
                on_chunk(pl.multiple_of(first + j * MOE_CHUNK, MOE_CHUNK))
                return cc

            lax.fori_loop(0, tail_ref[N_EXPERTS + e], per_chunk, 0)
            return c

        lax.fori_loop(0, N_EXPERTS, per_expert, 0)
        first = tail_ref[2 * N_EXPERTS]

        def per_block(j, c):
            on_block(pl.multiple_of(first + j * tb, tb))
            return c

        lax.fori_loop(0, tail_ref[2 * N_EXPERTS + 1], per_block, 0)

    perm = jnp.where(_one_hot_rows(ld_ref[0:1, :], lp), 1.0,
                     jnp.where(_one_hot_rows(ld_ref[1:2, :], lp), 1.0, 0.0)).astype(BF16)
    h = h_refs[0][...]
    for first, h_ref in zip(first_tiles[1:], h_refs[1:]):
        h = jnp.where(tile >= first, h_ref[...], h)
    buf_ref[slot] = _pack_halves(_dot(perm, h))
    rows_ref[slot] = _for_each_chunk(tile, pc_ref, goff_ref, lambda a, b: copy(slot, a, b).start())

    def wait_rows(s):
        n = pl.multiple_of(rows_ref[s], MOE_CHUNK)
        pltpu.make_async_copy(buf_ref.at[s, pl.ds(0, n)], xs_ref.at[pl.ds(0, n)], sem.at[s]).wait()

    @pl.when(tile == 0)
    def _():
        zero_ref[...] = jnp.zeros_like(zero_ref)
        for_each_fill(lambda d: fill_chunk(d).start(), lambda d: fill_block(d).start())
        for_each_fill(lambda d: fill_chunk(d).wait(), lambda d: fill_block(d).wait())

    @pl.when(tile > 0)
    def _():
        wait_rows(1 - slot)

    @pl.when(tile == n_tiles - 1)
    def _():
        wait_rows(slot)


def _dispatch(pc, goff, tails, ld, h2s, n_slots, t, lp, tb):
    d = h2s[0].shape[1]
    counts = [h.shape[0] // t for h in h2s]
    first_tiles = tuple(sum(counts[:g]) for g in range(len(h2s)))

    def rows_of(g):
        return pl.BlockSpec((t, d), lambda i, *_: (jnp.clip(i - first_tiles[g], 0, counts[g] - 1), 0))

    return pl.pallas_call(
        functools.partial(_dispatch_kernel, first_tiles=first_tiles),
        out_shape=jax.ShapeDtypeStruct((n_slots, d // 2), jnp.int32),
        grid_spec=pltpu.PrefetchScalarGridSpec(
            num_scalar_prefetch=3,
            grid=(sum(counts),),
            in_specs=[pl.BlockSpec((2, t), lambda i, *_: (0, i))] + [rows_of(g) for g in range(len(h2s))],
            out_specs=pl.BlockSpec(memory_space=pl.ANY),
            scratch_shapes=[pltpu.VMEM((2, lp, d // 2), jnp.int32), pltpu.VMEM((tb, d // 2), jnp.int32),
                            pltpu.SMEM((2,), jnp.int32),
                            pltpu.SemaphoreType.DMA((2,)), pltpu.SemaphoreType.DMA]),
        compiler_params=_params("arbitrary"),
        name="moe_dispatch",
    )(pc, goff, tails, ld, *h2s)


def _expert_kernel(be_ref, nv_ref, x_ref, w1_ref, w2_ref, y_ref, w1b_ref, w2b_ref):
    i = pl.program_id(0)

    @pl.when((i == 0) | (be_ref[i] != be_ref[jnp.maximum(i - 1, 0)]))
    def _():
        w1b_ref[...] = w1_ref[0].astype(BF16)
        w2b_ref[...] = w2_ref[0].astype(BF16)

    @pl.when(i < nv_ref[0])
    def _():
        e = w2b_ref.shape[0]
        half = x_ref.shape[1]
        x_lo, x_hi = _unpack_halves(x_ref[...])
        au = _dot(x_lo, w1b_ref[:half, :]) + _dot(x_hi, w1b_ref[half:, :])
        a = au[:, :e]
        u = au[:, e:]
        y = _dot((a * _sigmoid(a) * u).astype(BF16), w2b_ref[...])
        y_ref[...] = _pack_halves(_round_bf16(y))

    @pl.when(pl.program_id(0) >= nv_ref[0])
    def _():
        y_ref[...] = jnp.zeros_like(y_ref)


def _experts(block_e, n_valid, xs, w1, w2, tb):
    p, half = xs.shape
    d = 2 * half
    e = w2.shape[1]
    rows = lambda i, be, nv: (jnp.minimum(i, nv[0] - 1), 0)
    return pl.pallas_call(
        _expert_kernel,
        out_shape=jax.ShapeDtypeStruct((p, half), jnp.int32),
        grid_spec=pltpu.PrefetchScalarGridSpec(
            num_scalar_prefetch=2,
            grid=(p // tb,),
            in_specs=[pl.BlockSpec((tb, half), rows),
                      pl.BlockSpec((1, d, 2 * e), lambda i, be, nv: (be[i], 0, 0)),
                      pl.BlockSpec((1, e, d), lambda i, be, nv: (be[i], 0, 0))],
            out_specs=pl.BlockSpec((tb, half), lambda i, be, nv: (i, 0)),
            scratch_shapes=[pltpu.VMEM((d, 2 * e), BF16), pltpu.VMEM((e, d), BF16)]),
        compiler_params=_params("arbitrary"),
        name="moe_experts",
    )(block_e, n_valid, xs, w1, w2)


def _combine_kernel(pc_ref, goff_ref, ld_ref, gate_ref, y_hbm, x1_ref, mod_ref, lw_ref, lb_ref, o_ref,
                    buf_ref, rows_ref, sem, *, alpha, tile0):
    gb_, tl, d = x1_ref.shape
    nl = pl.num_programs(1)
    step = pl.program_id(0) * nl + pl.program_id(1)
    n_steps = pl.num_programs(0) * nl
    tile = tile0 + step
    slot = step % 2
    lp = buf_ref.shape[1]

    def copy(s, local, src):
        return pltpu.make_async_copy(y_hbm.at[pl.ds(src, MOE_CHUNK)], buf_ref.at[s, pl.ds(local, MOE_CHUNK)],
                                     sem.at[s])

    @pl.when(step == 0)
    def _():
        buf_ref[...] = jnp.zeros_like(buf_ref)
        rows_ref[slot] = _for_each_chunk(tile, pc_ref, goff_ref, lambda a, b: copy(slot, a, b).start())

    @pl.when(step + 1 < n_steps)
    def _():
        rows_ref[1 - slot] = _for_each_chunk(tile + 1, pc_ref, goff_ref,
                                             lambda a, b: copy(1 - slot, a, b).start())

    n = pl.multiple_of(rows_ref[slot], MOE_CHUNK)
    pltpu.make_async_copy(y_hbm.at[pl.ds(0, n)], buf_ref.at[slot, pl.ds(0, n)], sem.at[slot]).wait()
    y_lo, y_hi = _unpack_halves(buf_ref[slot])
    zero = jnp.zeros((), F32)
    unsort = jnp.where(_one_hot_rows(ld_ref[0:1, :], lp), gate_ref[0:1, :],
                       jnp.where(_one_hot_rows(ld_ref[1:2, :], lp), gate_ref[1:2, :], zero)).astype(BF16)
    y = jnp.concatenate([_dot_tn(unsort, y_lo), _dot_tn(unsort, y_hi)], axis=1)
    g2 = mod_ref[:, :, 5 * d:6 * d]
    o_ref[...] = _layer_norm(alpha * x1_ref[...] + (1.0 + g2) * y.reshape(gb_, tl, d), lw_ref[...], lb_ref[...])


def _combine(pc, goff, ld, gate, y, x1, mod, lw, lb, gb_, tl, alpha, lp, tile0):
    bx, l, d = x1.shape
    tm = gb_ * tl
    nl = l // tl
    seq = pl.BlockSpec((gb_, tl, d), lambda b, j, pc, go: (b, j, 0))
    tok = pl.BlockSpec((2, tm), lambda b, j, pc, go: (0, tile0 + b * nl + j))
    return pl.pallas_call(
        functools.partial(_combine_kernel, alpha=alpha, tile0=tile0),
        out_shape=jax.ShapeDtypeStruct((bx, l, d), F32),
        grid_spec=pltpu.PrefetchScalarGridSpec(
            num_scalar_prefetch=2,
            grid=(bx // gb_, nl),
            in_specs=[tok, tok,
                      pl.BlockSpec(memory_space=pl.ANY),
                      seq,
                      pl.BlockSpec((gb_, 1, mod.shape[-1]), lambda b, j, pc, go: (b, 0, 0)),
                      _resident(lw.shape), _resident(lb.shape)],
            out_specs=seq,
            scratch_shapes=[pltpu.VMEM((2, lp, d // 2), jnp.int32), pltpu.SMEM((2,), jnp.int32),
                            pltpu.SemaphoreType.DMA((2,))]),
        compiler_params=_params("arbitrary", "arbitrary"),
        name="moe_combine",
    )(pc, goff, ld, gate, y, x1, mod, lw, lb)


def _moe(h2s, eid, gate, groups, w1, w2, lw, lb, alpha, t, tb):
    m = eid.shape[1]
    n_tiles = m // t
    lp = 2 * t + N_EXPERTS * MOE_CHUNK
    ld, pc, pre = _plan(eid, t)
    pc = pc.reshape(n_tiles, N_EXPERTS)
    pre = pre.reshape(n_tiles, N_EXPERTS)
    total = pre[-1] + pc[-1]
    region = (total + tb - 1) // tb * tb
    gend = jnp.cumsum(region)
    goff = (gend - region)[None, :] + pre
    n_blocks = -(-(2 * m + n_tiles * N_EXPERTS * (MOE_CHUNK - 1) + N_EXPERTS * (tb - 1)) // tb)
    block_row0 = jnp.arange(n_blocks, dtype=jnp.int32) * tb
    block_e = jnp.minimum(jnp.sum((gend[None, :] <= block_row0[:, None]).astype(jnp.int32), axis=1),
                          N_EXPERTS - 1).astype(jnp.int32)
    n_valid = (gend[-1:] // tb).astype(jnp.int32)
    tails = jnp.concatenate([gend - region + total, (region - total) // MOE_CHUNK,
                             gend[-1:], n_blocks - gend[-1:] // tb]).astype(jnp.int32)
    pc = pc.reshape(-1).astype(jnp.int32)
    goff = goff.reshape(-1).astype(jnp.int32)
    xs = _dispatch(pc, goff, tails, ld, h2s, n_blocks * tb, t, lp, tb)
    y = _experts(block_e, n_valid, xs, w1, w2, tb)
    return [_combine(pc, goff, ld, gate, y, x1, mod, lw, lb, gb_, tl, alpha, lp, tile0)
            for x1, mod, gb_, tl, tile0 in groups]


def _rotary_tables(pos, reps):
    half = RET_QK_DIM // 2
    inv = ROPE_BASE ** (-jnp.linspace(0.0, 1.0, half, dtype=F32))
    ang = pos.astype(F32)[:, None] * inv[None, :]
    cos = jnp.cos(ang)
    sin = jnp.sin(ang)
    c2 = jnp.concatenate([cos, cos], axis=1)
    s2 = jnp.concatenate([-sin, sin], axis=1)
    kscale = RET_QK_DIM ** -0.5
    tabs = (c2, s2, c2 * kscale, s2 * kscale)
    return tuple(jnp.tile(t, (reps, 1)) for t in tabs)


def _pick(n, pref):
    t = min(n, pref)
    while n % t:
        t //= 2
    return t


def _row_tile(x, cached):
    bx, l, _ = x.shape
    return (_pick(bx, max(1, 512 // l)), l) if cached else (1, _pick(l, 512))


def _mix(x, mod, pos, weights, log_gamma, cache=None):
    (w2, b2, gn_w, gn_b) = weights[:4]
    bx, l, d = x.shape
    gb_, tl = _row_tile(x, cache is not None)
    tabs = _rotary_tables(pos, gb_ if cache is not None else 1)
    rq, rk, rv, rg, fq, fk, fv, fkb, fvb, lf, ga, gb = _inproj(x, mod, w2, b2, tabs, gb_, tl)

    lft = jnp.swapaxes(lf[:, :, :FOX_HEADS], 1, 2)
    if cache is None:
        lc = _pick(l, 256)
        y_ret, state = _retention(rq, rk, rv, rg, gn_w, gn_b, log_gamma, lc)
        fp = _fcum(lft, _pick(l, 512))
        y_fox = _fox_prompt(fq, fkb, fp, fvb, _pick(l, 512))
    else:
        state0, cache_k, cache_v, cache_logf = cache
        y_ret, state = _retention(rq, rk, rv, rg, gn_w, gn_b, log_gamma, l, state0)
        past = cache_k.shape[1]
        total = -(-(past + l) // LANES) * LANES
        lft_all = jnp.concatenate(
            [jnp.swapaxes(cache_logf.astype(F32), 1, 2), lft,
             jnp.zeros((bx, FOX_HEADS, total - past - l), F32)], axis=2)
        fp = _fcum(lft_all, total)
        y_fox = _fox_sample(fq, fkb, fvb, cache_k.reshape(bx, past, FOX_W), cache_v.reshape(bx, past, FOX_W), fp)

    return (y_ret, y_fox, ga, gb), (state, fk, fv, lf[:, :, :FOX_HEADS])


def _layer(xs, mods, positions, caches, weights, log_gamma, alpha):
    (wr, wf, wo, ln1w, ln1b, rw3, rb, we1, we2, ln2w, ln2b) = weights[4:]
    tiles = [_row_tile(x, c is not None) for x, c in zip(xs, caches)]
    t = tiles[0][0] * tiles[0][1]
    assert all(gb_ * tl == t for gb_, tl in tiles), "request groups must share one MoE tile size"
    tile0, groups, h2s, eids, gates, extras = 0, [], [], [], [], []
    for x, mod, pos, cache, (gb_, tl) in zip(xs, mods, positions, caches, tiles):
        branches, extra = _mix(x, mod, pos, weights, log_gamma, cache)
        x1, h2, eid, gate = _outproj(*branches, x, mod, wr, wf, wo, ln1w, ln1b, rw3, rb, gb_, tl, alpha)
        groups.append((x1, mod, gb_, tl, tile0))
        h2s.append(h2); eids.append(eid); gates.append(gate); extras.append(extra)
        tile0 += x.shape[0] * x.shape[1] // t
    outs = _moe(h2s, jnp.concatenate(eids, axis=1), jnp.concatenate(gates, axis=1), groups,
                we1, we2, ln2w, ln2b, alpha, t, tb=512)
    return outs, extras


def kernel(x_prompt, x_sample, state_ret, cache_fox_k, cache_fox_v, cache_fox_logf, c_prompt, c_sample,
           w_ada, b_ada, w_in, b_in, ret_gn_w, ret_gn_b, w_ret_proj, w_fox_proj, w_o, ln1_w, ln1_b,
           w_rg, b_rg, w_re, b_re, w_e_in, w_e_out, ln2_w, ln2_b):
    depth = w_ada.shape[0]
    d = x_prompt.shape[-1]
    bp, s, _ = x_prompt.shape
    bs, ls, _ = x_sample.shape
    past = cache_fox_k.shape[2]
    alpha = (2 * depth) ** 0.25
    log_gamma = jnp.log1p(-jnp.exp(jnp.linspace(math.log(1.0 / 32), math.log(1.0 / 512), RET_HEADS, dtype=F32)))
    pos_p = jnp.arange(s, dtype=jnp.int32)
    pos_s = past + jnp.arange(ls, dtype=jnp.int32)
    fg = 2 * RET_QK_W + 2 * RET_V_W + 3 * FOX_W

    xp, xs = x_prompt, x_sample
    p_ret, p_k, p_v, p_f, s_ret, s_k, s_v, s_f = [], [], [], [], [], [], [], []
    for li in range(depth):
        w2 = jnp.concatenate([w_in[li][:, :fg], w_in[li][:, fg + FOX_HEADS:],
                              jnp.pad(w_in[li][:, fg:fg + FOX_HEADS], ((0, 0), (0, LANES - FOX_HEADS)))],
                             axis=1).astype(BF16)
        b2 = jnp.concatenate([b_in[li][:fg], b_in[li][fg + FOX_HEADS:],
                              jnp.pad(b_in[li][fg:fg + FOX_HEADS], (0, LANES - FOX_HEADS))]).reshape(1, -1)
        n_rt = N_EXPERTS + N_GROUPS
        rt_rows = -(-n_rt // 8) * 8
        rwt = jnp.pad(jnp.concatenate([w_re[li], w_rg[li]], axis=1).T.astype(F32), ((0, rt_rows - n_rt), (0, 0)))
        r_hi = rwt.astype(BF16)
        r_mid = (rwt - r_hi.astype(F32)).astype(BF16)
        rw3 = jnp.stack([r_hi, r_mid])
        rb = jnp.pad(jnp.concatenate([b_re[li], b_rg[li]]).astype(F32), (0, rt_rows - n_rt)).reshape(rt_rows, 1)
        weights = (w2, b2, ret_gn_w[li], ret_gn_b[li],
                   w_ret_proj[li].astype(BF16), w_fox_proj[li].astype(BF16), w_o[li].astype(BF16),
                   ln1_w[li].reshape(1, d), ln1_b[li].reshape(1, d), rw3, rb,
                   w_e_in[li], w_e_out[li],
                   ln2_w[li].reshape(1, d), ln2_b[li].reshape(1, d))
        mod = _ada(jnp.concatenate([c_prompt, c_sample], axis=0), w_ada[li], b_ada[li])
        mod_p = mod[:bp].reshape(bp, 1, 6 * d)
        mod_s = mod[bp:].reshape(bs, 1, 6 * d)
        cache = (state_ret[li].astype(F32), cache_fox_k[li], cache_fox_v[li], cache_fox_logf[li])
        (xp, xs), (extra_p, extra_s) = _layer([xp, xs], [mod_p, mod_s], [pos_p, pos_s], [None, cache],
                                              weights, log_gamma, alpha)
        st, kk, vv, ff = extra_p
        p_ret.append(st); p_k.append(kk.reshape(bp, s, FOX_HEADS, FOX_HEAD_DIM))
        p_v.append(vv.reshape(bp, s, FOX_HEADS, FOX_HEAD_DIM)); p_f.append(ff)
        st, kk, vv, ff = extra_s
        s_ret.append(st); s_k.append(kk.reshape(bs, ls, FOX_HEADS, FOX_HEAD_DIM))
        s_v.append(vv.reshape(bs, ls, FOX_HEADS, FOX_HEAD_DIM)); s_f.append(ff)
    return (xp, xs, jnp.stack(p_ret), jnp.stack(p_k), jnp.stack(p_v), jnp.stack(p_f),
            jnp.stack(s_ret), jnp.stack(s_k), jnp.stack(s_v), jnp.stack(s_f))
```

```python
import functools
import math

import jax
import jax.numpy as jnp
import numpy as np
from jax import lax
from jax.experimental import pallas as pl
from jax.experimental.pallas import tpu as pltpu

F32 = jnp.float32
BF16 = jnp.bfloat16

RET_HEADS = 4
RET_QK_DIM = 128
RET_V_DIM = 256
ROPE_BASE = 10000.0
FOX_HEADS = 16
FOX_HEAD_DIM = 64
N_GROUPS = 4
EXPERTS_PER_GROUP = 8
N_EXPERTS = N_GROUPS * EXPERTS_PER_GROUP
LN_EPS = 1e-5
GN_EPS = 1e-6
RET_QK_W = RET_HEADS * RET_QK_DIM
RET_V_W = RET_HEADS * RET_V_DIM
FOX_W = FOX_HEADS * FOX_HEAD_DIM
FOX_PAIRS = FOX_HEADS // 2
LOG2_E = math.log2(math.e)
FOX_Q_SCALE = FOX_HEAD_DIM ** -0.5 * LOG2_E

LANES = 128
VMEM_LIMIT_BYTES = 56 * 1024 * 1024
MASK_VALUE = -1e30


def _dot(a, b):
    return jnp.dot(a, b, preferred_element_type=F32)


def _dot_nt(a, b):
    return lax.dot_general(a, b, (((1,), (1,)), ((), ())), preferred_element_type=F32)


def _dot_tn(a, b):
    return lax.dot_general(a, b, (((0,), (0,)), ((), ())), preferred_element_type=F32)


def _split3(x):
    hi = x.astype(BF16)
    r1 = x - hi.astype(F32)
    mid = r1.astype(BF16)
    lo = (r1 - mid.astype(F32)).astype(BF16)
    return hi, mid, lo


def _sigmoid(x):
    return 1.0 / (1.0 + jnp.exp(-x))


def _log_sigmoid(x):
    return -(jnp.maximum(-x, 0.0) + jnp.log1p(jnp.exp(-jnp.abs(x))))


def _params(*sem):
    return pltpu.CompilerParams(dimension_semantics=sem, vmem_limit_bytes=VMEM_LIMIT_BYTES)


def _resident(shape):
    nd = len(shape)
    return pl.BlockSpec(shape, lambda *_: (0,) * nd, pipeline_mode=pl.Buffered(1))


def _ada_kernel(c_ref, w_ref, b_ref, o_ref):
    c = c_ref[...]
    s = (c * _sigmoid(c)).astype(BF16)
    o_ref[...] = _dot(s, w_ref[...].astype(BF16)) + b_ref[...]


def _ada(c, w, b):
    n, d = c.shape
    nout = w.shape[1]
    tn = d
    return pl.pallas_call(
        _ada_kernel,
        out_shape=jax.ShapeDtypeStruct((n, nout), F32),
        grid=(nout // tn,),
        in_specs=[pl.BlockSpec((n, d), lambda j: (0, 0)),
                  pl.BlockSpec((d, tn), lambda j: (0, j)),
                  pl.BlockSpec((1, tn), lambda j: (0, j))],
        out_specs=pl.BlockSpec((n, tn), lambda j: (0, j)),
        compiler_params=_params("arbitrary"),
        name="ada",
    )(c, w, b.reshape(1, nout))


def _inproj_kernel(x_ref, mod_ref, w_ref, b_ref, cq_ref, sq_ref, ck_ref, sk_ref,
                   rq_ref, rk_ref, rv_ref, rg_ref, fq_ref, fk_ref, fv_ref, fkb_ref, fvb_ref,
                   lf_ref, ga_ref, gb_ref, *, transposed_v):
    gb_, tl, d = x_ref.shape
    tm = gb_ * tl
    sh = mod_ref[:, :, 0:d]
    sc = mod_ref[:, :, d:2 * d]
    h = (x_ref[...] * (1.0 + sc) + sh).reshape(tm, d).astype(BF16)

    def proj(lo, width):
        return _dot(h, w_ref[:, lo:lo + width]) + b_ref[:, lo:lo + width]

    def put(ref, val):
        ref[...] = val.reshape(ref.shape).astype(ref.dtype)

    def rot(z, c_ref, s_ref):
        c = c_ref[...]
        s = s_ref[...]
        parts = []
        for hh in range(RET_HEADS):
            zh = z[:, hh * RET_QK_DIM:(hh + 1) * RET_QK_DIM]
            parts.append(zh * c + pltpu.roll(zh, RET_QK_DIM // 2, axis=1) * s)
        return jnp.concatenate(parts, axis=1)

    off = 0
    put(rq_ref, rot(proj(off, RET_QK_W), cq_ref, sq_ref)); off += RET_QK_W
    put(rk_ref, rot(proj(off, RET_QK_W), ck_ref, sk_ref)); off += RET_QK_W
    put(rv_ref, proj(off, RET_V_W)); off += RET_V_W
    z = proj(off, RET_V_W); off += RET_V_W
    put(rg_ref, z * _sigmoid(z))
    put(fq_ref, proj(off, FOX_W) * FOX_Q_SCALE); off += FOX_W
    z = proj(off, FOX_W); off += FOX_W
    put(fk_ref, z); put(fkb_ref, z)
    z = proj(off, FOX_W); off += FOX_W
    put(fv_ref, z)
    if transposed_v:
        fvb_ref[0] = z.T.astype(fvb_ref.dtype)
    else:
        put(fvb_ref, z)
    put(ga_ref, _sigmoid(proj(off, d))); off += d
    put(gb_ref, _sigmoid(proj(off, d))); off += d
    put(lf_ref, _log_sigmoid(proj(off, LANES)))


def _inproj(x, mod, w2, b2, tabs, gb_, tl):
    bx, l, d = x.shape
    tm = gb_ * tl
    nw = w2.shape[1]
    grid = (bx // gb_, l // tl)
    if gb_ == 1:
        tab_spec = pl.BlockSpec((tl, LANES), lambda b, j: (j, 0))
    else:
        tab_spec = pl.BlockSpec((tm, LANES), lambda b, j: (0, 0))

    def out(width, dtype):
        return (jax.ShapeDtypeStruct((bx, l, width), dtype),
                pl.BlockSpec((gb_, tl, width), lambda b, j: (b, j, 0)))

    transposed_v = gb_ == 1
    if transposed_v:
        fvb = (jax.ShapeDtypeStruct((bx, FOX_W, l), BF16), pl.BlockSpec((1, FOX_W, tl), lambda b, j: (b, 0, j)))
    else:
        fvb = out(FOX_W, BF16)
    outs = [out(RET_QK_W, BF16), out(RET_QK_W, BF16), out(RET_V_W, BF16), out(RET_V_W, BF16),
            out(FOX_W, BF16), out(FOX_W, F32), out(FOX_W, F32), out(FOX_W, BF16), fvb,
            out(LANES, F32), out(d, BF16), out(d, BF16)]
    return pl.pallas_call(
        functools.partial(_inproj_kernel, transposed_v=transposed_v),
        out_shape=[o[0] for o in outs],
        grid=grid,
        in_specs=[pl.BlockSpec((gb_, tl, d), lambda b, j: (b, j, 0)),
                  pl.BlockSpec((gb_, 1, mod.shape[-1]), lambda b, j: (b, 0, 0)),
                  _resident((d, nw)), _resident((1, nw)),
                  tab_spec, tab_spec, tab_spec, tab_spec],
        out_specs=[o[1] for o in outs],
        compiler_params=_params("arbitrary", "arbitrary"),
        name="inproj",
    )(x, mod, w2, b2, *tabs)


BIAS_PIECES = 3


def _fcum_kernel(lft_ref, triu_ref, fp_ref, carry_ref):
    @pl.when(pl.program_id(1) == 0)
    def _():
        carry_ref[...] = jnp.zeros_like(carry_ref)

    tl = lft_ref.shape[2]
    hi, mid, lo = _split3(lft_ref[0])
    triu = triu_ref[...]
    cum = _dot(hi, triu) + _dot(mid, triu) + _dot(lo, triu) + carry_ref[...]
    carry_ref[...] = cum[:, tl - 1:tl]
    padded = jnp.concatenate([cum * -LOG2_E, jnp.zeros((LANES - FOX_HEADS, tl), F32)], axis=0)
    pieces = _split3(padded.T)
    out = pieces[0].astype(F32)
    for p in range(1, BIAS_PIECES):
        out = out + pltpu.roll(pieces[p].astype(F32), p * FOX_HEADS, axis=1)
    fp_ref[0] = out.astype(fp_ref.dtype)


def _fcum(lft, tl):
    bx, _, l = lft.shape
    triu = jnp.asarray(np.triu(np.ones((tl, tl), np.float32)), BF16)
    return pl.pallas_call(
        _fcum_kernel,
        out_shape=jax.ShapeDtypeStruct((bx, l, LANES), BF16),
        grid=(bx, l // tl),
        in_specs=[pl.BlockSpec((1, FOX_HEADS, tl), lambda b, j: (b, 0, j)),
                  pl.BlockSpec((tl, tl), lambda b, j: (0, 0))],
        out_specs=pl.BlockSpec((1, tl, LANES), lambda b, j: (b, j, 0)),
        scratch_shapes=[pltpu.VMEM((FOX_HEADS, 1), F32)],
        compiler_params=_params("arbitrary", "arbitrary"),
        name="fcum",
    )(lft, triu)


def _retention_kernel(*refs, has_state):
    if has_state:
        (q_ref, k_ref, v_ref, g_ref, dec_ref, qd_ref, kd_ref, gw_ref, gb_ref, s0_ref,
         y_ref, st_ref) = refs
    else:
        (q_ref, k_ref, v_ref, g_ref, dec_ref, qd_ref, kd_ref, gw_ref, gb_ref,
         y_ref, st_ref) = refs
        s0_ref = None
    lc = q_ref.shape[1]

    @pl.when(pl.program_id(1) == 0)
    def _():
        if has_state:
            st_ref[...] = s0_ref[...]
        else:
            st_ref[...] = jnp.zeros_like(st_ref)

    for hh in range(RET_HEADS):
        qs = slice(hh * RET_QK_DIM, (hh + 1) * RET_QK_DIM)
        vs = slice(hh * RET_V_DIM, (hh + 1) * RET_V_DIM)
        q = q_ref[0, :, qs]
        k = k_ref[0, :, qs]
        v = v_ref[0, :, vs]
        state = st_ref[0, hh]
        qd = qd_ref[hh]
        scores = _dot_nt(q, k) * dec_ref[hh]
        inner = _dot(scores.astype(BF16), v)
        cross = _dot(q, state.astype(BF16)) * jnp.concatenate([qd] * (RET_V_DIM // LANES), axis=1)
        o = inner + cross
        kdec = (k.astype(F32) * kd_ref[hh]).astype(BF16)
        st_ref[0, hh] = qd[lc - 1:lc, 0:1] * state + _dot_tn(kdec, v)
        mu = jnp.mean(o, axis=-1, keepdims=True)
        oc = o - mu
        var = jnp.mean(oc * oc, axis=-1, keepdims=True)
        on = oc * lax.rsqrt(var + GN_EPS) * gw_ref[:, vs] + gb_ref[:, vs]
        y_ref[0, :, vs] = (g_ref[0, :, vs].astype(F32) * on).astype(y_ref.dtype)


def _retention_tables(log_gamma, lc):
    n = jnp.arange(lc, dtype=F32)
    diff = n[:, None] - n[None, :]
    decay = jnp.where(diff[None] >= 0, jnp.exp(jnp.maximum(diff, 0.0)[None] * log_gamma[:, None, None]), 0.0)
    qdec = jnp.exp((n + 1.0)[None, :, None] * log_gamma[:, None, None])
    kdec = jnp.exp((lc - 1.0 - n)[None, :, None] * log_gamma[:, None, None])
    lanes = (RET_HEADS, lc, LANES)
    return decay, jnp.broadcast_to(qdec, lanes), jnp.broadcast_to(kdec, lanes)


def _retention(q, k, v, g, gn_w, gn_b, log_gamma, lc, state0=None):
    bx, l, _ = q.shape
    decay, qdec, kdec = _retention_tables(log_gamma, lc)
    has_state = state0 is not None
    seq = lambda w: pl.BlockSpec((1, lc, w), lambda b, c: (b, c, 0))
    whole = lambda a: pl.BlockSpec(a.shape, lambda b, c: (0,) * a.ndim)
    st_spec = pl.BlockSpec((1, RET_HEADS, RET_QK_DIM, RET_V_DIM), lambda b, c: (b, 0, 0, 0))
    gw = gn_w.reshape(1, RET_V_W)
    gb = gn_b.reshape(1, RET_V_W)
    args = [q, k, v, g, decay, qdec, kdec, gw, gb]
    in_specs = [seq(RET_QK_W), seq(RET_QK_W), seq(RET_V_W), seq(RET_V_W),
                whole(decay), whole(qdec), whole(kdec), whole(gw), whole(gb)]
    if has_state:
        args.append(state0)
        in_specs.append(st_spec)
    return pl.pallas_call(
        functools.partial(_retention_kernel, has_state=has_state),
        out_shape=[jax.ShapeDtypeStruct((bx, l, RET_V_W), BF16),
                   jax.ShapeDtypeStruct((bx, RET_HEADS, RET_QK_DIM, RET_V_DIM), F32)],
        grid=(bx, l // lc),
        in_specs=in_specs,
        out_specs=[seq(RET_V_W), st_spec],
        compiler_params=_params("arbitrary", "arbitrary"),
        name="retention",
    )(*args)


def _pair_queries(q2, pair):
    t = q2.shape[0]
    lane = lax.broadcasted_iota(jnp.int32, (t, LANES), 1)
    out = []
    for i in range(2):
        head = (lane >= i * FOX_HEAD_DIM) & (lane < (i + 1) * FOX_HEAD_DIM)
        offset = lane - (2 * pair + i)
        ones = (offset >= 0) & (offset < BIAS_PIECES * FOX_HEADS) & ((offset & (FOX_HEADS - 1)) == 0)
        out.append(jnp.concatenate([jnp.where(head, q2, jnp.zeros_like(q2)),
                                    jnp.where(ones, 1.0, 0.0).astype(q2.dtype)], axis=1))
    return out


FOX_PAIRS_PER_STEP = 2


def _fox_prompt_kernel(q_ref, k_ref, fp_ref, vt_ref, o_ref, m_ref, l_ref, acc_ref, qc_ref, sa_ref, sb_ref, *, t):
    nq = q_ref.shape[1] // t
    pairs = q_ref.shape[2] // LANES
    steps = [(pr, qi, ki) for pr in range(pairs) for qi in range(nq) for ki in range(qi + 1)]
    bufs = (sa_ref, sb_ref)
    half = t // 2

    def tiles(qi, ki):
        if ki == qi:
            return [((0, half), (0, t)), ((half, t), (half, t))]
        return [((0, t), (0, t))]

    def produce(n, i):
        pr, qi, ki = steps[n]
        lanes = slice(pr * LANES, (pr + 1) * LANES)
        if ki == 0 and i == 0:
            qc = _pair_queries(q_ref[0, qi * t:(qi + 1) * t, lanes], pl.program_id(1) * pairs + pr)
            qc_ref[0] = qc[0]
            qc_ref[1] = qc[1]
        for (k0, k1), (q0, q1) in tiles(qi, ki):
            keys = slice(ki * t + k0, ki * t + k1)
            kc = jnp.concatenate([k_ref[0, keys, lanes], fp_ref[0, keys, :]], axis=1)
            bufs[n % 2][i, k0:k1, q0:q1] = _dot_nt(kc, qc_ref[i, q0:q1, :])

    def consume(n, i):
        pr, qi, ki = steps[n]
        rows = slice(i * FOX_HEAD_DIM, (i + 1) * FOX_HEAD_DIM)
        vrows = slice(pr * LANES + i * FOX_HEAD_DIM, pr * LANES + (i + 1) * FOX_HEAD_DIM)
        for tile_no, ((k0, k1), (q0, q1)) in enumerate(tiles(qi, ki)):
            first = ki == 0 and tile_no == 0
            s = bufs[n % 2][i, k0:k1, q0:q1]
            if ki == qi:
                key = lax.broadcasted_iota(jnp.int32, s.shape, 0) + k0
                qry = lax.broadcasted_iota(jnp.int32, s.shape, 1) + q0
                s = jnp.where(qry >= key, s, MASK_VALUE)
            smax = jnp.max(s, axis=0, keepdims=True)
            m_new = smax if first else jnp.maximum(m_ref[i, :, q0:q1], smax)
            p = jnp.exp2(s - m_new)
            psum = jnp.sum(p, axis=0, keepdims=True)
            pv = _dot(vt_ref[0, vrows, ki * t + k0:ki * t + k1], p.astype(BF16))
            if first:
                l_ref[i, :, q0:q1] = psum
                acc_ref[rows, q0:q1] = pv
            else:
                alpha = jnp.exp2(m_ref[i, :, q0:q1] - m_new)
                l_ref[i, :, q0:q1] = alpha * l_ref[i, :, q0:q1] + psum
                acc_ref[rows, q0:q1] = acc_ref[rows, q0:q1] * alpha + pv
            m_ref[i, :, q0:q1] = m_new
        if ki == qi and i == 1:
            out_t = jnp.concatenate(
                [acc_ref[h * FOX_HEAD_DIM:(h + 1) * FOX_HEAD_DIM, :] * (1.0 / l_ref[h]) for h in range(2)], axis=0)
            o_ref[0, qi * t:(qi + 1) * t, pr * LANES:(pr + 1) * LANES] = out_t.T.astype(o_ref.dtype)

    for i in range(2):
        produce(0, i)
    for n in range(len(steps)):
        for i in range(2):
            if n + 1 < len(steps):
                produce(n + 1, i)
            consume(n, i)


def _fox_prompt(q, k, fp, vt, t):
    b, s, _ = q.shape
    width = FOX_PAIRS_PER_STEP * LANES
    seq = pl.BlockSpec((1, s, width), lambda bi, j: (bi, 0, j))
    return pl.pallas_call(
        functools.partial(_fox_prompt_kernel, t=t),
        out_shape=jax.ShapeDtypeStruct((b, s, FOX_W), BF16),
        grid=(b, FOX_PAIRS // FOX_PAIRS_PER_STEP),
        in_specs=[seq, seq, pl.BlockSpec((1, s, LANES), lambda bi, j: (bi, 0, 0)),
                  pl.BlockSpec((1, width, s), lambda bi, j: (bi, j, 0))],
        out_specs=seq,
        scratch_shapes=[pltpu.VMEM((2, 1, t), F32), pltpu.VMEM((2, 1, t), F32),
                        pltpu.VMEM((LANES, t), F32), pltpu.VMEM((2, t, 2 * LANES), BF16),
                        pltpu.VMEM((2, t, t), F32), pltpu.VMEM((2, t, t), F32)],
        compiler_params=_params("arbitrary", "arbitrary"),
        name="fox_prompt",
    )(q, k, fp, vt)


def _fox_sample_kernel(q_ref, kc_ref, vc_ref, kn_ref, vn_ref, fp_ref, o_ref):
    l = q_ref.shape[1]
    past = kc_ref.shape[1]
    lane = lax.broadcasted_iota(jnp.int32, (2 * l, LANES), 1)
    qrow = lax.broadcasted_iota(jnp.int32, (2 * l, LANES), 0)
    own_lanes = (lane >= FOX_HEAD_DIM) == (qrow >= l)
    row = lax.broadcasted_iota(jnp.int32, (2 * l, l), 0)
    col = lax.broadcasted_iota(jnp.int32, (2 * l, l), 1)
    causal = jnp.where(row >= l, row - l, row) >= col
    fp_old = fp_ref[0, 0:past, :]
    fp_new = fp_ref[0, past:past + l, :]
    for pair in range(FOX_PAIRS):
        lanes = slice(pair * LANES, (pair + 1) * LANES)
        qc = jnp.concatenate(_pair_queries(q_ref[0, :, lanes], pair), axis=0)
        kc = jnp.concatenate([kc_ref[0, :, lanes].astype(BF16), fp_old], axis=1)
        kn = jnp.concatenate([kn_ref[0, :, lanes], fp_new], axis=1)
        s_c = _dot_nt(qc, kc)
        s_n = jnp.where(causal, _dot_nt(qc, kn), MASK_VALUE)
        m = jnp.maximum(jnp.max(s_c, axis=-1, keepdims=True), jnp.max(s_n, axis=-1, keepdims=True))
        p_c = jnp.exp2(s_c - m)
        p_n = jnp.exp2(s_n - m)
        denom = jnp.sum(p_c, axis=-1, keepdims=True) + jnp.sum(p_n, axis=-1, keepdims=True)
        pv = (_dot(p_c.astype(BF16), vc_ref[0, :, lanes].astype(BF16))
              + _dot(p_n.astype(BF16), vn_ref[0, :, lanes])) * (1.0 / denom)
        pv = jnp.where(own_lanes, pv, 0.0)
        o_ref[0, :, lanes] = (pv[:l, :] + pv[l:, :]).astype(o_ref.dtype)


def _fox_sample(q, kn, vn, cache_k, cache_v, fp):
    b, l, _ = q.shape
    past = cache_k.shape[1]
    new = pl.BlockSpec((1, l, FOX_W), lambda bi: (bi, 0, 0))
    old = pl.BlockSpec((1, past, FOX_W), lambda bi: (bi, 0, 0))
    return pl.pallas_call(
        _fox_sample_kernel,
        out_shape=jax.ShapeDtypeStruct((b, l, FOX_W), BF16),
        grid=(b,),
        in_specs=[new, old, old, new, new,
                  pl.BlockSpec((1, fp.shape[1], LANES), lambda bi: (bi, 0, 0))],
        out_specs=new,
        compiler_params=_params("arbitrary"),
        name="fox_sample",
    )(q, cache_k, cache_v, kn, vn, fp)


def _layer_norm(x, w, b):
    mu = jnp.mean(x, axis=-1, keepdims=True)
    xc = x - mu
    var = jnp.mean(xc * xc, axis=-1, keepdims=True)
    return xc * lax.rsqrt(var + LN_EPS) * w + b


def _first_argmax_rows(x, n):
    rows = lax.broadcasted_iota(jnp.int32, x.shape, 0).astype(F32)
    mx = jnp.max(x, axis=0, keepdims=True)
    idx = jnp.min(jnp.where(x == mx, rows, float(n)), axis=0, keepdims=True)
    return mx, idx.astype(jnp.int32)


def _outproj_kernel(yr_ref, yf_ref, ga_ref, gb_ref, x_ref, mod_ref, wr_ref, wf_ref, wo_ref,
                    lw_ref, lb_ref, rw_ref, rb_ref,
                    x1_ref, h2_ref, eid_ref, gate_ref, *, alpha):
    gb_, tl, d = x_ref.shape
    if gb_ > 1:
        parts = [(slice(0, gb_ // 2), slice(None)), (slice(gb_ // 2, gb_), slice(None))]
    else:
        parts = [(slice(None), slice(0, tl // 2)), (slice(None), slice(tl // 2, tl))]
    rows = gb_ * tl // 2

    def mixed_branches(part):
        flat = lambda ref: ref[part[0], part[1], :].reshape(rows, ref.shape[-1])
        y_ret = _dot(flat(yr_ref), wr_ref[...])
        y_fox = _dot(flat(yf_ref), wf_ref[...])
        return flat(ga_ref).astype(F32) * y_ret + flat(gb_ref).astype(F32) * y_fox

    def norm_and_route(n, part, mixed):
        bs, ls = part
        mod = lambda k: mod_ref[bs, :, k * d:(k + 1) * d]
        x = x_ref[bs, ls, :]
        x1 = _layer_norm(alpha * x + (1.0 + mod(2)) * mixed.reshape(x.shape), lw_ref[...], lb_ref[...])
        x1_ref[bs, ls, :] = x1
        h2 = x1 * (1.0 + mod(4)) + mod(3)
        h2_ref[n * rows:(n + 1) * rows, :] = h2.reshape(rows, d).astype(h2_ref.dtype)
        h_hi, h_mid, _ = _split3(h2.reshape(rows, d))
        w_hi, w_mid = rw_ref[0], rw_ref[1]
        lt = _dot_nt(w_hi, h_hi) + _dot_nt(w_hi, h_mid) + _dot_nt(w_mid, h_hi) + rb_ref[...]
        gl = lt[N_EXPERTS:N_EXPERTS + N_GROUPS, :]
        gmax, gi = _first_argmax_rows(gl, N_GROUPS)
        g_p = 1.0 / jnp.sum(jnp.exp(gl - gmax), axis=0, keepdims=True)
        e_sel = lt[0:EXPERTS_PER_GROUP, :]
        for g in range(1, N_GROUPS):
            e_sel = jnp.where(gi == g, lt[g * EXPERTS_PER_GROUP:(g + 1) * EXPERTS_PER_GROUP, :], e_sel)
        sub = lax.broadcasted_iota(jnp.int32, e_sel.shape, 0)
        m1, i1 = _first_argmax_rows(e_sel, EXPERTS_PER_GROUP)
        m2, i2 = _first_argmax_rows(jnp.where(sub == i1, -jnp.inf, e_sel), EXPERTS_PER_GROUP)
        r = jnp.exp(m2 - m1)
        gate0 = g_p / (1.0 + r)
        gate1 = g_p * r / (1.0 + r)
        cols = slice(n * rows, (n + 1) * rows)
        eid_ref[0:1, cols] = gi * EXPERTS_PER_GROUP + i1
        eid_ref[1:2, cols] = gi * EXPERTS_PER_GROUP + i2
        gate_ref[0:1, cols] = gate0
        gate_ref[1:2, cols] = gate1

    mix = [mixed_branches(part) for part in parts]
    mixed = [_dot(mx.astype(BF16), wo_ref[...]) for mx in mix]
    for n, part in enumerate(parts):
        norm_and_route(n, part, mixed[n])


def _outproj(yr, yf, ga, gb, x, mod, wr, wf, wo, lw, lb, rw3, rb, gb_, tl, alpha):
    bx, l, d = x.shape
    m = bx * l
    tm = gb_ * tl
    nl = l // tl
    seq = lambda w: pl.BlockSpec((gb_, tl, w), lambda b, j: (b, j, 0))
    tok = lambda r: pl.BlockSpec((r, tm), lambda b, j: (0, b * nl + j))
    return pl.pallas_call(
        functools.partial(_outproj_kernel, alpha=alpha),
        out_shape=[jax.ShapeDtypeStruct((bx, l, d), F32), jax.ShapeDtypeStruct((m, d), BF16),
                   jax.ShapeDtypeStruct((2, m), jnp.int32), jax.ShapeDtypeStruct((2, m), F32)],
        grid=(bx // gb_, nl),
        in_specs=[seq(RET_V_W), seq(FOX_W), seq(d), seq(d), seq(d),
                  pl.BlockSpec((gb_, 1, mod.shape[-1]), lambda b, j: (b, 0, 0)),
                  _resident(wr.shape), _resident(wf.shape), _resident(wo.shape),
                  _resident(lw.shape), _resident(lb.shape), _resident(rw3.shape), _resident(rb.shape)],
        out_specs=[seq(d), pl.BlockSpec((tm, d), lambda b, j: (b * nl + j, 0)), tok(2), tok(2)],
        compiler_params=_params("arbitrary", "arbitrary"),
        name="outproj",
    )(yr, yf, ga, gb, x, mod, wr, wf, wo, lw, lb, rw3, rb)


MOE_CHUNK = 8


def _ceil_chunk(x):
    return jnp.floor((x + (MOE_CHUNK - 1.0)) * (1.0 / MOE_CHUNK)) * MOE_CHUNK


def _plan_kernel(eid_ref, triu_ref, ld_ref, pc_ref, pre_ref, carry_ref):
    @pl.when(pl.program_id(0) == 0)
    def _():
        carry_ref[...] = jnp.zeros_like(carry_ref)

    t = eid_ref.shape[1]
    experts = lax.broadcasted_iota(jnp.int32, (N_EXPERTS, t), 0)
    hit = [eid_ref[kk:kk + 1, :] == experts for kk in range(2)]
    onehot = [jnp.where(h, 1.0, 0.0) for h in hit]
    onehot_b = [o.astype(BF16) for o in onehot]
    ones = jnp.ones((8, t), BF16)
    cnt_row = (_dot_nt(ones, onehot_b[0]) + _dot_nt(ones, onehot_b[1]))[0:1, :]
    cnt0_col = jnp.sum(onehot[0], axis=1, keepdims=True)
    pc_row = _ceil_chunk(cnt_row)
    e_lane = lax.broadcasted_iota(jnp.int32, (N_EXPERTS, N_EXPERTS), 1)
    e_sub = lax.broadcasted_iota(jnp.int32, (N_EXPERTS, N_EXPERTS), 0)
    lstart_col = jnp.sum(jnp.where(e_lane < e_sub, pc_row, 0.0), axis=1, keepdims=True)
    for kk in range(2):
        before = _dot(onehot_b[kk], triu_ref[...])
        base = lstart_col + cnt0_col if kk == 1 else lstart_col
        row = jnp.sum(jnp.where(hit[kk], before + base, 0.0), axis=0, keepdims=True)
        ld_ref[kk:kk + 1, :] = row.astype(jnp.int32)
    pc_ref[0] = pc_row.astype(jnp.int32)
    pre_ref[0] = carry_ref[...].astype(jnp.int32)
    carry_ref[...] = carry_ref[...] + pc_row


def _plan(eid, t):
    m = eid.shape[1]
    n_tiles = m // t
    triu = jnp.asarray(np.triu(np.ones((t, t), np.float32), 1), BF16)
    per_tile = jax.ShapeDtypeStruct((n_tiles, 1, N_EXPERTS), jnp.int32)
    per_tile_spec = pl.BlockSpec((1, 1, N_EXPERTS), lambda i: (i, 0, 0))
    return pl.pallas_call(
        _plan_kernel,
        out_shape=[jax.ShapeDtypeStruct((2, m), jnp.int32), per_tile, per_tile],
        grid=(n_tiles,),
        in_specs=[pl.BlockSpec((2, t), lambda i: (0, i)), pl.BlockSpec((t, t), lambda i: (0, 0))],
        out_specs=[pl.BlockSpec((2, t), lambda i: (0, i)), per_tile_spec, per_tile_spec],
        scratch_shapes=[pltpu.VMEM((1, N_EXPERTS), F32)],
        compiler_params=_params("arbitrary"),
        name="moe_plan",
    )(eid, triu)


def _for_each_chunk(tile, pc_ref, goff_ref, fn):
    def per_expert(e, local):
        n = lax.div(pc_ref[tile * N_EXPERTS + e], jnp.int32(MOE_CHUNK))
        dst = goff_ref[tile * N_EXPERTS + e]

        def per_chunk(j, c):
            fn(pl.multiple_of(local + j * MOE_CHUNK, MOE_CHUNK), pl.multiple_of(dst + j * MOE_CHUNK, MOE_CHUNK))
            return c

        lax.fori_loop(0, n, per_chunk, 0)
        return local + n * MOE_CHUNK

    return lax.fori_loop(0, N_EXPERTS, per_expert, 0)


def _one_hot_rows(ld_row, lp):
    rows = lax.broadcasted_iota(jnp.int32, (lp, ld_row.shape[1]), 0)
    return rows == ld_row


HIGH_HALF = -65536


def _pack_halves(x):
    half = x.shape[1] // 2
    lo = pltpu.bitcast(x[:, :half], jnp.int32)
    hi = pltpu.bitcast(x[:, half:], jnp.int32)
    return (hi & HIGH_HALF) | lax.shift_right_logical(lo, jnp.int32(16))


def _unpack_halves(w):
    lo = pltpu.bitcast(lax.shift_left(w, jnp.int32(16)), F32)
    hi = pltpu.bitcast(w & HIGH_HALF, F32)
    return lo.astype(BF16), hi.astype(BF16)


def _round_bf16(x):
    return x.astype(BF16).astype(F32)


def _dispatch_kernel(pc_ref, goff_ref, tail_ref, ld_ref, *rest, first_tiles):
    h_refs = rest[:len(first_tiles)]
    xs_ref, buf_ref, zero_ref, rows_ref, sem, zero_sem = rest[len(first_tiles):]
    tile = pl.program_id(0)
    n_tiles = pl.num_programs(0)
    slot = tile % 2
    lp = buf_ref.shape[1]
    tb = zero_ref.shape[0]

    def copy(s, local, dst):
        return pltpu.make_async_copy(buf_ref.at[s, pl.ds(local, MOE_CHUNK)], xs_ref.at[pl.ds(dst, MOE_CHUNK)],
                                     sem.at[s])

    def fill_chunk(dst):
        return pltpu.make_async_copy(zero_ref.at[pl.ds(0, MOE_CHUNK)], xs_ref.at[pl.ds(dst, MOE_CHUNK)], zero_sem)

    def fill_block(dst):
        return pltpu.make_async_copy(zero_ref, xs_ref.at[pl.ds(dst, tb)], zero_sem)

    def for_each_fill(on_chunk, on_block):
        def per_expert(e, c):
            first = tail_ref[e]

            def per_chunk(j, cc):
                on_chunk(pl.multiple_of(first + j * MOE_CHUNK, MOE_CHUNK))
                return cc

            lax.fori_loop(0, tail_ref[N_EXPERTS + e], per_chunk, 0)
            return c

        lax.fori_loop(0, N_EXPERTS, per_expert, 0)
        first = tail_ref[2 * N_EXPERTS]

        def per_block(j, c):
            on_block(pl.multiple_of(first + j * tb, tb))
            return c

        lax.fori_loop(0, tail_ref[2 * N_EXPERTS + 1], per_block, 0)

    perm = jnp.where(_one_hot_rows(ld_ref[0:1, :], lp), 1.0,
                     jnp.where(_one_hot_rows(ld_ref[1:2, :], lp), 1.0, 0.0)).astype(BF16)
    h = h_refs[0][...]
    for first, h_ref in zip(first_tiles[1:], h_refs[1:]):
        h = jnp.where(tile >= first, h_ref[...], h)
    buf_ref[slot] = _pack_halves(_dot(perm, h))
    rows_ref[slot] = _for_each_chunk(tile, pc_ref, goff_ref, lambda a, b: copy(slot, a, b).start())

    def wait_rows(s):
        n = pl.multiple_of(rows_ref[s], MOE_CHUNK)
        pltpu.make_async_copy(buf_ref.at[s, pl.ds(0, n)], xs_ref.at[pl.ds(0, n)], sem.at[s]).wait()

    @pl.when(tile == 0)
    def _():
        zero_ref[...] = jnp.zeros_like(zero_ref)
        for_each_fill(lambda d: fill_chunk(d).start(), lambda d: fill_block(d).start())
        for_each_fill(lambda d: fill_chunk(d).wait(), lambda d: fill_block(d).wait())

    @pl.when(tile > 0)
    def _():
        wait_rows(1 - slot)

    @pl.when(tile == n_tiles - 1)
    def _():
        wait_rows(slot)


def _dispatch(pc, goff, tails, ld, h2s, n_slots, t, lp, tb):
    d = h2s[0].shape[1]
    counts = [h.shape[0] // t for h in h2s]
    first_tiles = tuple(sum(counts[:g]) for g in range(len(h2s)))

    def rows_of(g):
        return pl.BlockSpec((t, d), lambda i, *_: (jnp.clip(i - first_tiles[g], 0, counts[g] - 1), 0))

    return pl.pallas_call(
        functools.partial(_dispatch_kernel, first_tiles=first_tiles),
        out_shape=jax.ShapeDtypeStruct((n_slots, d // 2), jnp.int32),
        grid_spec=pltpu.PrefetchScalarGridSpec(
            num_scalar_prefetch=3,
            grid=(sum(counts),),
            in_specs=[pl.BlockSpec((2, t), lambda i, *_: (0, i))] + [rows_of(g) for g in range(len(h2s))],
            out_specs=pl.BlockSpec(memory_space=pl.ANY),
            scratch_shapes=[pltpu.VMEM((2, lp, d // 2), jnp.int32), pltpu.VMEM((tb, d // 2), jnp.int32),
                            pltpu.SMEM((2,), jnp.int32),
                            pltpu.SemaphoreType.DMA((2,)), pltpu.SemaphoreType.DMA]),
        compiler_params=_params("arbitrary"),
        name="moe_dispatch",
    )(pc, goff, tails, ld, *h2s)


def _expert_kernel(be_ref, nv_ref, x_ref, w1_ref, w2_ref, y_ref, w1b_ref, w2b_ref):
    i = pl.program_id(0)

    @pl.when((i == 0) | (be_ref[i] != be_ref[jnp.maximum(i - 1, 0)]))
    def _():
        w1b_ref[...] = w1_ref[0].astype(BF16)
        w2b_ref[...] = w2_ref[0].astype(BF16)

    @pl.when(i < nv_ref[0])
    def _():
        e = w2b_ref.shape[0]
        half = x_ref.shape[1]
        x_lo, x_hi = _unpack_halves(x_ref[...])
        au = _dot(x_lo, w1b_ref[:half, :]) + _dot(x_hi, w1b_ref[half:, :])
        a = au[:, :e]
        u = au[:, e:]
        y = _dot((a * _sigmoid(a) * u).astype(BF16), w2b_ref[...])
        y_ref[...] = _pack_halves(_round_bf16(y))

    @pl.when(pl.program_id(0) >= nv_ref[0])
    def _():
        y_ref[...] = jnp.zeros_like(y_ref)


def _experts(block_e, n_valid, xs, w1, w2, tb):
    p, half = xs.shape
    d = 2 * half
    e = w2.shape[1]
    rows = lambda i, be, nv: (jnp.minimum(i, nv[0] - 1), 0)
    return pl.pallas_call(
        _expert_kernel,
        out_shape=jax.ShapeDtypeStruct((p, half), jnp.int32),
        grid_spec=pltpu.PrefetchScalarGridSpec(
            num_scalar_prefetch=2,
            grid=(p // tb,),
            in_specs=[pl.BlockSpec((tb, half), rows),
                      pl.BlockSpec((1, d, 2 * e), lambda i, be, nv: (be[i], 0, 0)),
                      pl.BlockSpec((1, e, d), lambda i, be, nv: (be[i], 0, 0))],
            out_specs=pl.BlockSpec((tb, half), lambda i, be, nv: (i, 0)),
            scratch_shapes=[pltpu.VMEM((d, 2 * e), BF16), pltpu.VMEM((e, d), BF16)]),
        compiler_params=_params("arbitrary"),
        name="moe_experts",
    )(block_e, n_valid, xs, w1, w2)


def _combine_kernel(pc_ref, goff_ref, ld_ref, gate_ref, y_hbm, x1_ref, mod_ref, lw_ref, lb_ref, o_ref,
                    buf_ref, rows_ref, sem, *, alpha, tile0):
    gb_, tl, d = x1_ref.shape
    nl = pl.num_programs(1)
    step = pl.program_id(0) * nl + pl.program_id(1)
    n_steps = pl.num_programs(0) * nl
    tile = tile0 + step
    slot = step % 2
    lp = buf_ref.shape[1]

    def copy(s, local, src):
        return pltpu.make_async_copy(y_hbm.at[pl.ds(src, MOE_CHUNK)], buf_ref.at[s, pl.ds(local, MOE_CHUNK)],
                                     sem.at[s])

    @pl.when(step == 0)
    def _():
        buf_ref[...] = jnp.zeros_like(buf_ref)
        rows_ref[slot] = _for_each_chunk(tile, pc_ref, goff_ref, lambda a, b: copy(slot, a, b).start())

    @pl.when(step + 1 < n_steps)
    def _():
        rows_ref[1 - slot] = _for_each_chunk(tile + 1, pc_ref, goff_ref,
                                             lambda a, b: copy(1 - slot, a, b).start())

    n = pl.multiple_of(rows_ref[slot], MOE_CHUNK)
    pltpu.make_async_copy(y_hbm.at[pl.ds(0, n)], buf_ref.at[slot, pl.ds(0, n)], sem.at[slot]).wait()
    y_lo, y_hi = _unpack_halves(buf_ref[slot])
    zero = jnp.zeros((), F32)
    unsort = jnp.where(_one_hot_rows(ld_ref[0:1, :], lp), gate_ref[0:1, :],
                       jnp.where(_one_hot_rows(ld_ref[1:2, :], lp), gate_ref[1:2, :], zero)).astype(BF16)
    y = jnp.concatenate([_dot_tn(unsort, y_lo), _dot_tn(unsort, y_hi)], axis=1)
    g2 = mod_ref[:, :, 5 * d:6 * d]
    o_ref[...] = _layer_norm(alpha * x1_ref[...] + (1.0 + g2) * y.reshape(gb_, tl, d), lw_ref[...], lb_ref[...])


def _combine(pc, goff, ld, gate, y, x1, mod, lw, lb, gb_, tl, alpha, lp, tile0):
    bx, l, d = x1.shape
    tm = gb_ * tl
    nl = l // tl
    seq = pl.BlockSpec((gb_, tl, d), lambda b, j, pc, go: (b, j, 0))
    tok = pl.BlockSpec((2, tm), lambda b, j, pc, go: (0, tile0 + b * nl + j))
    return pl.pallas_call(
        functools.partial(_combine_kernel, alpha=alpha, tile0=tile0),
        out_shape=jax.ShapeDtypeStruct((bx, l, d), F32),
        grid_spec=pltpu.PrefetchScalarGridSpec(
            num_scalar_prefetch=2,
            grid=(bx // gb_, nl),
            in_specs=[tok, tok,
                      pl.BlockSpec(memory_space=pl.ANY),
                      seq,
                      pl.BlockSpec((gb_, 1, mod.shape[-1]), lambda b, j, pc, go: (b, 0, 0)),
                      _resident(lw.shape), _resident(lb.shape)],
            out_specs=seq,
            scratch_shapes=[pltpu.VMEM((2, lp, d // 2), jnp.int32), pltpu.SMEM((2,), jnp.int32),
                            pltpu.SemaphoreType.DMA((2,))]),
        compiler_params=_params("arbitrary", "arbitrary"),
        name="moe_combine",
    )(pc, goff, ld, gate, y, x1, mod, lw, lb)


def _moe(h2s, eid, gate, groups, w1, w2, lw, lb, alpha, t, tb):
    m = eid.shape[1]
    n_tiles = m // t
    lp = 2 * t + N_EXPERTS * MOE_CHUNK
    ld, pc, pre = _plan(eid, t)
    pc = pc.reshape(n_tiles, N_EXPERTS)
    pre = pre.reshape(n_tiles, N_EXPERTS)
    total = pre[-1] + pc[-1]
    region = (total + tb - 1) // tb * tb
    gend = jnp.cumsum(region)
    goff = (gend - region)[None, :] + pre
    n_blocks = -(-(2 * m + n_tiles * N_EXPERTS * (MOE_CHUNK - 1) + N_EXPERTS * (tb - 1)) // tb)
    block_row0 = jnp.arange(n_blocks, dtype=jnp.int32) * tb
    block_e = jnp.minimum(jnp.sum((gend[None, :] <= block_row0[:, None]).astype(jnp.int32), axis=1),
                          N_EXPERTS - 1).astype(jnp.int32)
    n_valid = (gend[-1:] // tb).astype(jnp.int32)
    tails = jnp.concatenate([gend - region + total, (region - total) // MOE_CHUNK,
                             gend[-1:], n_blocks - gend[-1:] // tb]).astype(jnp.int32)
    pc = pc.reshape(-1).astype(jnp.int32)
    goff = goff.reshape(-1).astype(jnp.int32)
    xs = _dispatch(pc, goff, tails, ld, h2s, n_blocks * tb, t, lp, tb)
    y = _experts(block_e, n_valid, xs, w1, w2, tb)
    return [_combine(pc, goff, ld, gate, y, x1, mod, lw, lb, gb_, tl, alpha, lp, tile0)
            for x1, mod, gb_, tl, tile0 in groups]


def _rotary_tables(pos, reps):
    half = RET_QK_DIM // 2
    inv = ROPE_BASE ** (-jnp.linspace(0.0, 1.0, half, dtype=F32))
    ang = pos.astype(F32)[:, None] * inv[None, :]
    cos = jnp.cos(ang)
    sin = jnp.sin(ang)
    c2 = jnp.concatenate([cos, cos], axis=1)
    s2 = jnp.concatenate([-sin, sin], axis=1)
    kscale = RET_QK_DIM ** -0.5
    tabs = (c2, s2, c2 * kscale, s2 * kscale)
    return tuple(jnp.tile(t, (reps, 1)) for t in tabs)


def _pick(n, pref):
    t = min(n, pref)
    while n % t:
        t //= 2
    return t


def _row_tile(x, cached):
    bx, l, _ = x.shape
    return (_pick(bx, max(1, 512 // l)), l) if cached else (1, _pick(l, 512))


def _mix(x, mod, pos, weights, log_gamma, cache=None):
    (w2, b2, gn_w, gn_b) = weights[:4]
    bx, l, d = x.shape
    gb_, tl = _row_tile(x, cache is not None)
    tabs = _rotary_tables(pos, gb_ if cache is not None else 1)
    rq, rk, rv, rg, fq, fk, fv, fkb, fvb, lf, ga, gb = _inproj(x, mod, w2, b2, tabs, gb_, tl)

    lft = jnp.swapaxes(lf[:, :, :FOX_HEADS], 1, 2)
    if cache is None:
        lc = _pick(l, 512)
        y_ret, state = _retention(rq, rk, rv, rg, gn_w, gn_b, log_gamma, lc)
        fp = _fcum(lft, _pick(l, 512))
        y_fox = _fox_prompt(fq, fkb, fp, fvb, _pick(l, 512))
    else:
        state0, cache_k, cache_v, cache_logf = cache
        y_ret, state = _retention(rq, rk, rv, rg, gn_w, gn_b, log_gamma, l, state0)
        past = cache_k.shape[1]
        total = -(-(past + l) // LANES) * LANES
        lft_all = jnp.concatenate(
            [jnp.swapaxes(cache_logf.astype(F32), 1, 2), lft,
             jnp.zeros((bx, FOX_HEADS, total - past - l), F32)], axis=2)
        fp = _fcum(lft_all, total)
        y_fox = _fox_sample(fq, fkb, fvb, cache_k.reshape(bx, past, FOX_W), cache_v.reshape(bx, past, FOX_W), fp)

    return (y_ret, y_fox, ga, gb), (state, fk, fv, lf[:, :, :FOX_HEADS])


def _layer(xs, mods, positions, caches, weights, log_gamma, alpha):
    (wr, wf, wo, ln1w, ln1b, rw3, rb, we1, we2, ln2w, ln2b) = weights[4:]
    tiles = [_row_tile(x, c is not None) for x, c in zip(xs, caches)]
    t = tiles[0][0] * tiles[0][1]
    assert all(gb_ * tl == t for gb_, tl in tiles), "request groups must share one MoE tile size"
    tile0, groups, h2s, eids, gates, extras = 0, [], [], [], [], []
    for x, mod, pos, cache, (gb_, tl) in zip(xs, mods, positions, caches, tiles):
        branches, extra = _mix(x, mod, pos, weights, log_gamma, cache)
        x1, h2, eid, gate = _outproj(*branches, x, mod, wr, wf, wo, ln1w, ln1b, rw3, rb, gb_, tl, alpha)
        groups.append((x1, mod, gb_, tl, tile0))
        h2s.append(h2); eids.append(eid); gates.append(gate); extras.append(extra)
        tile0 += x.shape[0] * x.shape[1] // t
    outs = _moe(h2s, jnp.concatenate(eids, axis=1), jnp.concatenate(gates, axis=1), groups,
                we1, we2, ln2w, ln2b, alpha, t, tb=512)
    return outs, extras


def kernel(x_prompt, x_sample, state_ret, cache_fox_k, cache_fox_v, cache_fox_logf, c_prompt, c_sample,
           w_ada, b_ada, w_in, b_in, ret_gn_w, ret_gn_b, w_ret_proj, w_fox_proj, w_o, ln1_w, ln1_b,
           w_rg, b_rg, w_re, b_re, w_e_in, w_e_out, ln2_w, ln2_b):
    depth = w_ada.shape[0]
    d = x_prompt.shape[-1]
    bp, s, _ = x_prompt.shape
    bs, ls, _ = x_sample.shape
    past = cache_fox_k.shape[2]
    alpha = (2 * depth) ** 0.25
    log_gamma = jnp.log1p(-jnp.exp(jnp.linspace(math.log(1.0 / 32), math.log(1.0 / 512), RET_HEADS, dtype=F32)))
    pos_p = jnp.arange(s, dtype=jnp.int32)
    pos_s = past + jnp.arange(ls, dtype=jnp.int32)
    fg = 2 * RET_QK_W + 2 * RET_V_W + 3 * FOX_W

    xp, xs = x_prompt, x_sample
    p_ret, p_k, p_v, p_f, s_ret, s_k, s_v, s_f = [], [], [], [], [], [], [], []
    for li in range(depth):
        w2 = jnp.concatenate([w_in[li][:, :fg], w_in[li][:, fg + FOX_HEADS:],
                              jnp.pad(w_in[li][:, fg:fg + FOX_HEADS], ((0, 0), (0, LANES - FOX_HEADS)))],
                             axis=1).astype(BF16)
        b2 = jnp.concatenate([b_in[li][:fg], b_in[li][fg + FOX_HEADS:],
                              jnp.pad(b_in[li][fg:fg + FOX_HEADS], (0, LANES - FOX_HEADS))]).reshape(1, -1)
        n_rt = N_EXPERTS + N_GROUPS
        rt_rows = -(-n_rt // 8) * 8
        rwt = jnp.pad(jnp.concatenate([w_re[li], w_rg[li]], axis=1).T.astype(F32), ((0, rt_rows - n_rt), (0, 0)))
        r_hi = rwt.astype(BF16)
        r_mid = (rwt - r_hi.astype(F32)).astype(BF16)
        rw3 = jnp.stack([r_hi, r_mid])
        rb = jnp.pad(jnp.concatenate([b_re[li], b_rg[li]]).astype(F32), (0, rt_rows - n_rt)).reshape(rt_rows, 1)
        weights = (w2, b2, ret_gn_w[li], ret_gn_b[li],
                   w_ret_proj[li].astype(BF16), w_fox_proj[li].astype(BF16), w_o[li].astype(BF16),
                   ln1_w[li].reshape(1, d), ln1_b[li].reshape(1, d), rw3, rb,
                   w_e_in[li], w_e_out[li],
                   ln2_w[li].reshape(1, d), ln2_b[li].reshape(1, d))
        mod = _ada(jnp.concatenate([c_prompt, c_sample], axis=0), w_ada[li], b_ada[li])
        mod_p = mod[:bp].reshape(bp, 1, 6 * d)
        mod_s = mod[bp:].reshape(bs, 1, 6 * d)
        cache = (state_ret[li].astype(F32), cache_fox_k[li], cache_fox_v[li], cache_fox_logf[li])
        (xp, xs), (extra_p, extra_s) = _layer([xp, xs], [mod_p, mod_s], [pos_p, pos_s], [None, cache],
                                              weights, log_gamma, alpha)
        st, kk, vv, ff = extra_p
        p_ret.append(st); p_k.append(kk.reshape(bp, s, FOX_HEADS, FOX_HEAD_DIM))
        p_v.append(vv.reshape(bp, s, FOX_HEADS, FOX_HEAD_DIM)); p_f.append(ff)
        st, kk, vv, ff = extra_s
        s_ret.append(st); s_k.append(kk.reshape(bs, ls, FOX_HEADS, FOX_HEAD_DIM))
        s_v.append(vv.reshape(bs, ls, FOX_HEADS, FOX_HEAD_DIM)); s_f.append(ff)
    return (xp, xs, jnp.stack(p_ret), jnp.stack(p_k), jnp.stack(p_v), jnp.stack(p_f),
            jnp.stack(s_ret), jnp.stack(s_k), jnp.stack(s_v), jnp.stack(s_f))
```

```python
import functools
import math

import jax
import jax.numpy as jnp
import numpy as np
from jax import lax
from jax.experimental import pallas as pl
from jax.experimental.pallas import tpu as pltpu

F32 = jnp.float32
BF16 = jnp.bfloat16

RET_HEADS = 4
RET_QK_DIM = 128
RET_V_DIM = 256
ROPE_BASE = 10000.0
FOX_HEADS = 16
FOX_HEAD_DIM = 64
N_GROUPS = 4
EXPERTS_PER_GROUP = 8
N_EXPERTS = N_GROUPS * EXPERTS_PER_GROUP
LN_EPS = 1e-5
GN_EPS = 1e-6
RET_QK_W = RET_HEADS * RET_QK_DIM
RET_V_W = RET_HEADS * RET_V_DIM
FOX_W = FOX_HEADS * FOX_HEAD_DIM
FOX_PAIRS = FOX_HEADS // 2
LOG2_E = math.log2(math.e)
FOX_Q_SCALE = FOX_HEAD_DIM ** -0.5 * LOG2_E

LANES = 128
VMEM_LIMIT_BYTES = 56 * 1024 * 1024
MASK_VALUE = -1e30

ROW_TILE = 512
ATTN_TILE = 512
FCUM_TILE = 1024
RET_CHUNK = 512
EXPERT_ROWS = 512


def _dot(a, b):
    return jnp.dot(a, b, preferred_element_type=F32)


def _dot_nt(a, b):
    return lax.dot_general(a, b, (((1,), (1,)), ((), ())), preferred_element_type=F32)


def _dot_tn(a, b):
    return lax.dot_general(a, b, (((0,), (0,)), ((), ())), preferred_element_type=F32)


def _split3(x):
    hi = x.astype(BF16)
    r1 = x - hi.astype(F32)
    mid = r1.astype(BF16)
    lo = (r1 - mid.astype(F32)).astype(BF16)
    return hi, mid, lo


def _sigmoid(x):
    return 1.0 / (1.0 + jnp.exp(-x))


def _log_sigmoid(x):
    return -(jnp.maximum(-x, 0.0) + jnp.log1p(jnp.exp(-jnp.abs(x))))


def _params(*sem):
    return pltpu.CompilerParams(dimension_semantics=sem, vmem_limit_bytes=VMEM_LIMIT_BYTES)


def _resident(shape):
    nd = len(shape)
    return pl.BlockSpec(shape, lambda *_: (0,) * nd, pipeline_mode=pl.Buffered(1))


def _ada_kernel(c_ref, w_ref, b_ref, o_ref):
    c = c_ref[...]
    s = (c * _sigmoid(c)).astype(BF16)
    o_ref[...] = _dot(s, w_ref[...].astype(BF16)) + b_ref[...]


def _ada(c, w, b):
    n, d = c.shape
    nout = w.shape[1]
    tn = d
    return pl.pallas_call(
        _ada_kernel,
        out_shape=jax.ShapeDtypeStruct((n, nout), F32),
        grid=(nout // tn,),
        in_specs=[pl.BlockSpec((n, d), lambda j: (0, 0)),
                  pl.BlockSpec((d, tn), lambda j: (0, j)),
                  pl.BlockSpec((1, tn), lambda j: (0, j))],
        out_specs=pl.BlockSpec((n, tn), lambda j: (0, j)),
        compiler_params=_params("arbitrary"),
        name="ada",
    )(c, w, b.reshape(1, nout))


def _inproj_kernel(x_ref, mod_ref, w_ref, b_ref, cq_ref, sq_ref, ck_ref, sk_ref,
                   rq_ref, rk_ref, rv_ref, rg_ref, fq_ref, fk_ref, fv_ref, fkb_ref, fvb_ref,
                   lf_ref, ga_ref, gb_ref, *, transposed_v):
    gb_, tl, d = x_ref.shape
    tm = gb_ * tl
    sh = mod_ref[:, :, 0:d]
    sc = mod_ref[:, :, d:2 * d]
    h = (x_ref[...] * (1.0 + sc) + sh).reshape(tm, d).astype(BF16)

    def proj(lo, width):
        return _dot(h, w_ref[:, lo:lo + width]) + b_ref[:, lo:lo + width]

    def put(ref, val):
        ref[...] = val.reshape(ref.shape).astype(ref.dtype)

    def rot(z, c_ref, s_ref):
        c = c_ref[...]
        s = s_ref[...]
        parts = []
        for hh in range(RET_HEADS):
            zh = z[:, hh * RET_QK_DIM:(hh + 1) * RET_QK_DIM]
            parts.append(zh * c + pltpu.roll(zh, RET_QK_DIM // 2, axis=1) * s)
        return jnp.concatenate(parts, axis=1)

    off = 0
    put(rq_ref, rot(proj(off, RET_QK_W), cq_ref, sq_ref)); off += RET_QK_W
    put(rk_ref, rot(proj(off, RET_QK_W), ck_ref, sk_ref)); off += RET_QK_W
    put(rv_ref, proj(off, RET_V_W)); off += RET_V_W
    z = proj(off, RET_V_W); off += RET_V_W
    put(rg_ref, z * _sigmoid(z))
    put(fq_ref, proj(off, FOX_W) * FOX_Q_SCALE); off += FOX_W
    z = proj(off, FOX_W); off += FOX_W
    put(fk_ref, z); put(fkb_ref, z)
    z = proj(off, FOX_W); off += FOX_W
    put(fv_ref, z)
    if transposed_v:
        fvb_ref[0] = z.T.astype(fvb_ref.dtype)
    else:
        put(fvb_ref, z)
    put(ga_ref, _sigmoid(proj(off, d))); off += d
    put(gb_ref, _sigmoid(proj(off, d))); off += d
    put(lf_ref, _log_sigmoid(proj(off, LANES)))


def _inproj(x, mod, w2, b2, tabs, gb_, tl):
    bx, l, d = x.shape
    tm = gb_ * tl
    nw = w2.shape[1]
    grid = (bx // gb_, l // tl)
    if gb_ == 1:
        tab_spec = pl.BlockSpec((tl, LANES), lambda b, j: (j, 0))
    else:
        tab_spec = pl.BlockSpec((tm, LANES), lambda b, j: (0, 0))

    def out(width, dtype):
        return (jax.ShapeDtypeStruct((bx, l, width), dtype),
                pl.BlockSpec((gb_, tl, width), lambda b, j: (b, j, 0)))

    transposed_v = gb_ == 1
    if transposed_v:
        fvb = (jax.ShapeDtypeStruct((bx, FOX_W, l), BF16), pl.BlockSpec((1, FOX_W, tl), lambda b, j: (b, 0, j)))
    else:
        fvb = out(FOX_W, BF16)
    outs = [out(RET_QK_W, BF16), out(RET_QK_W, BF16), out(RET_V_W, BF16), out(RET_V_W, BF16),
            out(FOX_W, BF16), out(FOX_W, F32), out(FOX_W, F32), out(FOX_W, BF16), fvb,
            out(LANES, F32), out(d, BF16), out(d, BF16)]
    return pl.pallas_call(
        functools.partial(_inproj_kernel, transposed_v=transposed_v),
        out_shape=[o[0] for o in outs],
        grid=grid,
        in_specs=[pl.BlockSpec((gb_, tl, d), lambda b, j: (b, j, 0)),
                  pl.BlockSpec((gb_, 1, mod.shape[-1]), lambda b, j: (b, 0, 0)),
                  _resident((d, nw)), _resident((1, nw)),
                  tab_spec, tab_spec, tab_spec, tab_spec],
        out_specs=[o[1] for o in outs],
        compiler_params=_params("arbitrary", "arbitrary"),
        name="inproj",
    )(x, mod, w2, b2, *tabs)


BIAS_PIECES = 3


def _fcum_kernel(lft_ref, triu_ref, fp_ref, carry_ref):
    @pl.when(pl.program_id(1) == 0)
    def _():
        carry_ref[...] = jnp.zeros_like(carry_ref)

    tl = lft_ref.shape[2]
    hi, mid, lo = _split3(lft_ref[0])
    triu = triu_ref[...]
    cum = _dot(hi, triu) + _dot(mid, triu) + _dot(lo, triu) + carry_ref[...]
    carry_ref[...] = cum[:, tl - 1:tl]
    padded = jnp.concatenate([cum * -LOG2_E, jnp.zeros((LANES - FOX_HEADS, tl), F32)], axis=0)
    pieces = _split3(padded.T)
    out = pieces[0].astype(F32)
    for p in range(1, BIAS_PIECES):
        out = out + pltpu.roll(pieces[p].astype(F32), p * FOX_HEADS, axis=1)
    fp_ref[0] = out.astype(fp_ref.dtype)


def _fcum(lft, tl):
    bx, _, l = lft.shape
    triu = jnp.asarray(np.triu(np.ones((tl, tl), np.float32)), BF16)
    return pl.pallas_call(
        _fcum_kernel,
        out_shape=jax.ShapeDtypeStruct((bx, l, LANES), BF16),
        grid=(bx, l // tl),
        in_specs=[pl.BlockSpec((1, FOX_HEADS, tl), lambda b, j: (b, 0, j)),
                  pl.BlockSpec((tl, tl), lambda b, j: (0, 0))],
        out_specs=pl.BlockSpec((1, tl, LANES), lambda b, j: (b, j, 0)),
        scratch_shapes=[pltpu.VMEM((FOX_HEADS, 1), F32)],
        compiler_params=_params("arbitrary", "arbitrary"),
        name="fcum",
    )(lft, triu)


def _retention_kernel(*refs, has_state):
    if has_state:
        (q_ref, k_ref, v_ref, g_ref, dec_ref, qd_ref, kd_ref, gw_ref, gb_ref, s0_ref,
         y_ref, st_ref) = refs
    else:
        (q_ref, k_ref, v_ref, g_ref, dec_ref, qd_ref, kd_ref, gw_ref, gb_ref,
         y_ref, st_ref) = refs
        s0_ref = None
    lc = q_ref.shape[1]

    @pl.when(pl.program_id(1) == 0)
    def _():
        if has_state:
            st_ref[...] = s0_ref[...]
        else:
            st_ref[...] = jnp.zeros_like(st_ref)

    for hh in range(RET_HEADS):
        qs = slice(hh * RET_QK_DIM, (hh + 1) * RET_QK_DIM)
        vs = slice(hh * RET_V_DIM, (hh + 1) * RET_V_DIM)
        q = q_ref[0, :, qs]
        k = k_ref[0, :, qs]
        v = v_ref[0, :, vs]
        state = st_ref[0, hh]
        qd = qd_ref[hh]
        scores = _dot_nt(q, k) * dec_ref[hh]
        inner = _dot(scores.astype(BF16), v)
        cross = _dot(q, state.astype(BF16)) * jnp.concatenate([qd] * (RET_V_DIM // LANES), axis=1)
        o = inner + cross
        kdec = (k.astype(F32) * kd_ref[hh]).astype(BF16)
        st_ref[0, hh] = qd[lc - 1:lc, 0:1] * state + _dot_tn(kdec, v)
        mu = jnp.mean(o, axis=-1, keepdims=True)
        oc = o - mu
        var = jnp.mean(oc * oc, axis=-1, keepdims=True)
        on = oc * lax.rsqrt(var + GN_EPS) * gw_ref[:, vs] + gb_ref[:, vs]
        y_ref[0, :, vs] = (g_ref[0, :, vs].astype(F32) * on).astype(y_ref.dtype)


def _retention_tables(log_gamma, lc):
    n = jnp.arange(lc, dtype=F32)
    diff = n[:, None] - n[None, :]
    decay = jnp.where(diff[None] >= 0, jnp.exp(jnp.maximum(diff, 0.0)[None] * log_gamma[:, None, None]), 0.0)
    qdec = jnp.exp((n + 1.0)[None, :, None] * log_gamma[:, None, None])
    kdec = jnp.exp((lc - 1.0 - n)[None, :, None] * log_gamma[:, None, None])
    lanes = (RET_HEADS, lc, LANES)
    return decay, jnp.broadcast_to(qdec, lanes), jnp.broadcast_to(kdec, lanes)


def _retention(q, k, v, g, gn_w, gn_b, log_gamma, lc, state0=None):
    bx, l, _ = q.shape
    decay, qdec, kdec = _retention_tables(log_gamma, lc)
    has_state = state0 is not None
    seq = lambda w: pl.BlockSpec((1, lc, w), lambda b, c: (b, c, 0))
    whole = lambda a: pl.BlockSpec(a.shape, lambda b, c: (0,) * a.ndim)
    st_spec = pl.BlockSpec((1, RET_HEADS, RET_QK_DIM, RET_V_DIM), lambda b, c: (b, 0, 0, 0))
    gw = gn_w.reshape(1, RET_V_W)
    gb = gn_b.reshape(1, RET_V_W)
    args = [q, k, v, g, decay, qdec, kdec, gw, gb]
    in_specs = [seq(RET_QK_W), seq(RET_QK_W), seq(RET_V_W), seq(RET_V_W),
                whole(decay), whole(qdec), whole(kdec), whole(gw), whole(gb)]
    if has_state:
        args.append(state0)
        in_specs.append(st_spec)
    return pl.pallas_call(
        functools.partial(_retention_kernel, has_state=has_state),
        out_shape=[jax.ShapeDtypeStruct((bx, l, RET_V_W), BF16),
                   jax.ShapeDtypeStruct((bx, RET_HEADS, RET_QK_DIM, RET_V_DIM), F32)],
        grid=(bx, l // lc),
        in_specs=in_specs,
        out_specs=[seq(RET_V_W), st_spec],
        compiler_params=_params("arbitrary", "arbitrary"),
        name="retention",
    )(*args)


def _pair_queries(q2, pair):
    t = q2.shape[0]
    lane = lax.broadcasted_iota(jnp.int32, (t, LANES), 1)
    out = []
    for i in range(2):
        head = (lane >= i * FOX_HEAD_DIM) & (lane < (i + 1) * FOX_HEAD_DIM)
        offset = lane - (2 * pair + i)
        ones = (offset >= 0) & (offset < BIAS_PIECES * FOX_HEADS) & ((offset & (FOX_HEADS - 1)) == 0)
        out.append(jnp.concatenate([jnp.where(head, q2, jnp.zeros_like(q2)),
                                    jnp.where(ones, 1.0, 0.0).astype(q2.dtype)], axis=1))
    return out


FOX_PAIRS_PER_STEP = 2


def _fox_prompt_kernel(q_ref, k_ref, fp_ref, vt_ref, o_ref, m_ref, l_ref, acc_ref, qc_ref, sa_ref, sb_ref, *, t):
    nq = q_ref.shape[1] // t
    pairs = q_ref.shape[2] // LANES
    steps = [(pr, qi, ki) for pr in range(pairs) for qi in range(nq) for ki in range(qi + 1)]
    bufs = (sa_ref, sb_ref)
    half = t // 2

    def tiles(qi, ki):
        if ki == qi:
            return [((0, half), (0, t)), ((half, t), (half, t))]
        return [((0, t), (0, t))]

    def produce(n, i):
        pr, qi, ki = steps[n]
        lanes = slice(pr * LANES, (pr + 1) * LANES)
        if ki == 0 and i == 0:
            qc = _pair_queries(q_ref[0, qi * t:(qi + 1) * t, lanes], pl.program_id(1) * pairs + pr)
            qc_ref[0] = qc[0]
            qc_ref[1] = qc[1]
        for (k0, k1), (q0, q1) in tiles(qi, ki):
            keys = slice(ki * t + k0, ki * t + k1)
            kc = jnp.concatenate([k_ref[0, keys, lanes], fp_ref[0, keys, :]], axis=1)
            bufs[n % 2][i, k0:k1, q0:q1] = _dot_nt(kc, qc_ref[i, q0:q1, :])

    def consume(n, i):
        pr, qi, ki = steps[n]
        rows = slice(i * FOX_HEAD_DIM, (i + 1) * FOX_HEAD_DIM)
        vrows = slice(pr * LANES + i * FOX_HEAD_DIM, pr * LANES + (i + 1) * FOX_HEAD_DIM)
        for tile_no, ((k0, k1), (q0, q1)) in enumerate(tiles(qi, ki)):
            first = ki == 0 and tile_no == 0
            s = bufs[n % 2][i, k0:k1, q0:q1]
            if ki == qi:
                key = lax.broadcasted_iota(jnp.int32, s.shape, 0) + k0
                qry = lax.broadcasted_iota(jnp.int32, s.shape, 1) + q0
                s = jnp.where(qry >= key, s, MASK_VALUE)
            smax = jnp.max(s, axis=0, keepdims=True)
            m_new = smax if first else jnp.maximum(m_ref[i, :, q0:q1], smax)
            p = jnp.exp2(s - m_new)
            psum = jnp.sum(p, axis=0, keepdims=True)
            pv = _dot(vt_ref[0, vrows, ki * t + k0:ki * t + k1], p.astype(BF16))
            if first:
                l_ref[i, :, q0:q1] = psum
                acc_ref[rows, q0:q1] = pv
            else:
                alpha = jnp.exp2(m_ref[i, :, q0:q1] - m_new)
                l_ref[i, :, q0:q1] = alpha * l_ref[i, :, q0:q1] + psum
                acc_ref[rows, q0:q1] = acc_ref[rows, q0:q1] * alpha + pv
            m_ref[i, :, q0:q1] = m_new
        if ki == qi and i == 1:
            out_t = jnp.concatenate(
                [acc_ref[h * FOX_HEAD_DIM:(h + 1) * FOX_HEAD_DIM, :] * (1.0 / l_ref[h]) for h in range(2)], axis=0)
            o_ref[0, qi * t:(qi + 1) * t, pr * LANES:(pr + 1) * LANES] = out_t.T.astype(o_ref.dtype)

    for i in range(2):
        produce(0, i)
    for n in range(len(steps)):
        for i in range(2):
            if n + 1 < len(steps):
                produce(n + 1, i)
            consume(n, i)


def _fox_prompt(q, k, fp, vt, t):
    b, s, _ = q.shape
    width = FOX_PAIRS_PER_STEP * LANES
    seq = pl.BlockSpec((1, s, width), lambda bi, j: (bi, 0, j))
    return pl.pallas_call(
        functools.partial(_fox_prompt_kernel, t=t),
        out_shape=jax.ShapeDtypeStruct((b, s, FOX_W), BF16),
        grid=(b, FOX_PAIRS // FOX_PAIRS_PER_STEP),
        in_specs=[seq, seq, pl.BlockSpec((1, s, LANES), lambda bi, j: (bi, 0, 0)),
                  pl.BlockSpec((1, width, s), lambda bi, j: (bi, j, 0))],
        out_specs=seq,
        scratch_shapes=[pltpu.VMEM((2, 1, t), F32), pltpu.VMEM((2, 1, t), F32),
                        pltpu.VMEM((LANES, t), F32), pltpu.VMEM((2, t, 2 * LANES), BF16),
                        pltpu.VMEM((2, t, t), F32), pltpu.VMEM((2, t, t), F32)],
        compiler_params=_params("arbitrary", "arbitrary"),
        name="fox_prompt",
    )(q, k, fp, vt)


def _fox_sample_kernel(q_ref, kc_ref, vc_ref, kn_ref, vn_ref, fp_ref, o_ref):
    l = q_ref.shape[1]
    past = kc_ref.shape[1]
    lane = lax.broadcasted_iota(jnp.int32, (2 * l, LANES), 1)
    qrow = lax.broadcasted_iota(jnp.int32, (2 * l, LANES), 0)
    own_lanes = (lane >= FOX_HEAD_DIM) == (qrow >= l)
    row = lax.broadcasted_iota(jnp.int32, (2 * l, l), 0)
    col = lax.broadcasted_iota(jnp.int32, (2 * l, l), 1)
    causal = jnp.where(row >= l, row - l, row) >= col
    fp_old = fp_ref[0, 0:past, :]
    fp_new = fp_ref[0, past:past + l, :]
    for pair in range(FOX_PAIRS):
        lanes = slice(pair * LANES, (pair + 1) * LANES)
        qc = jnp.concatenate(_pair_queries(q_ref[0, :, lanes], pair), axis=0)
        kc = jnp.concatenate([kc_ref[0, :, lanes].astype(BF16), fp_old], axis=1)
        kn = jnp.concatenate([kn_ref[0, :, lanes], fp_new], axis=1)
        s_c = _dot_nt(qc, kc)
        s_n = jnp.where(causal, _dot_nt(qc, kn), MASK_VALUE)
        m = jnp.maximum(jnp.max(s_c, axis=-1, keepdims=True), jnp.max(s_n, axis=-1, keepdims=True))
        p_c = jnp.exp2(s_c - m)
        p_n = jnp.exp2(s_n - m)
        denom = jnp.sum(p_c, axis=-1, keepdims=True) + jnp.sum(p_n, axis=-1, keepdims=True)
        pv = (_dot(p_c.astype(BF16), vc_ref[0, :, lanes].astype(BF16))
              + _dot(p_n.astype(BF16), vn_ref[0, :, lanes])) * (1.0 / denom)
        pv = jnp.where(own_lanes, pv, 0.0)
        o_ref[0, :, lanes] = (pv[:l, :] + pv[l:, :]).astype(o_ref.dtype)


def _fox_sample(q, kn, vn, cache_k, cache_v, fp):
    b, l, _ = q.shape
    past = cache_k.shape[1]
    new = pl.BlockSpec((1, l, FOX_W), lambda bi: (bi, 0, 0))
    old = pl.BlockSpec((1, past, FOX_W), lambda bi: (bi, 0, 0))
    return pl.pallas_call(
        _fox_sample_kernel,
        out_shape=jax.ShapeDtypeStruct((b, l, FOX_W), BF16),
        grid=(b,),
        in_specs=[new, old, old, new, new,
                  pl.BlockSpec((1, fp.shape[1], LANES), lambda bi: (bi, 0, 0))],
        out_specs=new,
        compiler_params=_params("arbitrary"),
        name="fox_sample",
    )(q, cache_k, cache_v, kn, vn, fp)


def _layer_norm(x, w, b):
    mu = jnp.mean(x, axis=-1, keepdims=True)
    xc = x - mu
    var = jnp.mean(xc * xc, axis=-1, keepdims=True)
    return xc * lax.rsqrt(var + LN_EPS) * w + b


def _first_argmax_rows(x, n):
    rows = lax.broadcasted_iota(jnp.int32, x.shape, 0).astype(F32)
    mx = jnp.max(x, axis=0, keepdims=True)
    idx = jnp.min(jnp.where(x == mx, rows, float(n)), axis=0, keepdims=True)
    return mx, idx.astype(jnp.int32)


def _outproj_kernel(yr_ref, yf_ref, ga_ref, gb_ref, x_ref, mod_ref, wr_ref, wf_ref, wo_ref,
                    lw_ref, lb_ref, rw_ref, rb_ref,
                    x1_ref, h2_ref, eid_ref, gate_ref, *, alpha):
    gb_, tl, d = x_ref.shape
    if gb_ > 1:
        parts = [(slice(0, gb_ // 2), slice(None)), (slice(gb_ // 2, gb_), slice(None))]
    else:
        parts = [(slice(None), slice(0, tl // 2)), (slice(None), slice(tl // 2, tl))]
    rows = gb_ * tl // 2

    def mixed_branches(part):
        flat = lambda ref: ref[part[0], part[1], :].reshape(rows, ref.shape[-1])
        y_ret = _dot(flat(yr_ref), wr_ref[...])
        y_fox = _dot(flat(yf_ref), wf_ref[...])
        return flat(ga_ref).astype(F32) * y_ret + flat(gb_ref).astype(F32) * y_fox

    def norm_and_route(n, part, mixed):
        bs, ls = part
        mod = lambda k: mod_ref[bs, :, k * d:(k + 1) * d]
        x = x_ref[bs, ls, :]
        x1 = _layer_norm(alpha * x + (1.0 + mod(2)) * mixed.reshape(x.shape), lw_ref[...], lb_ref[...])
        x1_ref[bs, ls, :] = x1
        h2 = x1 * (1.0 + mod(4)) + mod(3)
        h2_ref[n * rows:(n + 1) * rows, :] = h2.reshape(rows, d).astype(h2_ref.dtype)
        h_hi, h_mid, _ = _split3(h2.reshape(rows, d))
        w_hi, w_mid = rw_ref[0], rw_ref[1]
        lt = _dot_nt(w_hi, h_hi) + _dot_nt(w_hi, h_mid) + _dot_nt(w_mid, h_hi) + rb_ref[...]
        gl = lt[N_EXPERTS:N_EXPERTS + N_GROUPS, :]
        gmax, gi = _first_argmax_rows(gl, N_GROUPS)
        g_p = 1.0 / jnp.sum(jnp.exp(gl - gmax), axis=0, keepdims=True)
        e_sel = lt[0:EXPERTS_PER_GROUP, :]
        for g in range(1, N_GROUPS):
            e_sel = jnp.where(gi == g, lt[g * EXPERTS_PER_GROUP:(g + 1) * EXPERTS_PER_GROUP, :], e_sel)
        sub = lax.broadcasted_iota(jnp.int32, e_sel.shape, 0)
        m1, i1 = _first_argmax_rows(e_sel, EXPERTS_PER_GROUP)
        m2, i2 = _first_argmax_rows(jnp.where(sub == i1, -jnp.inf, e_sel), EXPERTS_PER_GROUP)
        r = jnp.exp(m2 - m1)
        gate0 = g_p / (1.0 + r)
        gate1 = g_p * r / (1.0 + r)
        cols = slice(n * rows, (n + 1) * rows)
        eid_ref[0:1, cols] = gi * EXPERTS_PER_GROUP + i1
        eid_ref[1:2, cols] = gi * EXPERTS_PER_GROUP + i2
        gate_ref[0:1, cols] = gate0
        gate_ref[1:2, cols] = gate1

    mix = [mixed_branches(part) for part in parts]
    mixed = [_dot(mx.astype(BF16), wo_ref[...]) for mx in mix]
    for n, part in enumerate(parts):
        norm_and_route(n, part, mixed[n])


def _outproj(yr, yf, ga, gb, x, mod, wr, wf, wo, lw, lb, rw3, rb, gb_, tl, alpha):
    bx, l, d = x.shape
    m = bx * l
    tm = gb_ * tl
    nl = l // tl
    seq = lambda w: pl.BlockSpec((gb_, tl, w), lambda b, j: (b, j, 0))
    tok = lambda r: pl.BlockSpec((r, tm), lambda b, j: (0, b * nl + j))
    return pl.pallas_call(
        functools.partial(_outproj_kernel, alpha=alpha),
        out_shape=[jax.ShapeDtypeStruct((bx, l, d), F32), jax.ShapeDtypeStruct((m, d), BF16),
                   jax.ShapeDtypeStruct((2, m), jnp.int32), jax.ShapeDtypeStruct((2, m), F32)],
        grid=(bx // gb_, nl),
        in_specs=[seq(RET_V_W), seq(FOX_W), seq(d), seq(d), seq(d),
                  pl.BlockSpec((gb_, 1, mod.shape[-1]), lambda b, j: (b, 0, 0)),
                  _resident(wr.shape), _resident(wf.shape), _resident(wo.shape),
                  _resident(lw.shape), _resident(lb.shape), _resident(rw3.shape), _resident(rb.shape)],
        out_specs=[seq(d), pl.BlockSpec((tm, d), lambda b, j: (b * nl + j, 0)), tok(2), tok(2)],
        compiler_params=_params("arbitrary", "arbitrary"),
        name="outproj",
    )(yr, yf, ga, gb, x, mod, wr, wf, wo, lw, lb, rw3, rb)


MOE_CHUNK = 8


def _ceil_chunk(x):
    return jnp.floor((x + (MOE_CHUNK - 1.0)) * (1.0 / MOE_CHUNK)) * MOE_CHUNK


def _plan_kernel(eid_ref, triu_ref, ld_ref, pc_ref, pre_ref, carry_ref):
    @pl.when(pl.program_id(0) == 0)
    def _():
        carry_ref[...] = jnp.zeros_like(carry_ref)

    t = eid_ref.shape[1]
    experts = lax.broadcasted_iota(jnp.int32, (N_EXPERTS, t), 0)
    hit = [eid_ref[kk:kk + 1, :] == experts for kk in range(2)]
    onehot = [jnp.where(h, 1.0, 0.0) for h in hit]
    onehot_b = [o.astype(BF16) for o in onehot]
    ones = jnp.ones((8, t), BF16)
    cnt_row = (_dot_nt(ones, onehot_b[0]) + _dot_nt(ones, onehot_b[1]))[0:1, :]
    cnt0_col = jnp.sum(onehot[0], axis=1, keepdims=True)
    pc_row = _ceil_chunk(cnt_row)
    e_lane = lax.broadcasted_iota(jnp.int32, (N_EXPERTS, N_EXPERTS), 1)
    e_sub = lax.broadcasted_iota(jnp.int32, (N_EXPERTS, N_EXPERTS), 0)
    lstart_col = jnp.sum(jnp.where(e_lane < e_sub, pc_row, 0.0), axis=1, keepdims=True)
    for kk in range(2):
        before = _dot(onehot_b[kk], triu_ref[...])
        base = lstart_col + cnt0_col if kk == 1 else lstart_col
        row = jnp.sum(jnp.where(hit[kk], before + base, 0.0), axis=0, keepdims=True)
        ld_ref[kk:kk + 1, :] = row.astype(jnp.int32)
    pc_ref[0] = pc_row.astype(jnp.int32)
    pre_ref[0] = carry_ref[...].astype(jnp.int32)
    carry_ref[...] = carry_ref[...] + pc_row


def _plan(eid, t):
    m = eid.shape[1]
    n_tiles = m // t
    triu = jnp.asarray(np.triu(np.ones((t, t), np.float32), 1), BF16)
    per_tile = jax.ShapeDtypeStruct((n_tiles, 1, N_EXPERTS), jnp.int32)
    per_tile_spec = pl.BlockSpec((1, 1, N_EXPERTS), lambda i: (i, 0, 0))
    return pl.pallas_call(
        _plan_kernel,
        out_shape=[jax.ShapeDtypeStruct((2, m), jnp.int32), per_tile, per_tile],
        grid=(n_tiles,),
        in_specs=[pl.BlockSpec((2, t), lambda i: (0, i)), pl.BlockSpec((t, t), lambda i: (0, 0))],
        out_specs=[pl.BlockSpec((2, t), lambda i: (0, i)), per_tile_spec, per_tile_spec],
        scratch_shapes=[pltpu.VMEM((1, N_EXPERTS), F32)],
        compiler_params=_params("arbitrary"),
        name="moe_plan",
    )(eid, triu)


def _for_each_chunk(tile, pc_ref, goff_ref, fn):
    def per_expert(e, local):
        n = lax.div(pc_ref[tile * N_EXPERTS + e], jnp.int32(MOE_CHUNK))
        dst = goff_ref[tile * N_EXPERTS + e]

        def per_chunk(j, c):
            fn(pl.multiple_of(local + j * MOE_CHUNK, MOE_CHUNK), pl.multiple_of(dst + j * MOE_CHUNK, MOE_CHUNK))
            return c

        lax.fori_loop(0, n, per_chunk, 0)
        return local + n * MOE_CHUNK

    return lax.fori_loop(0, N_EXPERTS, per_expert, 0)


def _one_hot_rows(ld_row, lp):
    rows = lax.broadcasted_iota(jnp.int32, (lp, ld_row.shape[1]), 0)
    return rows == ld_row


HIGH_HALF = -65536


def _pack_halves(x):
    half = x.shape[1] // 2
    lo = pltpu.bitcast(x[:, :half], jnp.int32)
    hi = pltpu.bitcast(x[:, half:], jnp.int32)
    return (hi & HIGH_HALF) | lax.shift_right_logical(lo, jnp.int32(16))


def _unpack_halves(w):
    lo = pltpu.bitcast(lax.shift_left(w, jnp.int32(16)), F32)
    hi = pltpu.bitcast(w & HIGH_HALF, F32)
    return lo.astype(BF16), hi.astype(BF16)


def _round_bf16(x):
    return x.astype(BF16).astype(F32)


def _dispatch_kernel(pc_ref, goff_ref, tail_ref, ld_ref, *rest, first_tiles):
    h_refs = rest[:len(first_tiles)]
    xs_ref, buf_ref, zero_ref, rows_ref, sem, zero_sem = rest[len(first_tiles):]
    tile = pl.program_id(0)
    n_tiles = pl.num_programs(0)
    slot = tile % 2
    lp = buf_ref.shape[1]
    tb = zero_ref.shape[0]

    def copy(s, local, dst):
        return pltpu.make_async_copy(buf_ref.at[s, pl.ds(local, MOE_CHUNK)], xs_ref.at[pl.ds(dst, MOE_CHUNK)],
                                     sem.at[s])

    def fill_chunk(dst):
        return pltpu.make_async_copy(zero_ref.at[pl.ds(0, MOE_CHUNK)], xs_ref.at[pl.ds(dst, MOE_CHUNK)], zero_sem)

    def fill_block(dst):
        return pltpu.make_async_copy(zero_ref, xs_ref.at[pl.ds(dst, tb)], zero_sem)

    def for_each_fill(on_chunk, on_block):
        def per_expert(e, c):
            first = tail_ref[e]

            def per_chunk(j, cc):
                on_chunk(pl.multiple_of(first + j * MOE_CHUNK, MOE_CHUNK))
                return cc

            lax.fori_loop(0, tail_ref[N_EXPERTS + e], per_chunk, 0)
            return c

        lax.fori_loop(0, N_EXPERTS, per_expert, 0)
        first = tail_ref[2 * N_EXPERTS]

        def per_block(j, c):
            on_block(pl.multiple_of(first + j * tb, tb))
            return c

        lax.fori_loop(0, tail_ref[2 * N_EXPERTS + 1], per_block, 0)

    perm = jnp.where(_one_hot_rows(ld_ref[0:1, :], lp), 1.0,
                     jnp.where(_one_hot_rows(ld_ref[1:2, :], lp), 1.0, 0.0)).astype(BF16)
    h = h_refs[0][...]
    for first, h_ref in zip(first_tiles[1:], h_refs[1:]):
        h = jnp.where(tile >= first, h_ref[...], h)
    buf_ref[slot] = _pack_halves(_dot(perm, h))
    rows_ref[slot] = _for_each_chunk(tile, pc_ref, goff_ref, lambda a, b: copy(slot, a, b).start())

    def wait_rows(s):
        n = pl.multiple_of(rows_ref[s], MOE_CHUNK)
        pltpu.make_async_copy(buf_ref.at[s, pl.ds(0, n)], xs_ref.at[pl.ds(0, n)], sem.at[s]).wait()

    @pl.when(tile == 0)
    def _():
        zero_ref[...] = jnp.zeros_like(zero_ref)
        for_each_fill(lambda d: fill_chunk(d).start(), lambda d: fill_block(d).start())
        for_each_fill(lambda d: fill_chunk(d).wait(), lambda d: fill_block(d).wait())

    @pl.when(tile > 0)
    def _():
        wait_rows(1 - slot)

    @pl.when(tile == n_tiles - 1)
    def _():
        wait_rows(slot)


def _dispatch(pc, goff, tails, ld, h2s, n_slots, t, lp, tb):
    d = h2s[0].shape[1]
    counts = [h.shape[0] // t for h in h2s]
    first_tiles = tuple(sum(counts[:g]) for g in range(len(h2s)))

    def rows_of(g):
        return pl.BlockSpec((t, d), lambda i, *_: (jnp.clip(i - first_tiles[g], 0, counts[g] - 1), 0))

    return pl.pallas_call(
        functools.partial(_dispatch_kernel, first_tiles=first_tiles),
        out_shape=jax.ShapeDtypeStruct((n_slots, d // 2), jnp.int32),
        grid_spec=pltpu.PrefetchScalarGridSpec(
            num_scalar_prefetch=3,
            grid=(sum(counts),),
            in_specs=[pl.BlockSpec((2, t), lambda i, *_: (0, i))] + [rows_of(g) for g in range(len(h2s))],
            out_specs=pl.BlockSpec(memory_space=pl.ANY),
            scratch_shapes=[pltpu.VMEM((2, lp, d // 2), jnp.int32), pltpu.VMEM((tb, d // 2), jnp.int32),
                            pltpu.SMEM((2,), jnp.int32),
                            pltpu.SemaphoreType.DMA((2,)), pltpu.SemaphoreType.DMA]),
        compiler_params=_params("arbitrary"),
        name="moe_dispatch",
    )(pc, goff, tails, ld, *h2s)


def _expert_kernel(be_ref, nv_ref, x_ref, w1_ref, w2_ref, y_ref, w1b_ref, w2b_ref):
    i = pl.program_id(0)

    @pl.when((i == 0) | (be_ref[i] != be_ref[jnp.maximum(i - 1, 0)]))
    def _():
        w1b_ref[...] = w1_ref[0].astype(BF16)
        w2b_ref[...] = w2_ref[0].astype(BF16)

    @pl.when(i < nv_ref[0])
    def _():
        e = w2b_ref.shape[0]
        half = x_ref.shape[1]
        x_lo, x_hi = _unpack_halves(x_ref[...])
        au = _dot(x_lo, w1b_ref[:half, :]) + _dot(x_hi, w1b_ref[half:, :])
        a = au[:, :e]
        u = au[:, e:]
        y = _dot((a * _sigmoid(a) * u).astype(BF16), w2b_ref[...])
        y_ref[...] = _pack_halves(_round_bf16(y))

    @pl.when(pl.program_id(0) >= nv_ref[0])
    def _():
        y_ref[...] = jnp.zeros_like(y_ref)


def _experts(block_e, n_valid, xs, w1, w2, tb):
    p, half = xs.shape
    d = 2 * half
    e = w2.shape[1]
    rows = lambda i, be, nv: (jnp.minimum(i, nv[0] - 1), 0)
    return pl.pallas_call(
        _expert_kernel,
        out_shape=jax.ShapeDtypeStruct((p, half), jnp.int32),
        grid_spec=pltpu.PrefetchScalarGridSpec(
            num_scalar_prefetch=2,
            grid=(p // tb,),
            in_specs=[pl.BlockSpec((tb, half), rows),
                      pl.BlockSpec((1, d, 2 * e), lambda i, be, nv: (be[i], 0, 0)),
                      pl.BlockSpec((1, e, d), lambda i, be, nv: (be[i], 0, 0))],
            out_specs=pl.BlockSpec((tb, half), lambda i, be, nv: (i, 0)),
            scratch_shapes=[pltpu.VMEM((d, 2 * e), BF16), pltpu.VMEM((e, d), BF16)]),
        compiler_params=_params("arbitrary"),
        name="moe_experts",
    )(block_e, n_valid, xs, w1, w2)


def _combine_kernel(pc_ref, goff_ref, ld_ref, gate_ref, y_hbm, x1_ref, mod_ref, lw_ref, lb_ref, o_ref,
                    buf_ref, rows_ref, sem, *, alpha, tile0):
    gb_, tl, d = x1_ref.shape
    nl = pl.num_programs(1)
    step = pl.program_id(0) * nl + pl.program_id(1)
    n_steps = pl.num_programs(0) * nl
    tile = tile0 + step
    slot = step % 2
    lp = buf_ref.shape[1]

    def copy(s, local, src):
        return pltpu.make_async_copy(y_hbm.at[pl.ds(src, MOE_CHUNK)], buf_ref.at[s, pl.ds(local, MOE_CHUNK)],
                                     sem.at[s])

    @pl.when(step == 0)
    def _():
        buf_ref[...] = jnp.zeros_like(buf_ref)
        rows_ref[slot] = _for_each_chunk(tile, pc_ref, goff_ref, lambda a, b: copy(slot, a, b).start())

    @pl.when(step + 1 < n_steps)
    def _():
        rows_ref[1 - slot] = _for_each_chunk(tile + 1, pc_ref, goff_ref,
                                             lambda a, b: copy(1 - slot, a, b).start())

    n = pl.multiple_of(rows_ref[slot], MOE_CHUNK)
    pltpu.make_async_copy(y_hbm.at[pl.ds(0, n)], buf_ref.at[slot, pl.ds(0, n)], sem.at[slot]).wait()
    y_lo, y_hi = _unpack_halves(buf_ref[slot])
    zero = jnp.zeros((), F32)
    unsort = jnp.where(_one_hot_rows(ld_ref[0:1, :], lp), gate_ref[0:1, :],
                       jnp.where(_one_hot_rows(ld_ref[1:2, :], lp), gate_ref[1:2, :], zero)).astype(BF16)
    y = jnp.concatenate([_dot_tn(unsort, y_lo), _dot_tn(unsort, y_hi)], axis=1)
    g2 = mod_ref[:, :, 5 * d:6 * d]
    o_ref[...] = _layer_norm(alpha * x1_ref[...] + (1.0 + g2) * y.reshape(gb_, tl, d), lw_ref[...], lb_ref[...])


def _combine(pc, goff, ld, gate, y, x1, mod, lw, lb, gb_, tl, alpha, lp, tile0):
    bx, l, d = x1.shape
    tm = gb_ * tl
    nl = l // tl
    seq = pl.BlockSpec((gb_, tl, d), lambda b, j, pc, go: (b, j, 0))
    tok = pl.BlockSpec((2, tm), lambda b, j, pc, go: (0, tile0 + b * nl + j))
    return pl.pallas_call(
        functools.partial(_combine_kernel, alpha=alpha, tile0=tile0),
        out_shape=jax.ShapeDtypeStruct((bx, l, d), F32),
        grid_spec=pltpu.PrefetchScalarGridSpec(
            num_scalar_prefetch=2,
            grid=(bx // gb_, nl),
            in_specs=[tok, tok,
                      pl.BlockSpec(memory_space=pl.ANY),
                      seq,
                      pl.BlockSpec((gb_, 1, mod.shape[-1]), lambda b, j, pc, go: (b, 0, 0)),
                      _resident(lw.shape), _resident(lb.shape)],
            out_specs=seq,
            scratch_shapes=[pltpu.VMEM((2, lp, d // 2), jnp.int32), pltpu.SMEM((2,), jnp.int32),
                            pltpu.SemaphoreType.DMA((2,))]),
        compiler_params=_params("arbitrary", "arbitrary"),
        name="moe_combine",
    )(pc, goff, ld, gate, y, x1, mod, lw, lb)


def _moe(h2s, eid, gate, groups, w1, w2, lw, lb, alpha, t, tb):
    m = eid.shape[1]
    n_tiles = m // t
    lp = 2 * t + N_EXPERTS * MOE_CHUNK
    ld, pc, pre = _plan(eid, t)
    pc = pc.reshape(n_tiles, N_EXPERTS)
    pre = pre.reshape(n_tiles, N_EXPERTS)
    total = pre[-1] + pc[-1]
    region = (total + tb - 1) // tb * tb
    gend = jnp.cumsum(region)
    goff = (gend - region)[None, :] + pre
    n_blocks = -(-(2 * m + n_tiles * N_EXPERTS * (MOE_CHUNK - 1) + N_EXPERTS * (tb - 1)) // tb)
    block_row0 = jnp.arange(n_blocks, dtype=jnp.int32) * tb
    block_e = jnp.minimum(jnp.sum((gend[None, :] <= block_row0[:, None]).astype(jnp.int32), axis=1),
                          N_EXPERTS - 1).astype(jnp.int32)
    n_valid = (gend[-1:] // tb).astype(jnp.int32)
    tails = jnp.concatenate([gend - region + total, (region - total) // MOE_CHUNK,
                             gend[-1:], n_blocks - gend[-1:] // tb]).astype(jnp.int32)
    pc = pc.reshape(-1).astype(jnp.int32)
    goff = goff.reshape(-1).astype(jnp.int32)
    xs = _dispatch(pc, goff, tails, ld, h2s, n_blocks * tb, t, lp, tb)
    y = _experts(block_e, n_valid, xs, w1, w2, tb)
    return [_combine(pc, goff, ld, gate, y, x1, mod, lw, lb, gb_, tl, alpha, lp, tile0)
            for x1, mod, gb_, tl, tile0 in groups]


def _rotary_tables(pos, reps):
    half = RET_QK_DIM // 2
    inv = ROPE_BASE ** (-jnp.linspace(0.0, 1.0, half, dtype=F32))
    ang = pos.astype(F32)[:, None] * inv[None, :]
    cos = jnp.cos(ang)
    sin = jnp.sin(ang)
    c2 = jnp.concatenate([cos, cos], axis=1)
    s2 = jnp.concatenate([-sin, sin], axis=1)
    kscale = RET_QK_DIM ** -0.5
    tabs = (c2, s2, c2 * kscale, s2 * kscale)
    return tuple(jnp.tile(t, (reps, 1)) for t in tabs)


def _pick(n, pref):
    t = min(n, pref)
    while n % t:
        t //= 2
    return t


def _row_tile(x, cached):
    bx, l, _ = x.shape
    return (_pick(bx, max(1, ROW_TILE // l)), l) if cached else (1, _pick(l, ROW_TILE))


def _mix(x, mod, pos, weights, log_gamma, cache=None):
    (w2, b2, gn_w, gn_b) = weights[:4]
    bx, l, d = x.shape
    gb_, tl = _row_tile(x, cache is not None)
    tabs = _rotary_tables(pos, gb_ if cache is not None else 1)
    rq, rk, rv, rg, fq, fk, fv, fkb, fvb, lf, ga, gb = _inproj(x, mod, w2, b2, tabs, gb_, tl)

    lft = jnp.swapaxes(lf[:, :, :FOX_HEADS], 1, 2)
    if cache is None:
        lc = _pick(l, RET_CHUNK)
        y_ret, state = _retention(rq, rk, rv, rg, gn_w, gn_b, log_gamma, lc)
        fp = _fcum(lft, _pick(l, FCUM_TILE))
        y_fox = _fox_prompt(fq, fkb, fp, fvb, _pick(l, ATTN_TILE))
    else:
        state0, cache_k, cache_v, cache_logf = cache
        y_ret, state = _retention(rq, rk, rv, rg, gn_w, gn_b, log_gamma, l, state0)
        past = cache_k.shape[1]
        total = -(-(past + l) // LANES) * LANES
        lft_all = jnp.concatenate(
            [jnp.swapaxes(cache_logf.astype(F32), 1, 2), lft,
             jnp.zeros((bx, FOX_HEADS, total - past - l), F32)], axis=2)
        fp = _fcum(lft_all, total)
        y_fox = _fox_sample(fq, fkb, fvb, cache_k.reshape(bx, past, FOX_W), cache_v.reshape(bx, past, FOX_W), fp)

    return (y_ret, y_fox, ga, gb), (state, fk, fv, lf[:, :, :FOX_HEADS])


def _layer(xs, mods, positions, caches, weights, log_gamma, alpha):
    (wr, wf, wo, ln1w, ln1b, rw3, rb, we1, we2, ln2w, ln2b) = weights[4:]
    tiles = [_row_tile(x, c is not None) for x, c in zip(xs, caches)]
    t = tiles[0][0] * tiles[0][1]
    assert all(gb_ * tl == t for gb_, tl in tiles), "request groups must share one MoE tile size"
    tile0, groups, h2s, eids, gates, extras = 0, [], [], [], [], []
    for x, mod, pos, cache, (gb_, tl) in zip(xs, mods, positions, caches, tiles):
        branches, extra = _mix(x, mod, pos, weights, log_gamma, cache)
        x1, h2, eid, gate = _outproj(*branches, x, mod, wr, wf, wo, ln1w, ln1b, rw3, rb, gb_, tl, alpha)
        groups.append((x1, mod, gb_, tl, tile0))
        h2s.append(h2); eids.append(eid); gates.append(gate); extras.append(extra)
        tile0 += x.shape[0] * x.shape[1] // t
    outs = _moe(h2s, jnp.concatenate(eids, axis=1), jnp.concatenate(gates, axis=1), groups,
                we1, we2, ln2w, ln2b, alpha, t, tb=EXPERT_ROWS)
    return outs, extras


def kernel(x_prompt, x_sample, state_ret, cache_fox_k, cache_fox_v, cache_fox_logf, c_prompt, c_sample,
           w_ada, b_ada, w_in, b_in, ret_gn_w, ret_gn_b, w_ret_proj, w_fox_proj, w_o, ln1_w, ln1_b,
           w_rg, b_rg, w_re, b_re, w_e_in, w_e_out, ln2_w, ln2_b):
    depth = w_ada.shape[0]
    d = x_prompt.shape[-1]
    bp, s, _ = x_prompt.shape
    bs, ls, _ = x_sample.shape
    past = cache_fox_k.shape[2]
    alpha = (2 * depth) ** 0.25
    log_gamma = jnp.log1p(-jnp.exp(jnp.linspace(math.log(1.0 / 32), math.log(1.0 / 512), RET_HEADS, dtype=F32)))
    pos_p = jnp.arange(s, dtype=jnp.int32)
    pos_s = past + jnp.arange(ls, dtype=jnp.int32)
    fg = 2 * RET_QK_W + 2 * RET_V_W + 3 * FOX_W

    xp, xs = x_prompt, x_sample
    p_ret, p_k, p_v, p_f, s_ret, s_k, s_v, s_f = [], [], [], [], [], [], [], []
    for li in range(depth):
        w2 = jnp.concatenate([w_in[li][:, :fg], w_in[li][:, fg + FOX_HEADS:],
                              jnp.pad(w_in[li][:, fg:fg + FOX_HEADS], ((0, 0), (0, LANES - FOX_HEADS)))],
                             axis=1).astype(BF16)
        b2 = jnp.concatenate([b_in[li][:fg], b_in[li][fg + FOX_HEADS:],
                              jnp.pad(b_in[li][fg:fg + FOX_HEADS], (0, LANES - FOX_HEADS))]).reshape(1, -1)
        n_rt = N_EXPERTS + N_GROUPS
        rt_rows = -(-n_rt // 8) * 8
        rwt = jnp.pad(jnp.concatenate([w_re[li], w_rg[li]], axis=1).T.astype(F32), ((0, rt_rows - n_rt), (0, 0)))
        r_hi = rwt.astype(BF16)
        r_mid = (rwt - r_hi.astype(F32)).astype(BF16)
        rw3 = jnp.stack([r_hi, r_mid])
        rb = jnp.pad(jnp.concatenate([b_re[li], b_rg[li]]).astype(F32), (0, rt_rows - n_rt)).reshape(rt_rows, 1)
        weights = (w2, b2, ret_gn_w[li], ret_gn_b[li],
                   w_ret_proj[li].astype(BF16), w_fox_proj[li].astype(BF16), w_o[li].astype(BF16),
                   ln1_w[li].reshape(1, d), ln1_b[li].reshape(1, d), rw3, rb,
                   w_e_in[li], w_e_out[li],
                   ln2_w[li].reshape(1, d), ln2_b[li].reshape(1, d))
        mod = _ada(jnp.concatenate([c_prompt, c_sample], axis=0), w_ada[li], b_ada[li])
        mod_p = mod[:bp].reshape(bp, 1, 6 * d)
        mod_s = mod[bp:].reshape(bs, 1, 6 * d)
        cache = (state_ret[li].astype(F32), cache_fox_k[li], cache_fox_v[li], cache_fox_logf[li])
        (xp, xs), (extra_p, extra_s) = _layer([xp, xs], [mod_p, mod_s], [pos_p, pos_s], [None, cache],
                                              weights, log_gamma, alpha)
        st, kk, vv, ff = extra_p
        p_ret.append(st); p_k.append(kk.reshape(bp, s, FOX_HEADS, FOX_HEAD_DIM))
        p_v.append(vv.reshape(bp, s, FOX_HEADS, FOX_HEAD_DIM)); p_f.append(ff)
        st, kk, vv, ff = extra_s
        s_ret.append(st); s_k.append(kk.reshape(bs, ls, FOX_HEADS, FOX_HEAD_DIM))
        s_v.append(vv.reshape(bs, ls, FOX_HEADS, FOX_HEAD_DIM)); s_f.append(ff)
    return (xp, xs, jnp.stack(p_ret), jnp.stack(p_k), jnp.stack(p_v), jnp.stack(p_f),
            jnp.stack(s_ret), jnp.stack(s_k), jnp.stack(s_v), jnp.stack(s_f))
```

```python
import functools
import math

import jax
import jax.numpy as jnp
import numpy as np
from jax import lax
from jax.experimental import pallas as pl
from jax.experimental.pallas import tpu as pltpu

F32 = jnp.float32
BF16 = jnp.bfloat16

RET_HEADS = 4
RET_QK_DIM = 128
RET_V_DIM = 256
ROPE_BASE = 10000.0
FOX_HEADS = 16
FOX_HEAD_DIM = 64
N_GROUPS = 4
EXPERTS_PER_GROUP = 8
N_EXPERTS = N_GROUPS * EXPERTS_PER_GROUP
LN_EPS = 1e-5
GN_EPS = 1e-6
RET_QK_W = RET_HEADS * RET_QK_DIM
RET_V_W = RET_HEADS * RET_V_DIM
FOX_W = FOX_HEADS * FOX_HEAD_DIM
FOX_PAIRS = FOX_HEADS // 2
LOG2_E = math.log2(math.e)
FOX_Q_SCALE = FOX_HEAD_DIM ** -0.5 * LOG2_E

LANES = 128
VMEM_LIMIT_BYTES = 56 * 1024 * 1024
MASK_VALUE = -1e30

ROW_TILE = 512
ATTN_TILE = 512
FCUM_TILE = 1024
RET_CHUNK = 512
EXPERT_ROWS = 512


def _dot(a, b):
    return jnp.dot(a, b, preferred_element_type=F32)


def _dot_nt(a, b):
    return lax.dot_general(a, b, (((1,), (1,)), ((), ())), preferred_element_type=F32)


def _dot_tn(a, b):
    return lax.dot_general(a, b, (((0,), (0,)), ((), ())), preferred_element_type=F32)


def _split3(x):
    hi = x.astype(BF16)
    r1 = x - hi.astype(F32)
    mid = r1.astype(BF16)
    lo = (r1 - mid.astype(F32)).astype(BF16)
    return hi, mid, lo


def _sigmoid(x):
    return 1.0 / (1.0 + jnp.exp(-x))


def _log_sigmoid(x):
    return -(jnp.maximum(-x, 0.0) + jnp.log1p(jnp.exp(-jnp.abs(x))))


def _params(*sem):
    return pltpu.CompilerParams(dimension_semantics=sem, vmem_limit_bytes=VMEM_LIMIT_BYTES)


def _resident(shape):
    nd = len(shape)
    return pl.BlockSpec(shape, lambda *_: (0,) * nd, pipeline_mode=pl.Buffered(1))


def _ada_kernel(c_ref, w_ref, b_ref, o_ref):
    c = c_ref[...]
    s = (c * _sigmoid(c)).astype(BF16)
    o_ref[...] = _dot(s, w_ref[...].astype(BF16)) + b_ref[...]


def _ada(c, w, b):
    n, d = c.shape
    nout = w.shape[1]
    tn = d
    return pl.pallas_call(
        _ada_kernel,
        out_shape=jax.ShapeDtypeStruct((n, nout), F32),
        grid=(nout // tn,),
        in_specs=[pl.BlockSpec((n, d), lambda j: (0, 0)),
                  pl.BlockSpec((d, tn), lambda j: (0, j)),
                  pl.BlockSpec((1, tn), lambda j: (0, j))],
        out_specs=pl.BlockSpec((n, tn), lambda j: (0, j)),
        compiler_params=_params("arbitrary"),
        name="ada",
    )(c, w, b.reshape(1, nout))


def _inproj_kernel(x_ref, mod_ref, w_ref, b_ref, cq_ref, sq_ref, ck_ref, sk_ref,
                   rq_ref, rk_ref, rv_ref, rg_ref, fq_ref, fk_ref, fv_ref, fkb_ref, fvb_ref,
                   lf_ref, ga_ref, gb_ref, *, transposed_v):
    gb_, tl, d = x_ref.shape
    tm = gb_ * tl
    sh = mod_ref[:, :, 0:d]
    sc = mod_ref[:, :, d:2 * d]
    h = (x_ref[...] * (1.0 + sc) + sh).reshape(tm, d).astype(BF16)

    def proj(lo, width):
        return _dot(h, w_ref[:, lo:lo + width]) + b_ref[:, lo:lo + width]

    def put(ref, val):
        ref[...] = val.reshape(ref.shape).astype(ref.dtype)

    def rot(z, c_ref, s_ref):
        c = c_ref[...]
        s = s_ref[...]
        parts = []
        for hh in range(RET_HEADS):
            zh = z[:, hh * RET_QK_DIM:(hh + 1) * RET_QK_DIM]
            parts.append(zh * c + pltpu.roll(zh, RET_QK_DIM // 2, axis=1) * s)
        return jnp.concatenate(parts, axis=1)

    off = 0
    put(rq_ref, rot(proj(off, RET_QK_W), cq_ref, sq_ref)); off += RET_QK_W
    put(rk_ref, rot(proj(off, RET_QK_W), ck_ref, sk_ref)); off += RET_QK_W
    put(rv_ref, proj(off, RET_V_W)); off += RET_V_W
    z = proj(off, RET_V_W); off += RET_V_W
    put(rg_ref, z * _sigmoid(z))
    put(fq_ref, proj(off, FOX_W) * FOX_Q_SCALE); off += FOX_W
    z = proj(off, FOX_W); off += FOX_W
    put(fk_ref, z); put(fkb_ref, z)
    z = proj(off, FOX_W); off += FOX_W
    put(fv_ref, z)
    if transposed_v:
        fvb_ref[0] = z.T.astype(fvb_ref.dtype)
    else:
        put(fvb_ref, z)
    put(ga_ref, _sigmoid(proj(off, d))); off += d
    put(gb_ref, _sigmoid(proj(off, d))); off += d
    put(lf_ref, _log_sigmoid(proj(off, LANES)))


def _inproj(x, mod, w2, b2, tabs, gb_, tl):
    bx, l, d = x.shape
    tm = gb_ * tl
    nw = w2.shape[1]
    grid = (bx // gb_, l // tl)
    if gb_ == 1:
        tab_spec = pl.BlockSpec((tl, LANES), lambda b, j: (j, 0))
    else:
        tab_spec = pl.BlockSpec((tm, LANES), lambda b, j: (0, 0))

    def out(width, dtype):
        return (jax.ShapeDtypeStruct((bx, l, width), dtype),
                pl.BlockSpec((gb_, tl, width), lambda b, j: (b, j, 0)))

    transposed_v = gb_ == 1
    if transposed_v:
        fvb = (jax.ShapeDtypeStruct((bx, FOX_W, l), BF16), pl.BlockSpec((1, FOX_W, tl), lambda b, j: (b, 0, j)))
    else:
        fvb = out(FOX_W, BF16)
    outs = [out(RET_QK_W, BF16), out(RET_QK_W, BF16), out(RET_V_W, BF16), out(RET_V_W, BF16),
            out(FOX_W, BF16), out(FOX_W, F32), out(FOX_W, F32), out(FOX_W, BF16), fvb,
            out(LANES, F32), out(d, BF16), out(d, BF16)]
    return pl.pallas_call(
        functools.partial(_inproj_kernel, transposed_v=transposed_v),
        out_shape=[o[0] for o in outs],
        grid=grid,
        in_specs=[pl.BlockSpec((gb_, tl, d), lambda b, j: (b, j, 0)),
                  pl.BlockSpec((gb_, 1, mod.shape[-1]), lambda b, j: (b, 0, 0)),
                  _resident((d, nw)), _resident((1, nw)),
                  tab_spec, tab_spec, tab_spec, tab_spec],
        out_specs=[o[1] for o in outs],
        compiler_params=_params("arbitrary", "arbitrary"),
        name="inproj",
    )(x, mod, w2, b2, *tabs)


BIAS_PIECES = 3


def _fcum_kernel(lft_ref, triu_ref, fp_ref, carry_ref):
    @pl.when(pl.program_id(1) == 0)
    def _():
        carry_ref[...] = jnp.zeros_like(carry_ref)

    tl = lft_ref.shape[2]
    hi, mid, lo = _split3(lft_ref[0])
    triu = triu_ref[...]
    cum = _dot(hi, triu) + _dot(mid, triu) + _dot(lo, triu) + carry_ref[...]
    carry_ref[...] = cum[:, tl - 1:tl]
    padded = jnp.concatenate([cum * -LOG2_E, jnp.zeros((LANES - FOX_HEADS, tl), F32)], axis=0)
    pieces = _split3(padded.T)
    out = pieces[0].astype(F32)
    for p in range(1, BIAS_PIECES):
        out = out + pltpu.roll(pieces[p].astype(F32), p * FOX_HEADS, axis=1)
    fp_ref[0] = out.astype(fp_ref.dtype)


def _fcum(lft, tl):
    bx, _, l = lft.shape
    triu = jnp.asarray(np.triu(np.ones((tl, tl), np.float32)), BF16)
    return pl.pallas_call(
        _fcum_kernel,
        out_shape=jax.ShapeDtypeStruct((bx, l, LANES), BF16),
        grid=(bx, l // tl),
        in_specs=[pl.BlockSpec((1, FOX_HEADS, tl), lambda b, j: (b, 0, j)),
                  pl.BlockSpec((tl, tl), lambda b, j: (0, 0))],
        out_specs=pl.BlockSpec((1, tl, LANES), lambda b, j: (b, j, 0)),
        scratch_shapes=[pltpu.VMEM((FOX_HEADS, 1), F32)],
        compiler_params=_params("arbitrary", "arbitrary"),
        name="fcum",
    )(lft, triu)


def _retention_kernel(*refs, has_state):
    if has_state:
        (q_ref, k_ref, v_ref, g_ref, dec_ref, qd_ref, kd_ref, gw_ref, gb_ref, s0_ref,
         y_ref, st_ref) = refs
    else:
        (q_ref, k_ref, v_ref, g_ref, dec_ref, qd_ref, kd_ref, gw_ref, gb_ref,
         y_ref, st_ref) = refs
        s0_ref = None
    lc = q_ref.shape[1]

    @pl.when(pl.program_id(1) == 0)
    def _():
        if has_state:
            st_ref[...] = s0_ref[...]
        else:
            st_ref[...] = jnp.zeros_like(st_ref)

    for hh in range(RET_HEADS):
        qs = slice(hh * RET_QK_DIM, (hh + 1) * RET_QK_DIM)
        vs = slice(hh * RET_V_DIM, (hh + 1) * RET_V_DIM)
        q = q_ref[0, :, qs]
        k = k_ref[0, :, qs]
        v = v_ref[0, :, vs]
        state = st_ref[0, hh]
        qd = qd_ref[hh]
        scores = _dot_nt(q, k) * dec_ref[hh]
        inner = _dot(scores.astype(BF16), v)
        cross = _dot(q, state.astype(BF16)) * jnp.concatenate([qd] * (RET_V_DIM // LANES), axis=1)
        o = inner + cross
        kdec = (k.astype(F32) * kd_ref[hh]).astype(BF16)
        st_ref[0, hh] = qd[lc - 1:lc, 0:1] * state + _dot_tn(kdec, v)
        mu = jnp.mean(o, axis=-1, keepdims=True)
        oc = o - mu
        var = jnp.mean(oc * oc, axis=-1, keepdims=True)
        on = oc * lax.rsqrt(var + GN_EPS) * gw_ref[:, vs] + gb_ref[:, vs]
        y_ref[0, :, vs] = (g_ref[0, :, vs].astype(F32) * on).astype(y_ref.dtype)


def _retention_tables(log_gamma, lc):
    n = jnp.arange(lc, dtype=F32)
    diff = n[:, None] - n[None, :]
    decay = jnp.where(diff[None] >= 0, jnp.exp(jnp.maximum(diff, 0.0)[None] * log_gamma[:, None, None]), 0.0)
    qdec = jnp.exp((n + 1.0)[None, :, None] * log_gamma[:, None, None])
    kdec = jnp.exp((lc - 1.0 - n)[None, :, None] * log_gamma[:, None, None])
    lanes = (RET_HEADS, lc, LANES)
    return decay, jnp.broadcast_to(qdec, lanes), jnp.broadcast_to(kdec, lanes)


def _retention(q, k, v, g, gn_w, gn_b, log_gamma, lc, state0=None):
    bx, l, _ = q.shape
    decay, qdec, kdec = _retention_tables(log_gamma, lc)
    has_state = state0 is not None
    seq = lambda w: pl.BlockSpec((1, lc, w), lambda b, c: (b, c, 0))
    whole = lambda a: pl.BlockSpec(a.shape, lambda b, c: (0,) * a.ndim)
    st_spec = pl.BlockSpec((1, RET_HEADS, RET_QK_DIM, RET_V_DIM), lambda b, c: (b, 0, 0, 0))
    gw = gn_w.reshape(1, RET_V_W)
    gb = gn_b.reshape(1, RET_V_W)
    args = [q, k, v, g, decay, qdec, kdec, gw, gb]
    in_specs = [seq(RET_QK_W), seq(RET_QK_W), seq(RET_V_W), seq(RET_V_W),
                whole(decay), whole(qdec), whole(kdec), whole(gw), whole(gb)]
    if has_state:
        args.append(state0)
        in_specs.append(st_spec)
    return pl.pallas_call(
        functools.partial(_retention_kernel, has_state=has_state),
        out_shape=[jax.ShapeDtypeStruct((bx, l, RET_V_W), BF16),
                   jax.ShapeDtypeStruct((bx, RET_HEADS, RET_QK_DIM, RET_V_DIM), F32)],
        grid=(bx, l // lc),
        in_specs=in_specs,
        out_specs=[seq(RET_V_W), st_spec],
        compiler_params=_params("arbitrary", "arbitrary"),
        name="retention",
    )(*args)


def _pair_queries(q2, pair):
    t = q2.shape[0]
    lane = lax.broadcasted_iota(jnp.int32, (t, LANES), 1)
    out = []
    for i in range(2):
        head = (lane >= i * FOX_HEAD_DIM) & (lane < (i + 1) * FOX_HEAD_DIM)
        offset = lane - (2 * pair + i)
        ones = (offset >= 0) & (offset < BIAS_PIECES * FOX_HEADS) & ((offset & (FOX_HEADS - 1)) == 0)
        out.append(jnp.concatenate([jnp.where(head, q2, jnp.zeros_like(q2)),
                                    jnp.where(ones, 1.0, 0.0).astype(q2.dtype)], axis=1))
    return out


FOX_PAIRS_PER_STEP = 4


def _fox_prompt_kernel(q_ref, k_ref, fp_ref, vt_ref, o_ref, m_ref, l_ref, acc_ref, qc_ref, sa_ref, sb_ref, *, t):
    nq = q_ref.shape[1] // t
    pairs = q_ref.shape[2] // LANES
    steps = [(pr, qi, ki) for pr in range(pairs) for qi in range(nq) for ki in range(qi + 1)]
    bufs = (sa_ref, sb_ref)
    half = t // 2

    def tiles(qi, ki):
        if ki == qi:
            return [((0, half), (0, t)), ((half, t), (half, t))]
        return [((0, t), (0, t))]

    def produce(n, i):
        pr, qi, ki = steps[n]
        lanes = slice(pr * LANES, (pr + 1) * LANES)
        if ki == 0 and i == 0:
            qc = _pair_queries(q_ref[0, qi * t:(qi + 1) * t, lanes], pl.program_id(1) * pairs + pr)
            qc_ref[0] = qc[0]
            qc_ref[1] = qc[1]
        for (k0, k1), (q0, q1) in tiles(qi, ki):
            keys = slice(ki * t + k0, ki * t + k1)
            kc = jnp.concatenate([k_ref[0, keys, lanes], fp_ref[0, keys, :]], axis=1)
            bufs[n % 2][i, k0:k1, q0:q1] = _dot_nt(kc, qc_ref[i, q0:q1, :])

    def consume(n, i):
        pr, qi, ki = steps[n]
        rows = slice(i * FOX_HEAD_DIM, (i + 1) * FOX_HEAD_DIM)
        vrows = slice(pr * LANES + i * FOX_HEAD_DIM, pr * LANES + (i + 1) * FOX_HEAD_DIM)
        for tile_no, ((k0, k1), (q0, q1)) in enumerate(tiles(qi, ki)):
            first = ki == 0 and tile_no == 0
            s = bufs[n % 2][i, k0:k1, q0:q1]
            if ki == qi:
                key = lax.broadcasted_iota(jnp.int32, s.shape, 0) + k0
                qry = lax.broadcasted_iota(jnp.int32, s.shape, 1) + q0
                s = jnp.where(qry >= key, s, MASK_VALUE)
            smax = jnp.max(s, axis=0, keepdims=True)
            m_new = smax if first else jnp.maximum(m_ref[i, :, q0:q1], smax)
            p = jnp.exp2(s - m_new)
            psum = jnp.sum(p, axis=0, keepdims=True)
            pv = _dot(vt_ref[0, vrows, ki * t + k0:ki * t + k1], p.astype(BF16))
            if first:
                l_ref[i, :, q0:q1] = psum
                acc_ref[rows, q0:q1] = pv
            else:
                alpha = jnp.exp2(m_ref[i, :, q0:q1] - m_new)
                l_ref[i, :, q0:q1] = alpha * l_ref[i, :, q0:q1] + psum
                acc_ref[rows, q0:q1] = acc_ref[rows, q0:q1] * alpha + pv
            m_ref[i, :, q0:q1] = m_new
        if ki == qi and i == 1:
            out_t = jnp.concatenate(
                [acc_ref[h * FOX_HEAD_DIM:(h + 1) * FOX_HEAD_DIM, :] * (1.0 / l_ref[h]) for h in range(2)], axis=0)
            o_ref[0, qi * t:(qi + 1) * t, pr * LANES:(pr + 1) * LANES] = out_t.T.astype(o_ref.dtype)

    for i in range(2):
        produce(0, i)
    for n in range(len(steps)):
        for i in range(2):
            if n + 1 < len(steps):
                produce(n + 1, i)
            consume(n, i)


def _fox_prompt(q, k, fp, vt, t):
    b, s, _ = q.shape
    width = FOX_PAIRS_PER_STEP * LANES
    seq = pl.BlockSpec((1, s, width), lambda bi, j: (bi, 0, j))
    return pl.pallas_call(
        functools.partial(_fox_prompt_kernel, t=t),
        out_shape=jax.ShapeDtypeStruct((b, s, FOX_W), BF16),
        grid=(b, FOX_PAIRS // FOX_PAIRS_PER_STEP),
        in_specs=[seq, seq, pl.BlockSpec((1, s, LANES), lambda bi, j: (bi, 0, 0)),
                  pl.BlockSpec((1, width, s), lambda bi, j: (bi, j, 0))],
        out_specs=seq,
        scratch_shapes=[pltpu.VMEM((2, 1, t), F32), pltpu.VMEM((2, 1, t), F32),
                        pltpu.VMEM((LANES, t), F32), pltpu.VMEM((2, t, 2 * LANES), BF16),
                        pltpu.VMEM((2, t, t), F32), pltpu.VMEM((2, t, t), F32)],
        compiler_params=_params("arbitrary", "arbitrary"),
        name="fox_prompt",
    )(q, k, fp, vt)


def _fox_sample_kernel(q_ref, kc_ref, vc_ref, kn_ref, vn_ref, fp_ref, o_ref):
    l = q_ref.shape[1]
    past = kc_ref.shape[1]
    lane = lax.broadcasted_iota(jnp.int32, (2 * l, LANES), 1)
    qrow = lax.broadcasted_iota(jnp.int32, (2 * l, LANES), 0)
    own_lanes = (lane >= FOX_HEAD_DIM) == (qrow >= l)
    row = lax.broadcasted_iota(jnp.int32, (2 * l, l), 0)
    col = lax.broadcasted_iota(jnp.int32, (2 * l, l), 1)
    causal = jnp.where(row >= l, row - l, row) >= col
    fp_old = fp_ref[0, 0:past, :]
    fp_new = fp_ref[0, past:past + l, :]
    for pair in range(FOX_PAIRS):
        lanes = slice(pair * LANES, (pair + 1) * LANES)
        qc = jnp.concatenate(_pair_queries(q_ref[0, :, lanes], pair), axis=0)
        kc = jnp.concatenate([kc_ref[0, :, lanes].astype(BF16), fp_old], axis=1)
        kn = jnp.concatenate([kn_ref[0, :, lanes], fp_new], axis=1)
        s_c = _dot_nt(qc, kc)
        s_n = jnp.where(causal, _dot_nt(qc, kn), MASK_VALUE)
        m = jnp.maximum(jnp.max(s_c, axis=-1, keepdims=True), jnp.max(s_n, axis=-1, keepdims=True))
        p_c = jnp.exp2(s_c - m)
        p_n = jnp.exp2(s_n - m)
        denom = jnp.sum(p_c, axis=-1, keepdims=True) + jnp.sum(p_n, axis=-1, keepdims=True)
        pv = (_dot(p_c.astype(BF16), vc_ref[0, :, lanes].astype(BF16))
              + _dot(p_n.astype(BF16), vn_ref[0, :, lanes])) * (1.0 / denom)
        pv = jnp.where(own_lanes, pv, 0.0)
        o_ref[0, :, lanes] = (pv[:l, :] + pv[l:, :]).astype(o_ref.dtype)


def _fox_sample(q, kn, vn, cache_k, cache_v, fp):
    b, l, _ = q.shape
    past = cache_k.shape[1]
    new = pl.BlockSpec((1, l, FOX_W), lambda bi: (bi, 0, 0))
    old = pl.BlockSpec((1, past, FOX_W), lambda bi: (bi, 0, 0))
    return pl.pallas_call(
        _fox_sample_kernel,
        out_shape=jax.ShapeDtypeStruct((b, l, FOX_W), BF16),
        grid=(b,),
        in_specs=[new, old, old, new, new,
                  pl.BlockSpec((1, fp.shape[1], LANES), lambda bi: (bi, 0, 0))],
        out_specs=new,
        compiler_params=_params("arbitrary"),
        name="fox_sample",
    )(q, cache_k, cache_v, kn, vn, fp)


def _layer_norm(x, w, b):
    mu = jnp.mean(x, axis=-1, keepdims=True)
    xc = x - mu
    var = jnp.mean(xc * xc, axis=-1, keepdims=True)
    return xc * lax.rsqrt(var + LN_EPS) * w + b


def _first_argmax_rows(x, n):
    rows = lax.broadcasted_iota(jnp.int32, x.shape, 0).astype(F32)
    mx = jnp.max(x, axis=0, keepdims=True)
    idx = jnp.min(jnp.where(x == mx, rows, float(n)), axis=0, keepdims=True)
    return mx, idx.astype(jnp.int32)


def _outproj_kernel(yr_ref, yf_ref, ga_ref, gb_ref, x_ref, mod_ref, wr_ref, wf_ref, wo_ref,
                    lw_ref, lb_ref, rw_ref, rb_ref,
                    x1_ref, h2_ref, eid_ref, gate_ref, *, alpha):
    gb_, tl, d = x_ref.shape
    if gb_ > 1:
        parts = [(slice(0, gb_ // 2), slice(None)), (slice(gb_ // 2, gb_), slice(None))]
    else:
        parts = [(slice(None), slice(0, tl // 2)), (slice(None), slice(tl // 2, tl))]
    rows = gb_ * tl // 2

    def mixed_branches(part):
        flat = lambda ref: ref[part[0], part[1], :].reshape(rows, ref.shape[-1])
        y_ret = _dot(flat(yr_ref), wr_ref[...])
        y_fox = _dot(flat(yf_ref), wf_ref[...])
        return flat(ga_ref).astype(F32) * y_ret + flat(gb_ref).astype(F32) * y_fox

    def norm_and_route(n, part, mixed):
        bs, ls = part
        mod = lambda k: mod_ref[bs, :, k * d:(k + 1) * d]
        x = x_ref[bs, ls, :]
        x1 = _layer_norm(alpha * x + (1.0 + mod(2)) * mixed.reshape(x.shape), lw_ref[...], lb_ref[...])
        x1_ref[bs, ls, :] = x1
        h2 = x1 * (1.0 + mod(4)) + mod(3)
        h2_ref[n * rows:(n + 1) * rows, :] = h2.reshape(rows, d).astype(h2_ref.dtype)
        h_hi, h_mid, _ = _split3(h2.reshape(rows, d))
        w_hi, w_mid = rw_ref[0], rw_ref[1]
        lt = _dot_nt(w_hi, h_hi) + _dot_nt(w_hi, h_mid) + _dot_nt(w_mid, h_hi) + rb_ref[...]
        gl = lt[N_EXPERTS:N_EXPERTS + N_GROUPS, :]
        gmax, gi = _first_argmax_rows(gl, N_GROUPS)
        g_p = 1.0 / jnp.sum(jnp.exp(gl - gmax), axis=0, keepdims=True)
        e_sel = lt[0:EXPERTS_PER_GROUP, :]
        for g in range(1, N_GROUPS):
            e_sel = jnp.where(gi == g, lt[g * EXPERTS_PER_GROUP:(g + 1) * EXPERTS_PER_GROUP, :], e_sel)
        sub = lax.broadcasted_iota(jnp.int32, e_sel.shape, 0)
        m1, i1 = _first_argmax_rows(e_sel, EXPERTS_PER_GROUP)
        m2, i2 = _first_argmax_rows(jnp.where(sub == i1, -jnp.inf, e_sel), EXPERTS_PER_GROUP)
        r = jnp.exp(m2 - m1)
        gate0 = g_p / (1.0 + r)
        gate1 = g_p * r / (1.0 + r)
        cols = slice(n * rows, (n + 1) * rows)
        eid_ref[0:1, cols] = gi * EXPERTS_PER_GROUP + i1
        eid_ref[1:2, cols] = gi * EXPERTS_PER_GROUP + i2
        gate_ref[0:1, cols] = gate0
        gate_ref[1:2, cols] = gate1

    mix = [mixed_branches(part) for part in parts]
    mixed = [_dot(mx.astype(BF16), wo_ref[...]) for mx in mix]
    for n, part in enumerate(parts):
        norm_and_route(n, part, mixed[n])


def _outproj(yr, yf, ga, gb, x, mod, wr, wf, wo, lw, lb, rw3, rb, gb_, tl, alpha):
    bx, l, d = x.shape
    m = bx * l
    tm = gb_ * tl
    nl = l // tl
    seq = lambda w: pl.BlockSpec((gb_, tl, w), lambda b, j: (b, j, 0))
    tok = lambda r: pl.BlockSpec((r, tm), lambda b, j: (0, b * nl + j))
    return pl.pallas_call(
        functools.partial(_outproj_kernel, alpha=alpha),
        out_shape=[jax.ShapeDtypeStruct((bx, l, d), F32), jax.ShapeDtypeStruct((m, d), BF16),
                   jax.ShapeDtypeStruct((2, m), jnp.int32), jax.ShapeDtypeStruct((2, m), F32)],
        grid=(bx // gb_, nl),
        in_specs=[seq(RET_V_W), seq(FOX_W), seq(d), seq(d), seq(d),
                  pl.BlockSpec((gb_, 1, mod.shape[-1]), lambda b, j: (b, 0, 0)),
                  _resident(wr.shape), _resident(wf.shape), _resident(wo.shape),
                  _resident(lw.shape), _resident(lb.shape), _resident(rw3.shape), _resident(rb.shape)],
        out_specs=[seq(d), pl.BlockSpec((tm, d), lambda b, j: (b * nl + j, 0)), tok(2), tok(2)],
        compiler_params=_params("arbitrary", "arbitrary"),
        name="outproj",
    )(yr, yf, ga, gb, x, mod, wr, wf, wo, lw, lb, rw3, rb)


MOE_CHUNK = 8


def _ceil_chunk(x):
    return jnp.floor((x + (MOE_CHUNK - 1.0)) * (1.0 / MOE_CHUNK)) * MOE_CHUNK


def _plan_kernel(eid_ref, triu_ref, ld_ref, pc_ref, pre_ref, carry_ref):
    @pl.when(pl.program_id(0) == 0)
    def _():
        carry_ref[...] = jnp.zeros_like(carry_ref)

    t = eid_ref.shape[1]
    experts = lax.broadcasted_iota(jnp.int32, (N_EXPERTS, t), 0)
    hit = [eid_ref[kk:kk + 1, :] == experts for kk in range(2)]
    onehot = [jnp.where(h, 1.0, 0.0) for h in hit]
    onehot_b = [o.astype(BF16) for o in onehot]
    ones = jnp.ones((8, t), BF16)
    cnt_row = (_dot_nt(ones, onehot_b[0]) + _dot_nt(ones, onehot_b[1]))[0:1, :]
    cnt0_col = jnp.sum(onehot[0], axis=1, keepdims=True)
    pc_row = _ceil_chunk(cnt_row)
    e_lane = lax.broadcasted_iota(jnp.int32, (N_EXPERTS, N_EXPERTS), 1)
    e_sub = lax.broadcasted_iota(jnp.int32, (N_EXPERTS, N_EXPERTS), 0)
    lstart_col = jnp.sum(jnp.where(e_lane < e_sub, pc_row, 0.0), axis=1, keepdims=True)
    for kk in range(2):
        before = _dot(onehot_b[kk], triu_ref[...])
        base = lstart_col + cnt0_col if kk == 1 else lstart_col
        row = jnp.sum(jnp.where(hit[kk], before + base, 0.0), axis=0, keepdims=True)
        ld_ref[kk:kk + 1, :] = row.astype(jnp.int32)
    pc_ref[0] = pc_row.astype(jnp.int32)
    pre_ref[0] = carry_ref[...].astype(jnp.int32)
    carry_ref[...] = carry_ref[...] + pc_row


def _plan(eid, t):
    m = eid.shape[1]
    n_tiles = m // t
    triu = jnp.asarray(np.triu(np.ones((t, t), np.float32), 1), BF16)
    per_tile = jax.ShapeDtypeStruct((n_tiles, 1, N_EXPERTS), jnp.int32)
    per_tile_spec = pl.BlockSpec((1, 1, N_EXPERTS), lambda i: (i, 0, 0))
    return pl.pallas_call(
        _plan_kernel,
        out_shape=[jax.ShapeDtypeStruct((2, m), jnp.int32), per_tile, per_tile],
        grid=(n_tiles,),
        in_specs=[pl.BlockSpec((2, t), lambda i: (0, i)), pl.BlockSpec((t, t), lambda i: (0, 0))],
        out_specs=[pl.BlockSpec((2, t), lambda i: (0, i)), per_tile_spec, per_tile_spec],
        scratch_shapes=[pltpu.VMEM((1, N_EXPERTS), F32)],
        compiler_params=_params("arbitrary"),
        name="moe_plan",
    )(eid, triu)


def _for_each_chunk(tile, pc_ref, goff_ref, fn):
    def per_expert(e, local):
        n = lax.div(pc_ref[tile * N_EXPERTS + e], jnp.int32(MOE_CHUNK))
        dst = goff_ref[tile * N_EXPERTS + e]

        def per_chunk(j, c):
            fn(pl.multiple_of(local + j * MOE_CHUNK, MOE_CHUNK), pl.multiple_of(dst + j * MOE_CHUNK, MOE_CHUNK))
            return c

        lax.fori_loop(0, n, per_chunk, 0)
        return local + n * MOE_CHUNK

    return lax.fori_loop(0, N_EXPERTS, per_expert, 0)


def _one_hot_rows(ld_row, lp):
    rows = lax.broadcasted_iota(jnp.int32, (lp, ld_row.shape[1]), 0)
    return rows == ld_row


HIGH_HALF = -65536


def _pack_halves(x):
    half = x.shape[1] // 2
    lo = pltpu.bitcast(x[:, :half], jnp.int32)
    hi = pltpu.bitcast(x[:, half:], jnp.int32)
    return (hi & HIGH_HALF) | lax.shift_right_logical(lo, jnp.int32(16))


def _unpack_halves(w):
    lo = pltpu.bitcast(lax.shift_left(w, jnp.int32(16)), F32)
    hi = pltpu.bitcast(w & HIGH_HALF, F32)
    return lo.astype(BF16), hi.astype(BF16)


def _round_bf16(x):
    return x.astype(BF16).astype(F32)


def _dispatch_kernel(pc_ref, goff_ref, tail_ref, ld_ref, *rest, first_tiles):
    h_refs = rest[:len(first_tiles)]
    xs_ref, buf_ref, zero_ref, rows_ref, sem, zero_sem = rest[len(first_tiles):]
    tile = pl.program_id(0)
    n_tiles = pl.num_programs(0)
    slot = tile % 2
    lp = buf_ref.shape[1]
    tb = zero_ref.shape[0]

    def copy(s, local, dst):
        return pltpu.make_async_copy(buf_ref.at[s, pl.ds(local, MOE_CHUNK)], xs_ref.at[pl.ds(dst, MOE_CHUNK)],
                                     sem.at[s])

    def fill_chunk(dst):
        return pltpu.make_async_copy(zero_ref.at[pl.ds(0, MOE_CHUNK)], xs_ref.at[pl.ds(dst, MOE_CHUNK)], zero_sem)

    def fill_block(dst):
        return pltpu.make_async_copy(zero_ref, xs_ref.at[pl.ds(dst, tb)], zero_sem)

    def for_each_fill(on_chunk, on_block):
        def per_expert(e, c):
            first = tail_ref[e]

            def per_chunk(j, cc):
                on_chunk(pl.multiple_of(first + j * MOE_CHUNK, MOE_CHUNK))
                return cc

            lax.fori_loop(0, tail_ref[N_EXPERTS + e], per_chunk, 0)
            return c

        lax.fori_loop(0, N_EXPERTS, per_expert, 0)
        first = tail_ref[2 * N_EXPERTS]

        def per_block(j, c):
            on_block(pl.multiple_of(first + j * tb, tb))
            return c

        lax.fori_loop(0, tail_ref[2 * N_EXPERTS + 1], per_block, 0)

    perm = jnp.where(_one_hot_rows(ld_ref[0:1, :], lp), 1.0,
                     jnp.where(_one_hot_rows(ld_ref[1:2, :], lp), 1.0, 0.0)).astype(BF16)
    h = h_refs[0][...]
    for first, h_ref in zip(first_tiles[1:], h_refs[1:]):
        h = jnp.where(tile >= first, h_ref[...], h)
    buf_ref[slot] = _pack_halves(_dot(perm, h))
    rows_ref[slot] = _for_each_chunk(tile, pc_ref, goff_ref, lambda a, b: copy(slot, a, b).start())

    def wait_rows(s):
        n = pl.multiple_of(rows_ref[s], MOE_CHUNK)
        pltpu.make_async_copy(buf_ref.at[s, pl.ds(0, n)], xs_ref.at[pl.ds(0, n)], sem.at[s]).wait()

    @pl.when(tile == 0)
    def _():
        zero_ref[...] = jnp.zeros_like(zero_ref)
        for_each_fill(lambda d: fill_chunk(d).start(), lambda d: fill_block(d).start())
        for_each_fill(lambda d: fill_chunk(d).wait(), lambda d: fill_block(d).wait())

    @pl.when(tile > 0)
    def _():
        wait_rows(1 - slot)

    @pl.when(tile == n_tiles - 1)
    def _():
        wait_rows(slot)


def _dispatch(pc, goff, tails, ld, h2s, n_slots, t, lp, tb):
    d = h2s[0].shape[1]
    counts = [h.shape[0] // t for h in h2s]
    first_tiles = tuple(sum(counts[:g]) for g in range(len(h2s)))

    def rows_of(g):
        return pl.BlockSpec((t, d), lambda i, *_: (jnp.clip(i - first_tiles[g], 0, counts[g] - 1), 0))

    return pl.pallas_call(
        functools.partial(_dispatch_kernel, first_tiles=first_tiles),
        out_shape=jax.ShapeDtypeStruct((n_slots, d // 2), jnp.int32),
        grid_spec=pltpu.PrefetchScalarGridSpec(
            num_scalar_prefetch=3,
            grid=(sum(counts),),
            in_specs=[pl.BlockSpec((2, t), lambda i, *_: (0, i))] + [rows_of(g) for g in range(len(h2s))],
            out_specs=pl.BlockSpec(memory_space=pl.ANY),
            scratch_shapes=[pltpu.VMEM((2, lp, d // 2), jnp.int32), pltpu.VMEM((tb, d // 2), jnp.int32),
                            pltpu.SMEM((2,), jnp.int32),
                            pltpu.SemaphoreType.DMA((2,)), pltpu.SemaphoreType.DMA]),
        compiler_params=_params("arbitrary"),
        name="moe_dispatch",
    )(pc, goff, tails, ld, *h2s)


def _expert_kernel(be_ref, nv_ref, x_ref, w1_ref, w2_ref, y_ref, w1b_ref, w2b_ref):
    i = pl.program_id(0)

    @pl.when((i == 0) | (be_ref[i] != be_ref[jnp.maximum(i - 1, 0)]))
    def _():
        w1b_ref[...] = w1_ref[0].astype(BF16)
        w2b_ref[...] = w2_ref[0].astype(BF16)

    @pl.when(i < nv_ref[0])
    def _():
        e = w2b_ref.shape[0]
        half = x_ref.shape[1]
        x_lo, x_hi = _unpack_halves(x_ref[...])
        au = _dot(x_lo, w1b_ref[:half, :]) + _dot(x_hi, w1b_ref[half:, :])
        a = au[:, :e]
        u = au[:, e:]
        y = _dot((a * _sigmoid(a) * u).astype(BF16), w2b_ref[...])
        y_ref[...] = _pack_halves(_round_bf16(y))

    @pl.when(pl.program_id(0) >= nv_ref[0])
    def _():
        y_ref[...] = jnp.zeros_like(y_ref)


def _experts(block_e, n_valid, xs, w1, w2, tb):
    p, half = xs.shape
    d = 2 * half
    e = w2.shape[1]
    rows = lambda i, be, nv: (jnp.minimum(i, nv[0] - 1), 0)
    return pl.pallas_call(
        _expert_kernel,
        out_shape=jax.ShapeDtypeStruct((p, half), jnp.int32),
        grid_spec=pltpu.PrefetchScalarGridSpec(
            num_scalar_prefetch=2,
            grid=(p // tb,),
            in_specs=[pl.BlockSpec((tb, half), rows),
                      pl.BlockSpec((1, d, 2 * e), lambda i, be, nv: (be[i], 0, 0)),
                      pl.BlockSpec((1, e, d), lambda i, be, nv: (be[i], 0, 0))],
            out_specs=pl.BlockSpec((tb, half), lambda i, be, nv: (i, 0)),
            scratch_shapes=[pltpu.VMEM((d, 2 * e), BF16), pltpu.VMEM((e, d), BF16)]),
        compiler_params=_params("arbitrary"),
        name="moe_experts",
    )(block_e, n_valid, xs, w1, w2)


def _combine_kernel(pc_ref, goff_ref, ld_ref, gate_ref, y_hbm, x1_ref, mod_ref, lw_ref, lb_ref, o_ref,
                    buf_ref, rows_ref, sem, *, alpha, tile0):
    gb_, tl, d = x1_ref.shape
    nl = pl.num_programs(1)
    step = pl.program_id(0) * nl + pl.program_id(1)
    n_steps = pl.num_programs(0) * nl
    tile = tile0 + step
    slot = step % 2
    lp = buf_ref.shape[1]

    def copy(s, local, src):
        return pltpu.make_async_copy(y_hbm.at[pl.ds(src, MOE_CHUNK)], buf_ref.at[s, pl.ds(local, MOE_CHUNK)],
                                     sem.at[s])

    @pl.when(step == 0)
    def _():
        buf_ref[...] = jnp.zeros_like(buf_ref)
        rows_ref[slot] = _for_each_chunk(tile, pc_ref, goff_ref, lambda a, b: copy(slot, a, b).start())

    @pl.when(step + 1 < n_steps)
    def _():
        rows_ref[1 - slot] = _for_each_chunk(tile + 1, pc_ref, goff_ref,
                                             lambda a, b: copy(1 - slot, a, b).start())

    n = pl.multiple_of(rows_ref[slot], MOE_CHUNK)
    pltpu.make_async_copy(y_hbm.at[pl.ds(0, n)], buf_ref.at[slot, pl.ds(0, n)], sem.at[slot]).wait()
    y_lo, y_hi = _unpack_halves(buf_ref[slot])
    zero = jnp.zeros((), F32)
    unsort = jnp.where(_one_hot_rows(ld_ref[0:1, :], lp), gate_ref[0:1, :],
                       jnp.where(_one_hot_rows(ld_ref[1:2, :], lp), gate_ref[1:2, :], zero)).astype(BF16)
    y = jnp.concatenate([_dot_tn(unsort, y_lo), _dot_tn(unsort, y_hi)], axis=1)
    g2 = mod_ref[:, :, 5 * d:6 * d]
    o_ref[...] = _layer_norm(alpha * x1_ref[...] + (1.0 + g2) * y.reshape(gb_, tl, d), lw_ref[...], lb_ref[...])


def _combine(pc, goff, ld, gate, y, x1, mod, lw, lb, gb_, tl, alpha, lp, tile0):
    bx, l, d = x1.shape
    tm = gb_ * tl
    nl = l // tl
    seq = pl.BlockSpec((gb_, tl, d), lambda b, j, pc, go: (b, j, 0))
    tok = pl.BlockSpec((2, tm), lambda b, j, pc, go: (0, tile0 + b * nl + j))
    return pl.pallas_call(
        functools.partial(_combine_kernel, alpha=alpha, tile0=tile0),
        out_shape=jax.ShapeDtypeStruct((bx, l, d), F32),
        grid_spec=pltpu.PrefetchScalarGridSpec(
            num_scalar_prefetch=2,
            grid=(bx // gb_, nl),
            in_specs=[tok, tok,
                      pl.BlockSpec(memory_space=pl.ANY),
                      seq,
                      pl.BlockSpec((gb_, 1, mod.shape[-1]), lambda b, j, pc, go: (b, 0, 0)),
                      _resident(lw.shape), _resident(lb.shape)],
            out_specs=seq,
            scratch_shapes=[pltpu.VMEM((2, lp, d // 2), jnp.int32), pltpu.SMEM((2,), jnp.int32),
                            pltpu.SemaphoreType.DMA((2,))]),
        compiler_params=_params("arbitrary", "arbitrary"),
        name="moe_combine",
    )(pc, goff, ld, gate, y, x1, mod, lw, lb)


def _moe(h2s, eid, gate, groups, w1, w2, lw, lb, alpha, t, tb):
    m = eid.shape[1]
    n_tiles = m // t
    lp = 2 * t + N_EXPERTS * MOE_CHUNK
    ld, pc, pre = _plan(eid, t)
    pc = pc.reshape(n_tiles, N_EXPERTS)
    pre = pre.reshape(n_tiles, N_EXPERTS)
    total = pre[-1] + pc[-1]
    region = (total + tb - 1) // tb * tb
    gend = jnp.cumsum(region)
    goff = (gend - region)[None, :] + pre
    n_blocks = -(-(2 * m + n_tiles * N_EXPERTS * (MOE_CHUNK - 1) + N_EXPERTS * (tb - 1)) // tb)
    block_row0 = jnp.arange(n_blocks, dtype=jnp.int32) * tb
    block_e = jnp.minimum(jnp.sum((gend[None, :] <= block_row0[:, None]).astype(jnp.int32), axis=1),
                          N_EXPERTS - 1).astype(jnp.int32)
    n_valid = (gend[-1:] // tb).astype(jnp.int32)
    tails = jnp.concatenate([gend - region + total, (region - total) // MOE_CHUNK,
                             gend[-1:], n_blocks - gend[-1:] // tb]).astype(jnp.int32)
    pc = pc.reshape(-1).astype(jnp.int32)
    goff = goff.reshape(-1).astype(jnp.int32)
    xs = _dispatch(pc, goff, tails, ld, h2s, n_blocks * tb, t, lp, tb)
    y = _experts(block_e, n_valid, xs, w1, w2, tb)
    return [_combine(pc, goff, ld, gate, y, x1, mod, lw, lb, gb_, tl, alpha, lp, tile0)
            for x1, mod, gb_, tl, tile0 in groups]


def _rotary_tables(pos, reps):
    half = RET_QK_DIM // 2
    inv = ROPE_BASE ** (-jnp.linspace(0.0, 1.0, half, dtype=F32))
    ang = pos.astype(F32)[:, None] * inv[None, :]
    cos = jnp.cos(ang)
    sin = jnp.sin(ang)
    c2 = jnp.concatenate([cos, cos], axis=1)
    s2 = jnp.concatenate([-sin, sin], axis=1)
    kscale = RET_QK_DIM ** -0.5
    tabs = (c2, s2, c2 * kscale, s2 * kscale)
    return tuple(jnp.tile(t, (reps, 1)) for t in tabs)


def _pick(n, pref):
    t = min(n, pref)
    while n % t:
        t //= 2
    return t


def _row_tile(x, cached):
    bx, l, _ = x.shape
    return (_pick(bx, max(1, ROW_TILE // l)), l) if cached else (1, _pick(l, ROW_TILE))


def _mix(x, mod, pos, weights, log_gamma, cache=None):
    (w2, b2, gn_w, gn_b) = weights[:4]
    bx, l, d = x.shape
    gb_, tl = _row_tile(x, cache is not None)
    tabs = _rotary_tables(pos, gb_ if cache is not None else 1)
    rq, rk, rv, rg, fq, fk, fv, fkb, fvb, lf, ga, gb = _inproj(x, mod, w2, b2, tabs, gb_, tl)

    lft = jnp.swapaxes(lf[:, :, :FOX_HEADS], 1, 2)
    if cache is None:
        lc = _pick(l, RET_CHUNK)
        y_ret, state = _retention(rq, rk, rv, rg, gn_w, gn_b, log_gamma, lc)
        fp = _fcum(lft, _pick(l, FCUM_TILE))
        y_fox = _fox_prompt(fq, fkb, fp, fvb, _pick(l, ATTN_TILE))
    else:
        state0, cache_k, cache_v, cache_logf = cache
        y_ret, state = _retention(rq, rk, rv, rg, gn_w, gn_b, log_gamma, l, state0)
        past = cache_k.shape[1]
        total = -(-(past + l) // LANES) * LANES
        lft_all = jnp.concatenate(
            [jnp.swapaxes(cache_logf.astype(F32), 1, 2), lft,
             jnp.zeros((bx, FOX_HEADS, total - past - l), F32)], axis=2)
        fp = _fcum(lft_all, total)
        y_fox = _fox_sample(fq, fkb, fvb, cache_k.reshape(bx, past, FOX_W), cache_v.reshape(bx, past, FOX_W), fp)

    return (y_ret, y_fox, ga, gb), (state, fk, fv, lf[:, :, :FOX_HEADS])


def _layer(xs, mods, positions, caches, weights, log_gamma, alpha):
    (wr, wf, wo, ln1w, ln1b, rw3, rb, we1, we2, ln2w, ln2b) = weights[4:]
    tiles = [_row_tile(x, c is not None) for x, c in zip(xs, caches)]
    t = tiles[0][0] * tiles[0][1]
    assert all(gb_ * tl == t for gb_, tl in tiles), "request groups must share one MoE tile size"
    tile0, groups, h2s, eids, gates, extras = 0, [], [], [], [], []
    for x, mod, pos, cache, (gb_, tl) in zip(xs, mods, positions, caches, tiles):
        branches, extra = _mix(x, mod, pos, weights, log_gamma, cache)
        x1, h2, eid, gate = _outproj(*branches, x, mod, wr, wf, wo, ln1w, ln1b, rw3, rb, gb_, tl, alpha)
        groups.append((x1, mod, gb_, tl, tile0))
        h2s.append(h2); eids.append(eid); gates.append(gate); extras.append(extra)
        tile0 += x.shape[0] * x.shape[1] // t
    outs = _moe(h2s, jnp.concatenate(eids, axis=1), jnp.concatenate(gates, axis=1), groups,
                we1, we2, ln2w, ln2b, alpha, t, tb=EXPERT_ROWS)
    return outs, extras


def kernel(x_prompt, x_sample, state_ret, cache_fox_k, cache_fox_v, cache_fox_logf, c_prompt, c_sample,
           w_ada, b_ada, w_in, b_in, ret_gn_w, ret_gn_b, w_ret_proj, w_fox_proj, w_o, ln1_w, ln1_b,
           w_rg, b_rg, w_re, b_re, w_e_in, w_e_out, ln2_w, ln2_b):
    depth = w_ada.shape[0]
    d = x_prompt.shape[-1]
    bp, s, _ = x_prompt.shape
    bs, ls, _ = x_sample.shape
    past = cache_fox_k.shape[2]
    alpha = (2 * depth) ** 0.25
    log_gamma = jnp.log1p(-jnp.exp(jnp.linspace(math.log(1.0 / 32), math.log(1.0 / 512), RET_HEADS, dtype=F32)))
    pos_p = jnp.arange(s, dtype=jnp.int32)
    pos_s = past + jnp.arange(ls, dtype=jnp.int32)
    fg = 2 * RET_QK_W + 2 * RET_V_W + 3 * FOX_W

    xp, xs = x_prompt, x_sample
    p_ret, p_k, p_v, p_f, s_ret, s_k, s_v, s_f = [], [], [], [], [], [], [], []
    for li in range(depth):
        w2 = jnp.concatenate([w_in[li][:, :fg], w_in[li][:, fg + FOX_HEADS:],
                              jnp.pad(w_in[li][:, fg:fg + FOX_HEADS], ((0, 0), (0, LANES - FOX_HEADS)))],
                             axis=1).astype(BF16)
        b2 = jnp.concatenate([b_in[li][:fg], b_in[li][fg + FOX_HEADS:],
                              jnp.pad(b_in[li][fg:fg + FOX_HEADS], (0, LANES - FOX_HEADS))]).reshape(1, -1)
        n_rt = N_EXPERTS + N_GROUPS
        rt_rows = -(-n_rt // 8) * 8
        rwt = jnp.pad(jnp.concatenate([w_re[li], w_rg[li]], axis=1).T.astype(F32), ((0, rt_rows - n_rt), (0, 0)))
        r_hi = rwt.astype(BF16)
        r_mid = (rwt - r_hi.astype(F32)).astype(BF16)
        rw3 = jnp.stack([r_hi, r_mid])
        rb = jnp.pad(jnp.concatenate([b_re[li], b_rg[li]]).astype(F32), (0, rt_rows - n_rt)).reshape(rt_rows, 1)
        weights = (w2, b2, ret_gn_w[li], ret_gn_b[li],
                   w_ret_proj[li].astype(BF16), w_fox_proj[li].astype(BF16), w_o[li].astype(BF16),
                   ln1_w[li].reshape(1, d), ln1_b[li].reshape(1, d), rw3, rb,
                   w_e_in[li], w_e_out[li],
                   ln2_w[li].reshape(1, d), ln2_b[li].reshape(1, d))
        mod = _ada(jnp.concatenate([c_prompt, c_sample], axis=0), w_ada[li], b_ada[li])
        mod_p = mod[:bp].reshape(bp, 1, 6 * d)
        mod_s = mod[bp:].reshape(bs, 1, 6 * d)
        cache = (state_ret[li].astype(F32), cache_fox_k[li], cache_fox_v[li], cache_fox_logf[li])
        (xp, xs), (extra_p, extra_s) = _layer([xp, xs], [mod_p, mod_s], [pos_p, pos_s], [None, cache],
                                              weights, log_gamma, alpha)
        st, kk, vv, ff = extra_p
        p_ret.append(st); p_k.append(kk.reshape(bp, s, FOX_HEADS, FOX_HEAD_DIM))
        p_v.append(vv.reshape(bp, s, FOX_HEADS, FOX_HEAD_DIM)); p_f.append(ff)
        st, kk, vv, ff = extra_s
        s_ret.append(st); s_k.append(kk.reshape(bs, ls, FOX_HEADS, FOX_HEAD_DIM))
        s_v.append(vv.reshape(bs, ls, FOX_HEADS, FOX_HEAD_DIM)); s_f.append(ff)
    return (xp, xs, jnp.stack(p_ret), jnp.stack(p_k), jnp.stack(p_v), jnp.stack(p_f),
            jnp.stack(s_ret), jnp.stack(s_k), jnp.stack(s_v), jnp.stack(s_f))
```
